```python
import math
import jax, jax.numpy as jnp
from jax import lax
import numpy as np

D_MODEL = 2048
BATCH = 2
SEQ = 8192
DEPTH = 2

CTX_LEN = 256
GRID_W = 64
EXPAND = 2
D_MIX = EXPAND * D_MODEL
W_A = D_MIX // 2
W_B = D_MIX - W_A
NH_A = 8
DV_A = W_A // NH_A
DQK_A = DV_A // 2
P_B = 64
H_B = W_B // P_B
G_B = 4
R_B = H_B // G_B
N_B = 128
D_CONV = 3
CHUNK = 128
QK_W = 2 * NH_A * DQK_A
XBC_W = W_B + 2 * G_B * N_B
IN_SIZES = (NH_A * DQK_A, NH_A * DQK_A, W_A, W_A, W_A, 4 * NH_A, XBC_W, 2 * H_B, W_B)
N_IN = sum(IN_SIZES)
LN_EPS = 1e-5
DEEPNORM_ALPHA = (2 * DEPTH) ** 0.25
DEEPNORM_BETA = (8 * DEPTH) ** -0.25

kernel_name = 'hybrid_mlstm_ssd_deepnorm_prefix_dit'


def layer_norm(x, g=None, b=None, eps=LN_EPS):
    x32 = x.astype(jnp.float32)
    mu = jnp.mean(x32, axis=-1, keepdims=True)
    var = jnp.mean(jnp.square(x32 - mu), axis=-1, keepdims=True)
    y = (x32 - mu) * lax.rsqrt(var + eps)
    if g is not None:
        y = y * g.astype(jnp.float32) + b.astype(jnp.float32)
    return y.astype(x.dtype)


def rms_norm(x, g, eps=LN_EPS):
    x32 = x.astype(jnp.float32)
    y = x32 * lax.rsqrt(jnp.mean(jnp.square(x32), axis=-1, keepdims=True) + eps)
    return (y * g.astype(jnp.float32)).astype(x.dtype)


def dwconv(x, w, b):
    k = w.shape[0]
    y = lax.conv_general_dilated(x, w[:, None, :].astype(x.dtype), window_strides=(1,),
                                 padding=[(k // 2, k // 2)],
                                 dimension_numbers=('NWC', 'WIO', 'NWC'),
                                 feature_group_count=x.shape[-1])
    return y + b


def to_colmajor(t):
    bsz, t_len, ch = t.shape
    rows = t_len // GRID_W
    return t.reshape(bsz, rows, GRID_W, ch).transpose(0, 2, 1, 3).reshape(bsz, t_len, ch)


def from_colmajor(t):
    bsz, t_len, ch = t.shape
    rows = t_len // GRID_W
    return t.reshape(bsz, GRID_W, rows, ch).transpose(0, 2, 1, 3).reshape(bsz, t_len, ch)


def flip(t):
    return jnp.flip(t, axis=1)


def _chunk(t, nc):
    return t.reshape(t.shape[0], nc, CHUNK, *t.shape[2:]).swapaxes(0, 1)


def _unchunk(ys):
    ys = ys.swapaxes(0, 1)
    return ys.reshape(ys.shape[0], ys.shape[1] * CHUNK, *ys.shape[3:])


def mlstm_scan(q, k, v, li, lf, state, need_out):
    f32 = jnp.float32
    q, k, v, li, lf = (t.astype(f32) for t in (q, k, v, li, lf))
    nc = q.shape[1] // CHUNK
    mask = jnp.tril(jnp.ones((CHUNK, CHUNK), bool))

    def body(carry, inp):
        c_prev, n_prev, m_prev = carry
        qc, kc, vc, lic, lfc = inp
        b = jnp.cumsum(lfc, axis=1)
        g = b[:, -1]
        a = g[:, None] - b + lic
        m_new = jnp.maximum(g + m_prev, jnp.max(a, axis=1))
        w = jnp.exp(a - m_new[:, None])
        decay = jnp.exp(g + m_prev - m_new)
        c_new = decay[..., None, None] * c_prev + jnp.einsum('bsh,bshk,bshv->bhkv', w, kc, vc)
        n_new = decay[..., None] * n_prev + jnp.einsum('bsh,bshk->bhk', w, kc)
        if not need_out:
            return (c_new, n_new, m_new), None
        dmat = b[:, :, None, :] - b[:, None, :, :] + lic[:, None, :, :]
        dmat = jnp.where(mask[None, :, :, None], dmat, -jnp.inf)
        inter = b + m_prev[:, None]
        m_t = jnp.maximum(inter, jnp.max(dmat, axis=2))
        s = jnp.einsum('bthk,bshk->btsh', qc, kc) * jnp.exp(dmat - m_t[:, :, None])
        w_inter = jnp.exp(inter - m_t)
        num = (w_inter[..., None] * jnp.einsum('bthk,bhkv->bthv', qc, c_prev)
               + jnp.einsum('btsh,bshv->bthv', s, vc))
        den = w_inter * jnp.einsum('bthk,bhk->bth', qc, n_prev) + jnp.sum(s, axis=2)
        h = num / jnp.maximum(jnp.abs(den), jnp.exp(-m_t))[..., None]
        return (c_new, n_new, m_new), h

    chunks = tuple(_chunk(t, nc) for t in (q, k, v, li, lf))
    state, hs = lax.scan(body, state, chunks)
    h = _unchunk(hs) if need_out else None
    return h, state


def ssd_scan(xs, bm, cm, dt, a, state, need_out):
    f32 = jnp.float32
    xs, bm, cm, dt = (t.astype(f32) for t in (xs, bm, cm, dt))
    da = dt * a.astype(f32)
    nc = xs.shape[1] // CHUNK
    mask = jnp.tril(jnp.ones((CHUNK, CHUNK), bool))

    def body(s_prev, inp):
        xc, bc, cc, dtc, dac = inp
        acum = jnp.cumsum(dac, axis=1)
        total = acum[:, -1]
        w_state = jnp.exp(total[:, None] - acum) * dtc
        s_new = (jnp.exp(total)[..., None, None] * s_prev
                 + jnp.einsum('bsgr,bsgn,bsgrp->bgrpn', w_state, bc, xc))
        if not need_out:
            return s_new, None
        seg = acum[:, :, None] - acum[:, None]
        decay = jnp.exp(jnp.where(mask[None, :, :, None, None], seg, -jnp.inf))
        cb = jnp.einsum('btgn,bsgn->btsg', cc, bc)
        w_intra = cb[..., None] * decay * dtc[:, None]
        y = (jnp.einsum('btsgr,bsgrp->btgrp', w_intra, xc)
             + jnp.exp(acum)[..., None] * jnp.einsum('btgn,bgrpn->btgrp', cc, s_prev))
        return s_new, y

    chunks = tuple(_chunk(t, nc) for t in (xs, bm, cm, dt, da))
    state, ys = lax.scan(body, state, chunks)
    y = _unchunk(ys) if need_out else None
    return y, state


def zero_states(bsz):
    f32 = jnp.float32
    m_state = (jnp.zeros((bsz, NH_A, DQK_A, DV_A), f32), jnp.zeros((bsz, NH_A, DQK_A), f32),
               jnp.zeros((bsz, NH_A), f32))
    s_state = jnp.zeros((bsz, G_B, R_B, P_B, N_B), f32)
    return (m_state, m_state, s_state, s_state)


def mixer(u, w_in, conv_qk_w, conv_qk_b, gate_b, mh_w, conv_xbc_w, conv_xbc_b, dt_bias, a_log,
          d_skip, ssm_w, states, latent, need_out):
    bsz, t_len, _ = u.shape
    p = u @ w_in
    idx = np.cumsum(IN_SIZES)[:-1].tolist()
    q_raw, k_raw, v, o, z_a, g_raw, xbc, dt_raw, z_b = jnp.split(p, idx, axis=-1)
    st_mf, st_mb, st_sf, st_sb = states

    qk = jax.nn.silu(dwconv(jnp.concatenate([q_raw, k_raw], axis=-1), conv_qk_w, conv_qk_b))
    q = qk[..., :NH_A * DQK_A].reshape(bsz, t_len, NH_A, DQK_A) * (DQK_A ** -0.5)
    k = qk[..., NH_A * DQK_A:].reshape(bsz, t_len, NH_A, DQK_A)
    v = v.reshape(bsz, t_len, NH_A, DV_A)
    g = g_raw.reshape(bsz, t_len, 4, NH_A) + gate_b
    li_f, lf_f = g[:, :, 0], jax.nn.log_sigmoid(g[:, :, 1])
    li_b, lf_b = g[:, :, 2], jax.nn.log_sigmoid(g[:, :, 3])
    h_f, st_mf = mlstm_scan(q, k, v, li_f, lf_f, st_mf, need_out)
    h_b, st_mb = mlstm_scan(flip(q), flip(k), flip(v), flip(li_b), flip(lf_b), st_mb, need_out)

    if latent:
        xbc, dt_raw = to_colmajor(xbc), to_colmajor(dt_raw)
    xbc = jax.nn.silu(dwconv(xbc, conv_xbc_w, conv_xbc_b))
    xs = xbc[..., :W_B].reshape(bsz, t_len, G_B, R_B, P_B)
    bm = xbc[..., W_B:W_B + G_B * N_B].reshape(bsz, t_len, G_B, N_B)
    cm = xbc[..., W_B + G_B * N_B:].reshape(bsz, t_len, G_B, N_B)
    dt = jax.nn.softplus(dt_raw.reshape(bsz, t_len, 2, G_B, R_B) + dt_bias.reshape(2, G_B, R_B))
    a = -jnp.exp(a_log.reshape(2, G_B, R_B))
    y_f, st_sf = ssd_scan(xs, bm, cm, dt[:, :, 0], a[0], st_sf, need_out)
    y_b, st_sb = ssd_scan(flip(xs), flip(bm), flip(cm), flip(dt[:, :, 1]), a[1], st_sb, need_out)
    new_states = (st_mf, st_mb, st_sf, st_sb)
    if not need_out:
        return None, new_states

    h = layer_norm(h_f + flip(h_b)).reshape(bsz, t_len, W_A).astype(u.dtype) * mh_w
    y_a = jax.nn.sigmoid(o) * h * jax.nn.silu(z_a)
    y = (y_f + flip(y_b)).astype(u.dtype) + d_skip.reshape(G_B, R_B)[..., None] * xs
    y = y.reshape(bsz, t_len, W_B)
    if latent:
        y = from_colmajor(y)
    y_b_out = rms_norm(y * jax.nn.silu(z_b), ssm_w)
    return jnp.concatenate([y_a, y_b_out], axis=-1), new_states


def setup_inputs(seed: int = 0) -> dict:
    key = jax.random.key(seed)
    ks = jax.random.split(key, 24)
    f32 = jnp.float32

    def nrm(k, shape, s):
        return s * jax.random.normal(k, shape, f32)

    x = nrm(ks[0], (BATCH, SEQ, D_MODEL), 1.0)
    c = nrm(ks[1], (BATCH, D_MODEL), 1.0)
    ctx = nrm(ks[2], (BATCH, CTX_LEN, D_MODEL), 1.0)
    c_ctx = nrm(ks[3], (D_MODEL,), 1.0)
    w_ada = nrm(ks[4], (DEPTH, D_MODEL, 3 * D_MODEL), D_MODEL ** -0.5)
    b_ada = nrm(ks[5], (DEPTH, 3 * D_MODEL), 0.02)
    w_in = nrm(ks[6], (DEPTH, D_MODEL, N_IN), D_MODEL ** -0.5)
    conv_qk_w = nrm(ks[7], (DEPTH, D_CONV, QK_W), D_CONV ** -0.5)
    conv_qk_b = nrm(ks[8], (DEPTH, QK_W), 0.02)
    i_bias = nrm(ks[9], (DEPTH, 2, NH_A), 0.1)
    f_bias = jnp.linspace(3.0, 6.0, NH_A, dtype=f32) + nrm(ks[10], (DEPTH, 2, NH_A), 0.1)
    gate_b = jnp.stack([i_bias[:, 0], f_bias[:, 0], i_bias[:, 1], f_bias[:, 1]], axis=1)
    mh_norm_w = 1.0 + nrm(ks[11], (DEPTH, W_A), 0.02)
    conv_xbc_w = nrm(ks[12], (DEPTH, D_CONV, XBC_W), D_CONV ** -0.5)
    conv_xbc_b = nrm(ks[13], (DEPTH, XBC_W), 0.02)
    dt0 = jnp.exp(jax.random.uniform(ks[14], (DEPTH, 2, H_B), f32, math.log(1e-3), math.log(1e-1)))
    dt_bias = dt0 + jnp.log(-jnp.expm1(-dt0))
    a_log = jnp.log(jax.random.uniform(ks[15], (DEPTH, 2, H_B), f32, 1.0, 16.0))
    d_skip = 1.0 + nrm(ks[16], (DEPTH, H_B), 0.1)
    ssm_norm_w = 1.0 + nrm(ks[17], (DEPTH, W_B), 0.02)
    w_out = nrm(ks[18], (DEPTH, D_MIX, D_MODEL), D_MIX ** -0.5 * DEEPNORM_BETA)
    ln_g = 1.0 + nrm(ks[19], (DEPTH, D_MODEL), 0.02)
    ln_b = nrm(ks[20], (DEPTH, D_MODEL), 0.02)
    return {'x': x, 'c': c, 'ctx': ctx, 'c_ctx': c_ctx, 'w_ada': w_ada, 'b_ada': b_ada,
            'w_in': w_in, 'conv_qk_w': conv_qk_w, 'conv_qk_b': conv_qk_b, 'gate_b': gate_b,
            'mh_norm_w': mh_norm_w, 'conv_xbc_w': conv_xbc_w, 'conv_xbc_b': conv_xbc_b,
            'dt_bias': dt_bias, 'a_log': a_log, 'd_skip': d_skip, 'ssm_norm_w': ssm_norm_w,
            'w_out': w_out, 'ln_g': ln_g, 'ln_b': ln_b}


def reference(x, c, ctx, c_ctx, w_ada, b_ada, w_in, conv_qk_w, conv_qk_b, gate_b, mh_norm_w,
              conv_xbc_w, conv_xbc_b, dt_bias, a_log, d_skip, ssm_norm_w, w_out, ln_g, ln_b):
    xc = ctx
    for l in range(DEPTH):
        last = l == DEPTH - 1
        params = (w_in[l], conv_qk_w[l], conv_qk_b[l], gate_b[l], mh_norm_w[l], conv_xbc_w[l],
                  conv_xbc_b[l], dt_bias[l], a_log[l], d_skip[l], ssm_norm_w[l])
        shift, scale, gate = jnp.split(jax.nn.silu(c) @ w_ada[l] + b_ada[l], 3, axis=-1)
        shift_c, scale_c, gate_c = jnp.split(jax.nn.silu(c_ctx) @ w_ada[l] + b_ada[l], 3, axis=-1)
        uc = layer_norm(xc) * (1.0 + scale_c) + shift_c
        yc, ctx_states = mixer(uc, *params, zero_states(x.shape[0]), latent=False, need_out=not last)
        u = layer_norm(x) * (1.0 + scale[:, None]) + shift[:, None]
        y, _ = mixer(u, *params, ctx_states, latent=True, need_out=True)
        x = layer_norm(DEEPNORM_ALPHA * x + gate[:, None] * (y @ w_out[l]), ln_g[l], ln_b[l])
        if not last:
            xc = layer_norm(DEEPNORM_ALPHA * xc + gate_c * (yc @ w_out[l]), ln_g[l], ln_b[l])
    return x
```

```python
import functools

import jax
import jax.numpy as jnp
from jax import lax
from jax.experimental import pallas as pl
from jax.experimental.pallas import tpu as pltpu

F32 = jnp.float32
BF16 = jnp.bfloat16

D_MODEL = 2048
DEPTH = 2
GRID_W = 64
NH_A = 8
DV_A = 256
DQK_A = 128
W_A = NH_A * DV_A
P_B = 64
H_B = 32
G_B = 4
R_B = H_B // G_B
N_B = 128
W_B = H_B * P_B
GP_B = R_B * P_B
CHUNK = 128
LN_EPS = 1e-5
DEEPNORM_ALPHA = (2 * DEPTH) ** 0.25

CW = 1024
COL_V, COL_O, COL_ZA, COL_Q, COL_K = 0, 2, 4, 6, 7
N_MAIN_A = 8 * CW
COL_ZB, COL_X, COL_BC = 0, 2, 4
N_MAIN_B = 5 * CW
N_SMALL = 128

HALO = 8
VMEM_LIMIT = 56 * 1024 * 1024

PROJ_DTYPE = F32
HB_DTYPE = F32


def _cparams(sem):
    return pltpu.CompilerParams(dimension_semantics=sem, vmem_limit_bytes=VMEM_LIMIT)


def _softplus(x):
    return jnp.maximum(x, 0.0) + jnp.log1p(jnp.exp(-jnp.abs(x)))


def _log_sigmoid(x):
    return jnp.minimum(x, 0.0) - jnp.log1p(jnp.exp(-jnp.abs(x)))


def _silu(x):
    return x * jax.nn.sigmoid(x)


def _dot(a, b):
    return jnp.dot(a, b, preferred_element_type=F32)


def _dot_nt(a, b):
    return lax.dot_general(a, b, (((1,), (1,)), ((), ())), preferred_element_type=F32)


def _dot_tn(a, b):
    return lax.dot_general(a, b, (((0,), (0,)), ((), ())), preferred_element_type=F32)


def _cumsum_rows(tri, x):
    return jnp.dot(tri, x, preferred_element_type=F32, precision=lax.Precision.HIGHEST)


def _ada_kernel(c_ref, w_ref, b_ref, o_ref):
    s = _silu(c_ref[...])
    o_ref[0] = jnp.dot(s, w_ref[0], preferred_element_type=F32,
                       precision=lax.Precision.HIGHEST) + b_ref[0]


def _ada_call(cvec, w_ada, b_ada):
    depth, d, n3 = w_ada.shape
    tn = 1024
    return pl.pallas_call(
        _ada_kernel,
        grid=(depth, n3 // tn),
        in_specs=[pl.BlockSpec((8, d), lambda l, j: (0, 0)),
                  pl.BlockSpec((1, d, tn), lambda l, j: (l, 0, j)),
                  pl.BlockSpec((1, 1, tn), lambda l, j: (l, 0, j))],
        out_specs=pl.BlockSpec((1, 8, tn), lambda l, j: (l, 0, j)),
        out_shape=jax.ShapeDtypeStruct((depth, 8, n3), F32),
        compiler_params=_cparams(("arbitrary", "arbitrary")),
        name="adaln_mod",
    )(cvec, w_ada, b_ada.reshape(depth, 1, n3))


def _in_proj_kernel(x_ref, shift_ref, scale_ref, w_ref, ws_ref, p_ref, small_ref, u_ref):
    @pl.when(pl.program_id(1) == 0)
    def _():
        x = x_ref[...]
        mu = jnp.mean(x, axis=-1, keepdims=True)
        xc = x - mu
        var = jnp.mean(xc * xc, axis=-1, keepdims=True)
        u = xc * lax.rsqrt(var + LN_EPS) * (1.0 + scale_ref[0]) + shift_ref[0]
        ub = u.astype(BF16)
        u_ref[...] = ub
        small_ref[...] = _dot(ub, ws_ref[...])

    p_ref[...] = _dot(u_ref[...], w_ref[...]).astype(p_ref.dtype)


def _in_proj_call(x2d, shift, scale, w_main, w_small, mod_index, tm):
    rows, d = x2d.shape
    n_main = w_main.shape[1]
    tn = CW
    return pl.pallas_call(
        _in_proj_kernel,
        grid=(rows // tm, n_main // tn),
        in_specs=[pl.BlockSpec((tm, d), lambda i, j: (i, 0)),
                  pl.BlockSpec((1, 1, d), lambda i, j: (mod_index(i), 0, 0)),
                  pl.BlockSpec((1, 1, d), lambda i, j: (mod_index(i), 0, 0)),
                  pl.BlockSpec((d, tn), lambda i, j: (0, j)),
                  pl.BlockSpec((d, N_SMALL), lambda i, j: (0, 0))],
        out_specs=[pl.BlockSpec((tm, tn), lambda i, j: (i, j)),
                   pl.BlockSpec((tm, N_SMALL), lambda i, j: (i, 0))],
        out_shape=[jax.ShapeDtypeStruct((rows, n_main), PROJ_DTYPE),
                   jax.ShapeDtypeStruct((rows, N_SMALL), F32)],
        scratch_shapes=[pltpu.VMEM((tm, d), BF16)],
        compiler_params=_cparams(("arbitrary", "arbitrary")),
        name="in_proj",
    )(x2d, shift, scale, w_main, w_small)


def _out_proj_kernel(ya_ref, yb_ref, x_ref, gate_ref, wa_ref, wb_ref, g_ref, b_ref, o_ref):
    acc = _dot(ya_ref[...], wa_ref[...]) + _dot(yb_ref[...], wb_ref[...])
    r = DEEPNORM_ALPHA * x_ref[...] + gate_ref[0] * acc
    mu = jnp.mean(r, axis=-1, keepdims=True)
    rc = r - mu
    var = jnp.mean(rc * rc, axis=-1, keepdims=True)
    o_ref[...] = rc * lax.rsqrt(var + LN_EPS) * g_ref[...] + b_ref[...]


def _out_proj_call(ya, yb, x2d, gate, w_a, w_b, ln_g, ln_b, mod_index, tm):
    rows, d = x2d.shape
    const = lambda i: (0, 0)
    return pl.pallas_call(
        _out_proj_kernel,
        grid=(rows // tm,),
        in_specs=[pl.BlockSpec((tm, W_A), lambda i: (i, 0)),
                  pl.BlockSpec((tm, W_B), lambda i: (i, 0)),
                  pl.BlockSpec((tm, d), lambda i: (i, 0)),
                  pl.BlockSpec((1, 1, d), lambda i: (mod_index(i), 0, 0)),
                  pl.BlockSpec((W_A, d), const),
                  pl.BlockSpec((W_B, d), const),
                  pl.BlockSpec((1, d), const),
                  pl.BlockSpec((1, d), const)],
        out_specs=pl.BlockSpec((tm, d), lambda i: (i, 0)),
        out_shape=jax.ShapeDtypeStruct((rows, d), F32),
        compiler_params=_cparams(("arbitrary",)),
        name="out_proj",
    )(ya, yb, x2d, gate, w_a, w_b, ln_g.reshape(1, d), ln_b.reshape(1, d))


def _conv_silu(x, prev_row, next_row, w, bias):
    rows = x.shape[0]
    row = lax.broadcasted_iota(jnp.int32, x.shape, 0)
    xm = jnp.where(row == 0, prev_row, pltpu.roll(x, 1, 0))
    xp = jnp.where(row == rows - 1, next_row, pltpu.roll(x, rows - 1, 0))
    y = w[0:1] * xm + w[1:2] * x + w[2:3] * xp + bias
    return _silu(y)


def _tri_masks(reverse):
    t = lax.broadcasted_iota(jnp.int32, (CHUNK, CHUNK), 0)
    s = lax.broadcasted_iota(jnp.int32, (CHUNK, CHUNK), 1)
    mask = (s >= t) if reverse else (s <= t)
    return mask, mask.astype(F32)


def _chunk_pos(reverse):
    i = pl.program_id(1)
    nc = pl.num_programs(1)
    j = (nc - 1 - i) if reverse else i
    return i, j, nc


def _chunk_specs(t_len, reverse):
    nc = t_len // CHUNK
    nhb = t_len // HALO
    hpc = CHUNK // HALO

    def jj(i):
        return (nc - 1 - i) if reverse else i

    def main(col, width):
        return pl.BlockSpec((1, CHUNK, width), lambda b, i: (b, jj(i), (col * CW) // width))

    def prev(col):
        return pl.BlockSpec((1, HALO, CW), lambda b, i: (b, jnp.maximum(jj(i) * hpc - 1, 0), col))

    def nxt(col):
        return pl.BlockSpec((1, HALO, CW),
                            lambda b, i: (b, jnp.minimum((jj(i) + 1) * hpc, nhb - 1), col))

    def const2(shape):
        return pl.BlockSpec(shape, lambda b, i: (0, 0))

    return nc, main, prev, nxt, const2


def _mlstm_kernel(reverse, finalize, *refs):
    (q_ref, qp_ref, qn_ref, k_ref, kp_ref, kn_ref, v_ref, gs_ref, cw_ref, cb_ref, gb_ref,
     c0_ref, n0_ref, m0_ref) = refs[:14]
    if finalize:
        hb_ref, o_ref, z_ref, mhw_ref = refs[14:18]
        out_ref, cf_ref, nf_ref, mf_ref = refs[18:22]
    else:
        out_ref, cf_ref, nf_ref, mf_ref = refs[14:18]

    i, j, nc = _chunk_pos(reverse)

    @pl.when(i == 0)
    def _():
        cf_ref[...] = c0_ref[...]
        nf_ref[...] = n0_ref[...]
        mf_ref[...] = m0_ref[...]

    mask, tri = _tri_masks(reverse)
    has_prev = j > 0
    has_next = j < nc - 1
    edge = 0 if reverse else CHUNK - 1

    gb = gs_ref[0] + gb_ref[...]
    gbt = gb.T
    lf_all = _log_sigmoid(gb)
    b_all = _cumsum_rows(tri, lf_all)
    b_all_t = b_all.T
    io = 2 * NH_A if reverse else 0
    li_c = gb[:, io:io + NH_A]
    b_c = b_all[:, io + NH_A:io + 2 * NH_A]
    c_r = gbt[io:io + NH_A, :] - b_all_t[io + NH_A:io + 2 * NH_A, :]

    m_prev = mf_ref[0][:, 0:NH_A]
    g_c = b_c[edge:edge + 1, :]
    a_c = g_c - b_c + li_c
    m_new = jnp.maximum(g_c + m_prev, jnp.max(a_c, axis=0, keepdims=True))
    w_c = jnp.exp(a_c - m_new)
    dec = jnp.exp(g_c + m_prev - m_new)

    for h in range(NH_A):
        ks = slice(h * DQK_A, (h + 1) * DQK_A)
        vs = slice(h * DV_A, (h + 1) * DV_A)
        kk = slice(NH_A * DQK_A + h * DQK_A, NH_A * DQK_A + (h + 1) * DQK_A)
        q_prev = jnp.where(has_prev, qp_ref[0, HALO - 1:HALO, ks], 0.0).astype(F32)
        q_next = jnp.where(has_next, qn_ref[0, 0:1, ks], 0.0).astype(F32)
        k_prev = jnp.where(has_prev, kp_ref[0, HALO - 1:HALO, ks], 0.0).astype(F32)
        k_next = jnp.where(has_next, kn_ref[0, 0:1, ks], 0.0).astype(F32)
        qh = _conv_silu(q_ref[0, :, ks].astype(F32), q_prev, q_next,
                        cw_ref[:, ks], cb_ref[:, ks]) * (DQK_A ** -0.5)
        kh = _conv_silu(k_ref[0, :, ks].astype(F32), k_prev, k_next,
                        cw_ref[:, kk], cb_ref[:, kk])
        qb = qh.astype(BF16)
        kb = kh.astype(BF16)
        vb = v_ref[0, :, vs].astype(BF16)

        s_qk = _dot_nt(qb, kb)
        cm = jnp.where(mask, c_r[h:h + 1, :], -jnp.inf)
        mp = m_prev[:, h:h + 1]
        mm = jnp.maximum(jnp.max(cm, axis=1, keepdims=True), mp)
        p = s_qk * jnp.exp(cm - mm)
        w_int = jnp.exp(mp - mm)
        c_prev = cf_ref[0, h]
        n_prev = nf_ref[0, h:h + 1, :]
        num = w_int * _dot(qb, c_prev.astype(BF16)) + _dot(p.astype(BF16), vb)
        den = (w_int * jnp.sum(qh * n_prev, axis=1, keepdims=True)
               + jnp.sum(p, axis=1, keepdims=True))
        m_t = b_c[:, h:h + 1] + mm
        hval = num / jnp.maximum(jnp.abs(den), jnp.exp(-m_t))

        kw = kh * w_c[:, h:h + 1]
        dh = dec[:, h:h + 1]
        cf_ref[0, h] = dh * c_prev + _dot_tn(kw.astype(BF16), vb)
        nf_ref[0, h:h + 1, :] = dh * n_prev + jnp.sum(kw, axis=0, keepdims=True)

        if finalize:
            hs = hval + hb_ref[0, :, vs].astype(F32)
            mu = jnp.mean(hs, axis=-1, keepdims=True)
            hc = hs - mu
            var = jnp.mean(hc * hc, axis=-1, keepdims=True)
            hn = hc * lax.rsqrt(var + LN_EPS) * mhw_ref[:, vs]
            ya = jax.nn.sigmoid(o_ref[0, :, vs].astype(F32)) * hn * _silu(z_ref[0, :, vs].astype(F32))
            out_ref[0, :, vs] = ya.astype(out_ref.dtype)
        else:
            out_ref[0, :, vs] = hval.astype(out_ref.dtype)

    mf_ref[0, :, 0:NH_A] = m_new


def _mlstm_call(p3, small3, conv_w, conv_b, gate_b128, state, reverse, finalize,
                hb=None, mh_w=None):
    bsz, t_len, _ = p3.shape
    nc, main, prev, nxt, const2 = _chunk_specs(t_len, reverse)
    c0, n0, m0 = state
    state_specs = [pl.BlockSpec((1, NH_A, DQK_A, DV_A), lambda b, i: (b, 0, 0, 0)),
                   pl.BlockSpec((1, NH_A, DQK_A), lambda b, i: (b, 0, 0)),
                   pl.BlockSpec((1, 1, 128), lambda b, i: (b, 0, 0))]
    in_specs = [main(COL_Q, CW), prev(COL_Q), nxt(COL_Q),
                main(COL_K, CW), prev(COL_K), nxt(COL_K),
                main(COL_V, W_A), main(0, N_SMALL),
                const2((3, 2 * CW)), const2((1, 2 * CW)), const2((1, N_SMALL))] + state_specs
    args = [p3, p3, p3, p3, p3, p3, p3, small3, conv_w, conv_b, gate_b128, c0, n0, m0]
    if finalize:
        in_specs += [main(0, W_A), main(COL_O, W_A), main(COL_ZA, W_A), const2((1, W_A))]
        args += [hb, p3, p3, mh_w]
    out_dtype = BF16 if finalize else HB_DTYPE
    out_shape = [jax.ShapeDtypeStruct((bsz, t_len, W_A), out_dtype),
                 jax.ShapeDtypeStruct(c0.shape, F32),
                 jax.ShapeDtypeStruct(n0.shape, F32),
                 jax.ShapeDtypeStruct(m0.shape, F32)]
    out, cf, nf, mf = pl.pallas_call(
        functools.partial(_mlstm_kernel, reverse, finalize),
        grid=(bsz, nc), in_specs=in_specs, out_specs=[main(0, W_A)] + state_specs,
        out_shape=out_shape,
        compiler_params=_cparams(("arbitrary", "arbitrary")),
        name="mlstm_" + ("bwd" if reverse else "fwd"),
    )(*args)
    return out, (cf, nf, mf)


def _ssd_kernel(reverse, finalize, *refs):
    (x0_ref, x0p_ref, x0n_ref, x1_ref, x1p_ref, x1n_ref, bc_ref, bcp_ref, bcn_ref,
     gs_ref, cw_ref, cb_ref, dtb_ref, alog_ref, s0_ref) = refs[:15]
    if finalize:
        yb_ref, z0_ref, z1_ref, dsk_ref, nw_ref = refs[15:20]
        out_ref, sf_ref, xs_s, y_s, xw_s = refs[20:25]
    else:
        out_ref, sf_ref, xs_s, y_s, xw_s = refs[15:20]

    i, j, nc = _chunk_pos(reverse)

    @pl.when(i == 0)
    def _():
        sf_ref[...] = s0_ref[...]

    mask, tri = _tri_masks(reverse)
    has_prev = j > 0
    has_next = j < nc - 1
    edge = 0 if reverse else CHUNK - 1

    def halo(p_ref, n_ref, cs):
        pr = jnp.where(has_prev, p_ref[0, HALO - 1:HALO, cs], 0.0).astype(F32)
        nx = jnp.where(has_next, n_ref[0, 0:1, cs], 0.0).astype(F32)
        return pr, nx

    dt_all = _softplus(gs_ref[0] + dtb_ref[...])
    da_all = dt_all * (-jnp.exp(alog_ref[...]))
    acum_all = _cumsum_rows(tri, da_all)
    do = H_B if reverse else 0
    dt_r = dt_all.T[do:do + H_B, :]
    acum_r = acum_all.T[do:do + H_B, :]
    dt_c = dt_all[:, do:do + H_B]
    acum_c = acum_all[:, do:do + H_B]
    total_c = acum_c[edge:edge + 1, :]
    ws_c = jnp.exp(total_c - acum_c) * dt_c
    eac_c = jnp.exp(acum_c)
    etot = jnp.exp(total_c)

    bc_w0 = W_B
    for half in range(2):
        cs = slice(half * GP_B, (half + 1) * GP_B)
        ws = slice(bc_w0 + half * GP_B, bc_w0 + (half + 1) * GP_B)
        pr, nx = halo(bcp_ref, bcn_ref, cs)
        val = _conv_silu(bc_ref[0, :, cs].astype(F32), pr, nx, cw_ref[:, ws], cb_ref[:, ws])
        if half == 0:
            bmat = val.astype(BF16)
        else:
            cmat = val.astype(BF16)

    for g in range(G_B):
        xr, xpr, xnr = (x0_ref, x0p_ref, x0n_ref) if g < 2 else (x1_ref, x1p_ref, x1n_ref)
        cs = slice((g % 2) * GP_B, (g % 2 + 1) * GP_B)
        gs = slice(g * GP_B, (g + 1) * GP_B)
        pr, nx = halo(xpr, xnr, cs)
        xg = _conv_silu(xr[0, :, cs].astype(F32), pr, nx, cw_ref[:, gs], cb_ref[:, gs])
        if finalize:
            xs_s[:, gs] = xg
        xgb = xg.astype(BF16)
        bg = bmat[:, g * N_B:(g + 1) * N_B]
        cg = cmat[:, g * N_B:(g + 1) * N_B]
        cb = _dot_nt(cg, bg)
        s_prev = sf_ref[0, g]
        cs_prev = _dot(cg, s_prev.astype(BF16))
        for r in range(R_B):
            h = g * R_B + r
            hs = slice(r * P_B, (r + 1) * P_B)
            seg = acum_c[:, h:h + 1] - acum_r[h:h + 1, :]
            decay = jnp.exp(jnp.where(mask, seg, -jnp.inf))
            w_intra = cb * decay * dt_r[h:h + 1, :]
            y_h = (_dot(w_intra.astype(BF16), xgb[:, hs])
                   + eac_c[:, h:h + 1] * cs_prev[:, hs])
            y_s[:, h * P_B:(h + 1) * P_B] = y_h
            xw_s[:, hs] = xg[:, hs] * ws_c[:, h:h + 1]
        upd = _dot_tn(bg, xw_s[...].astype(BF16))
        for r in range(R_B):
            h = g * R_B + r
            hs = slice(r * P_B, (r + 1) * P_B)
            sf_ref[0, g, :, hs] = etot[:, h:h + 1] * s_prev[:, hs] + upd[:, hs]

    if finalize:
        ss = jnp.zeros((CHUNK, 1), F32)
        for half in range(2):
            cs = slice(half * CW, (half + 1) * CW)
            z_ref = z0_ref if half == 0 else z1_ref
            y = y_s[:, cs] + yb_ref[0, :, cs].astype(F32) + dsk_ref[:, cs] * xs_s[:, cs]
            y = y * _silu(z_ref[0].astype(F32))
            y_s[:, cs] = y
            ss = ss + jnp.sum(y * y, axis=-1, keepdims=True)
        inv = lax.rsqrt(ss * (1.0 / W_B) + LN_EPS)
        for half in range(2):
            cs = slice(half * CW, (half + 1) * CW)
            out_ref[0, :, cs] = (y_s[:, cs] * inv * nw_ref[:, cs]).astype(out_ref.dtype)
    else:
        out_ref[0] = y_s[...].astype(out_ref.dtype)


def _ssd_call(p3, small3, conv_w, conv_b, dtb128, alog128, s0, reverse, finalize,
              yb=None, d_skip=None, norm_w=None):
    bsz, t_len, _ = p3.shape
    nc, main, prev, nxt, const2 = _chunk_specs(t_len, reverse)
    state_spec = pl.BlockSpec((1, G_B, N_B, GP_B), lambda b, i: (b, 0, 0, 0))
    in_specs = [main(COL_X, CW), prev(COL_X), nxt(COL_X),
                main(COL_X + 1, CW), prev(COL_X + 1), nxt(COL_X + 1),
                main(COL_BC, CW), prev(COL_BC), nxt(COL_BC),
                main(0, N_SMALL),
                const2((3, 3 * CW)), const2((1, 3 * CW)), const2((1, N_SMALL)),
                const2((1, N_SMALL)), state_spec]
    args = [p3] * 9 + [small3, conv_w, conv_b, dtb128, alog128, s0]
    if finalize:
        in_specs += [main(0, W_B), main(COL_ZB, CW), main(COL_ZB + 1, CW),
                     const2((1, W_B)), const2((1, W_B))]
        args += [yb, p3, p3, d_skip, norm_w]
    out_dtype = BF16 if finalize else HB_DTYPE
    out, sf = pl.pallas_call(
        functools.partial(_ssd_kernel, reverse, finalize),
        grid=(bsz, nc), in_specs=in_specs,
        out_specs=[main(0, W_B), state_spec],
        out_shape=[jax.ShapeDtypeStruct((bsz, t_len, W_B), out_dtype),
                   jax.ShapeDtypeStruct(s0.shape, F32)],
        scratch_shapes=[pltpu.VMEM((CHUNK, W_B), F32),
                        pltpu.VMEM((CHUNK, W_B), F32),
                        pltpu.VMEM((CHUNK, GP_B), F32)],
        compiler_params=_cparams(("arbitrary", "arbitrary")),
        name="ssd_" + ("bwd" if reverse else "fwd"),
    )(*args)
    return out, sf


def _to_colmajor(t, bsz):
    ch = t.shape[-1]
    return t.reshape(bsz, -1, GRID_W, ch).transpose(0, 2, 1, 3).reshape(-1, ch)


def _from_colmajor(t, bsz):
    ch = t.shape[-1]
    return t.reshape(bsz, GRID_W, -1, ch).transpose(0, 2, 1, 3).reshape(-1, ch)


def _mixer(x2d, bsz, t_len, latent, shift, scale, mod_index, tm, lp, states):
    st_mf, st_mb, st_sf, st_sb = states

    pa, sa = _in_proj_call(x2d, shift, scale, lp["w_main_a"], lp["w_small_a"], mod_index, tm)
    pa = pa.reshape(bsz, t_len, N_MAIN_A)
    sa = sa.reshape(bsz, t_len, N_SMALL)
    m_args = (pa, sa, lp["conv_qk_w"], lp["conv_qk_b"], lp["gate_b128"])
    h_b, st_mb = _mlstm_call(*m_args, st_mb, True, False)
    y_a, st_mf = _mlstm_call(*m_args, st_mf, False, True, hb=h_b, mh_w=lp["mh_w"])

    xs2d = _to_colmajor(x2d, bsz) if latent else x2d
    pb, sb = _in_proj_call(xs2d, shift, scale, lp["w_main_b"], lp["w_small_b"], mod_index, tm)
    pb = pb.reshape(bsz, t_len, N_MAIN_B)
    sb = sb.reshape(bsz, t_len, N_SMALL)
    s_args = (pb, sb, lp["conv_xbc_w"], lp["conv_xbc_b"], lp["dtb128"], lp["alog128"])
    y_bb, st_sb = _ssd_call(*s_args, st_sb, True, False)
    y_b, st_sf = _ssd_call(*s_args, st_sf, False, True, yb=y_bb, d_skip=lp["d_skip"],
                           norm_w=lp["ssm_w"])
    y_b = y_b.reshape(bsz * t_len, W_B)
    if latent:
        y_b = _from_colmajor(y_b, bsz)
    return y_a.reshape(bsz * t_len, W_A), y_b, (st_mf, st_mb, st_sf, st_sb)


def _zero_states(bsz):
    m_state = (jnp.zeros((bsz, NH_A, DQK_A, DV_A), F32), jnp.zeros((bsz, NH_A, DQK_A), F32),
               jnp.zeros((bsz, 1, 128), F32))
    s_state = jnp.zeros((bsz, G_B, N_B, GP_B), F32)
    return (m_state, m_state, s_state, s_state)


def _layer_params(l, w_in, conv_qk_w, conv_qk_b, gate_b, mh_norm_w, conv_xbc_w, conv_xbc_b,
                  dt_bias, a_log, d_skip, ssm_norm_w, w_out):
    w = w_in[l]
    nqk = NH_A * DQK_A
    o_q, o_k, o_v, o_o, o_za = 0, nqk, 2 * nqk, 2 * nqk + W_A, 2 * nqk + 2 * W_A
    o_g = o_za + W_A
    o_x = o_g + 4 * NH_A
    o_b = o_x + W_B
    o_c = o_b + G_B * N_B
    o_dt = o_c + G_B * N_B
    o_zb = o_dt + 2 * H_B
    cols = lambda a, n: w[:, a:a + n]
    w_main_a = jnp.concatenate(
        [cols(o_v, W_A), cols(o_o, W_A), cols(o_za, W_A), cols(o_q, nqk), cols(o_k, nqk)],
        axis=1).astype(BF16)
    w_main_b = jnp.concatenate(
        [cols(o_zb, W_B), cols(o_x, W_B), cols(o_b, G_B * N_B), cols(o_c, G_B * N_B)],
        axis=1).astype(BF16)

    def pad_cols(a):
        return jnp.concatenate([a, jnp.zeros((a.shape[0], N_SMALL - a.shape[1]), F32)], axis=1)

    return {
        "w_main_a": w_main_a, "w_main_b": w_main_b,
        "w_small_a": pad_cols(cols(o_g, 4 * NH_A)).astype(BF16),
        "w_small_b": pad_cols(cols(o_dt, 2 * H_B)).astype(BF16),
        "conv_qk_w": conv_qk_w[l], "conv_qk_b": conv_qk_b[l].reshape(1, -1),
        "gate_b128": pad_cols(gate_b[l].reshape(1, -1)),
        "mh_w": mh_norm_w[l].reshape(1, W_A),
        "conv_xbc_w": conv_xbc_w[l], "conv_xbc_b": conv_xbc_b[l].reshape(1, -1),
        "dtb128": pad_cols(dt_bias[l].reshape(1, -1)),
        "alog128": pad_cols(a_log[l].reshape(1, -1)),
        "d_skip": jnp.repeat(d_skip[l], P_B).reshape(1, W_B),
        "ssm_w": ssm_norm_w[l].reshape(1, W_B),
        "w_out_a": w_out[l, :W_A].astype(BF16), "w_out_b": w_out[l, W_A:].astype(BF16),
    }


def kernel(x, c, ctx, c_ctx, w_ada, b_ada, w_in, conv_qk_w, conv_qk_b, gate_b, mh_norm_w,
           conv_xbc_w, conv_xbc_b, dt_bias, a_log, d_skip, ssm_norm_w, w_out, ln_g, ln_b):
    bsz, t_len, d = x.shape
    ctx_len = ctx.shape[1]
    depth = w_in.shape[0]
    assert d == D_MODEL and depth == DEPTH and bsz + 1 <= 8
    assert t_len % (CHUNK * GRID_W) == 0 and ctx_len % CHUNK == 0

    cvec = jnp.concatenate([c, c_ctx[None, :], jnp.zeros((8 - bsz - 1, d), F32)], axis=0)
    mod = _ada_call(cvec, w_ada, b_ada)

    x2d = x.reshape(bsz * t_len, d)
    xc2d = ctx.reshape(bsz * ctx_len, d)
    tm_in, tm_out, tm_ctx = 1024, 256, 256
    lat_in = lambda i: i // (t_len // tm_in)
    lat_out = lambda i: i // (t_len // tm_out)
    ctx_index = lambda i: bsz

    for l in range(depth):
        last = l == depth - 1
        lp = _layer_params(l, w_in, conv_qk_w, conv_qk_b, gate_b, mh_norm_w, conv_xbc_w,
                           conv_xbc_b, dt_bias, a_log, d_skip, ssm_norm_w, w_out)
        shift = mod[l, :, 0:d].reshape(8, 1, d)
        scale = mod[l, :, d:2 * d].reshape(8, 1, d)
        gate = mod[l, :, 2 * d:3 * d].reshape(8, 1, d)

        yc_a, yc_b, ctx_states = _mixer(xc2d, bsz, ctx_len, False, shift, scale, ctx_index,
                                        tm_ctx, lp, _zero_states(bsz))
        y_a, y_b, _ = _mixer(x2d, bsz, t_len, True, shift, scale, lat_in, tm_in, lp, ctx_states)
        x2d = _out_proj_call(y_a, y_b, x2d, gate, lp["w_out_a"], lp["w_out_b"], ln_g[l], ln_b[l],
                             lat_out, tm_out)
        if not last:
            xc2d = _out_proj_call(yc_a, yc_b, xc2d, gate, lp["w_out_a"], lp["w_out_b"], ln_g[l],
                                  ln_b[l], ctx_index, tm_ctx)
    return x2d.reshape(bsz, t_len, d)
```

```python
import functools

import jax
import jax.numpy as jnp
from jax import lax
from jax.experimental import pallas as pl
from jax.experimental.pallas import tpu as pltpu

F32 = jnp.float32
BF16 = jnp.bfloat16

D_MODEL = 2048
DEPTH = 2
GRID_W = 64
NH_A = 8
DV_A = 256
DQK_A = 128
W_A = NH_A * DV_A
QK_A = NH_A * DQK_A
P_B = 64
H_B = 32
G_B = 4
R_B = H_B // G_B
N_B = 128
W_B = H_B * P_B
GP_B = R_B * P_B
NPAIR = H_B // 2
CHUNK = 128
HALF = CHUNK // 2
LN_EPS = 1e-5
DEEPNORM_ALPHA = (2 * DEPTH) ** 0.25
LOG2E = 1.4426950408889634

CW = 1024
COL_V, COL_O, COL_ZA, COL_Q, COL_K = 0, 2, 4, 6, 7
N_MAIN_A = 8 * CW
COL_ZB, COL_X, COL_BC = 0, 2, 4
N_MAIN_B = 5 * CW
N_SMALL = 128

HALO = 16
VMEM_LIMIT = 56 * 1024 * 1024

PROJ_DTYPE = BF16
HB_DTYPE = BF16


def _cparams(sem):
    return pltpu.CompilerParams(dimension_semantics=sem, vmem_limit_bytes=VMEM_LIMIT)


def _softplus(x):
    return jnp.maximum(x, 0.0) + jnp.log1p(jnp.exp(-jnp.abs(x)))


def _log_sigmoid(x):
    return jnp.minimum(x, 0.0) - jnp.log1p(jnp.exp(-jnp.abs(x)))


def _silu(x):
    return x * jax.nn.sigmoid(x)


def _dot(a, b):
    return jnp.dot(a, b, preferred_element_type=F32)


def _dot_nt(a, b):
    return lax.dot_general(a, b, (((1,), (1,)), ((), ())), preferred_element_type=F32)


def _dot_tn(a, b):
    return lax.dot_general(a, b, (((0,), (0,)), ((), ())), preferred_element_type=F32)


def _cumsum_rows(tri, x):
    return jnp.dot(tri, x, preferred_element_type=F32, precision=lax.Precision.HIGHEST)


def _ada_kernel(c_ref, w_ref, b_ref, o_ref):
    s = _silu(c_ref[...])
    o_ref[0] = jnp.dot(s, w_ref[0], preferred_element_type=F32,
                       precision=lax.Precision.HIGHEST) + b_ref[0]


def _ada_call(cvec, w_ada, b_ada):
    depth, d, n3 = w_ada.shape
    tn = 1024
    return pl.pallas_call(
        _ada_kernel,
        grid=(depth, n3 // tn),
        in_specs=[pl.BlockSpec((8, d), lambda l, j: (0, 0)),
                  pl.BlockSpec((1, d, tn), lambda l, j: (l, 0, j)),
                  pl.BlockSpec((1, 1, tn), lambda l, j: (l, 0, j))],
        out_specs=pl.BlockSpec((1, 8, tn), lambda l, j: (l, 0, j)),
        out_shape=jax.ShapeDtypeStruct((depth, 8, n3), F32),
        compiler_params=_cparams(("arbitrary", "arbitrary")),
        name="adaln_mod",
    )(cvec, w_ada, b_ada.reshape(depth, 1, n3))


def _in_proj_kernel(x_ref, shift_ref, scale_ref, w_ref, ws_ref, p_ref, small_ref, u_ref):
    @pl.when(pl.program_id(1) == 0)
    def _():
        x = x_ref[...]
        mu = jnp.mean(x, axis=-1, keepdims=True)
        xc = x - mu
        var = jnp.mean(xc * xc, axis=-1, keepdims=True)
        u = xc * lax.rsqrt(var + LN_EPS) * (1.0 + scale_ref[0]) + shift_ref[0]
        ub = u.astype(BF16)
        u_ref[...] = ub
        small_ref[...] = _dot(ub, ws_ref[...])

    p_ref[...] = _dot(u_ref[...], w_ref[...]).astype(p_ref.dtype)


def _in_proj_call(x2d, shift, scale, w_main, w_small, mod_index, tm):
    rows, d = x2d.shape
    n_main = w_main.shape[1]
    tn = CW
    return pl.pallas_call(
        _in_proj_kernel,
        grid=(rows // tm, n_main // tn),
        in_specs=[pl.BlockSpec((tm, d), lambda i, j: (i, 0)),
                  pl.BlockSpec((1, 1, d), lambda i, j: (mod_index(i), 0, 0)),
                  pl.BlockSpec((1, 1, d), lambda i, j: (mod_index(i), 0, 0)),
                  pl.BlockSpec((d, tn), lambda i, j: (0, j)),
                  pl.BlockSpec((d, N_SMALL), lambda i, j: (0, 0))],
        out_specs=[pl.BlockSpec((tm, tn), lambda i, j: (i, j)),
                   pl.BlockSpec((tm, N_SMALL), lambda i, j: (i, 0))],
        out_shape=[jax.ShapeDtypeStruct((rows, n_main), PROJ_DTYPE),
                   jax.ShapeDtypeStruct((rows, N_SMALL), F32)],
        scratch_shapes=[pltpu.VMEM((tm, d), BF16)],
        compiler_params=_cparams(("arbitrary", "arbitrary")),
        name="in_proj",
    )(x2d, shift, scale, w_main, w_small)


def _out_proj_kernel(ya_ref, yb_ref, x_ref, gate_ref, wa_ref, wb_ref, g_ref, b_ref, o_ref):
    acc = _dot(ya_ref[...], wa_ref[...]) + _dot(yb_ref[...], wb_ref[...])
    r = DEEPNORM_ALPHA * x_ref[...] + gate_ref[0] * acc
    mu = jnp.mean(r, axis=-1, keepdims=True)
    rc = r - mu
    var = jnp.mean(rc * rc, axis=-1, keepdims=True)
    o_ref[...] = rc * lax.rsqrt(var + LN_EPS) * g_ref[...] + b_ref[...]


def _out_proj_call(ya, yb, x2d, gate, w_a, w_b, ln_g, ln_b, mod_index, tm):
    rows, d = x2d.shape
    const = lambda i: (0, 0)
    return pl.pallas_call(
        _out_proj_kernel,
        grid=(rows // tm,),
        in_specs=[pl.BlockSpec((tm, W_A), lambda i: (i, 0)),
                  pl.BlockSpec((tm, W_B), lambda i: (i, 0)),
                  pl.BlockSpec((tm, d), lambda i: (i, 0)),
                  pl.BlockSpec((1, 1, d), lambda i: (mod_index(i), 0, 0)),
                  pl.BlockSpec((W_A, d), const),
                  pl.BlockSpec((W_B, d), const),
                  pl.BlockSpec((1, d), const),
                  pl.BlockSpec((1, d), const)],
        out_specs=pl.BlockSpec((tm, d), lambda i: (i, 0)),
        out_shape=jax.ShapeDtypeStruct((rows, d), F32),
        compiler_params=_cparams(("arbitrary",)),
        name="out_proj",
    )(ya, yb, x2d, gate, w_a, w_b, ln_g.reshape(1, d), ln_b.reshape(1, d))


def _conv_silu(x, prev_row, next_row, w, bias):
    rows = x.shape[0]
    row = lax.broadcasted_iota(jnp.int32, x.shape, 0)
    xm = jnp.where(row == 0, prev_row, pltpu.roll(x, 1, 0))
    xp = jnp.where(row == rows - 1, next_row, pltpu.roll(x, rows - 1, 0))
    y = w[0:1] * xm + w[1:2] * x + w[2:3] * xp + bias
    return _silu(y)


def _conv_block(x_ref, p_ref, n_ref, cs, has_prev, has_next, w, bias):
    pr = jnp.where(has_prev, p_ref[0, HALO - 1:HALO, cs].astype(F32), 0.0)
    nx = jnp.where(has_next, n_ref[0, 0:1, cs].astype(F32), 0.0)
    return _conv_silu(x_ref[0, :, cs].astype(F32), pr, nx, w, bias)


def _tri_masks(reverse):
    t = lax.broadcasted_iota(jnp.int32, (CHUNK, CHUNK), 0)
    s = lax.broadcasted_iota(jnp.int32, (CHUNK, CHUNK), 1)
    mask = (s >= t) if reverse else (s <= t)
    return mask, mask.astype(F32)


def _chunk_pos(reverse):
    i = pl.program_id(1)
    nc = pl.num_programs(1)
    j = (nc - 1 - i) if reverse else i
    return i, j, nc


def _chunk_specs(t_len, reverse):
    nc = t_len // CHUNK
    nhb = t_len // HALO
    hpc = CHUNK // HALO

    def jj(i):
        return (nc - 1 - i) if reverse else i

    def main(col, width):
        return pl.BlockSpec((1, CHUNK, width), lambda b, i: (b, jj(i), (col * CW) // width))

    def prev(col):
        return pl.BlockSpec((1, HALO, CW), lambda b, i: (b, jnp.maximum(jj(i) * hpc - 1, 0), col))

    def nxt(col):
        return pl.BlockSpec((1, HALO, CW),
                            lambda b, i: (b, jnp.minimum((jj(i) + 1) * hpc, nhb - 1), col))

    def const2(shape):
        return pl.BlockSpec(shape, lambda b, i: (0, 0))

    return nc, main, prev, nxt, const2


def _mlstm_kernel(reverse, finalize, *refs):
    if finalize:
        (qk_ref, v_ref, gs_ref, gb_ref, c0_ref, n0_ref, m0_ref, hb_ref, o_ref, z_ref, mhw_ref,
         out_ref, cf_ref, nf_ref, mf_ref) = refs
    else:
        (q_ref, qp_ref, qn_ref, k_ref, kp_ref, kn_ref, v_ref, gs_ref, cw_ref, cb_ref, gb_ref,
         c0_ref, n0_ref, m0_ref, out_ref, qk_ref, cf_ref, nf_ref, mf_ref) = refs

    i, j, nc = _chunk_pos(reverse)

    @pl.when(i == 0)
    def _():
        cf_ref[...] = c0_ref[...]
        nf_ref[...] = n0_ref[...]
        mf_ref[...] = m0_ref[...]

    mask, tri = _tri_masks(reverse)
    has_prev = j > 0
    has_next = j < nc - 1
    edge = 0 if reverse else CHUNK - 1

    gb = gs_ref[0] + gb_ref[...]
    gbt = gb.T
    lf_all = _log_sigmoid(gb)
    b_all = _cumsum_rows(tri, lf_all)
    b_all_t = b_all.T
    io = 2 * NH_A if reverse else 0
    li_c = gb[:, io:io + NH_A]
    b_c = b_all[:, io + NH_A:io + 2 * NH_A]
    c_r = gbt[io:io + NH_A, :] - b_all_t[io + NH_A:io + 2 * NH_A, :]

    m_prev = mf_ref[0][:, 0:NH_A]
    g_c = b_c[edge:edge + 1, :]
    a_c = g_c - b_c + li_c
    m_new = jnp.maximum(g_c + m_prev, jnp.max(a_c, axis=0, keepdims=True))
    w_c = jnp.exp(a_c - m_new)
    dec = jnp.exp(g_c + m_prev - m_new)

    heads = range(NH_A)
    ksl = [slice(h * DQK_A, (h + 1) * DQK_A) for h in heads]
    vsl = [slice(h * DV_A, (h + 1) * DV_A) for h in heads]
    kksl = [slice(QK_A + h * DQK_A, QK_A + (h + 1) * DQK_A) for h in heads]

    if finalize:
        qb = [qk_ref[0, :, ksl[h]] for h in heads]
        kb = [qk_ref[0, :, kksl[h]] for h in heads]
        qh = [t.astype(F32) for t in qb]
        kh = [t.astype(F32) for t in kb]
    else:
        qh = [_conv_block(q_ref, qp_ref, qn_ref, ksl[h], has_prev, has_next,
                          cw_ref[:, ksl[h]], cb_ref[:, ksl[h]]) * (DQK_A ** -0.5) for h in heads]
        kh = [_conv_block(k_ref, kp_ref, kn_ref, ksl[h], has_prev, has_next,
                          cw_ref[:, kksl[h]], cb_ref[:, kksl[h]]) for h in heads]
        qb = [t.astype(BF16) for t in qh]
        kb = [t.astype(BF16) for t in kh]
        for h in heads:
            qk_ref[0, :, ksl[h]] = qb[h]
            qk_ref[0, :, kksl[h]] = kb[h]
    vb = [v_ref[0, :, vsl[h]].astype(BF16) for h in heads]
    c_prev = [cf_ref[0, h] for h in heads]
    n_prev = [nf_ref[0, h:h + 1, :] for h in heads]
    mp = [m_prev[:, h:h + 1] for h in heads]

    s_qk = [_dot_nt(qb[h], kb[h]) for h in heads]
    q_c = [_dot(qb[h], c_prev[h].astype(BF16)) for h in heads]
    cm = [jnp.where(mask, c_r[h:h + 1, :], -jnp.inf) for h in heads]
    mm = [jnp.maximum(jnp.max(cm[h], axis=1, keepdims=True), mp[h]) for h in heads]
    q_n = [jnp.sum(qh[h] * n_prev[h], axis=1, keepdims=True) for h in heads]

    kw = [kh[h] * w_c[:, h:h + 1] for h in heads]
    for h in heads:
        dh = dec[:, h:h + 1]
        cf_ref[0, h] = dh * c_prev[h] + _dot_tn(kw[h].astype(BF16), vb[h])
        nf_ref[0, h:h + 1, :] = dh * n_prev[h] + jnp.sum(kw[h], axis=0, keepdims=True)
    mf_ref[0, :, 0:NH_A] = m_new

    p = [s_qk[h] * jnp.exp(cm[h] - mm[h]) for h in heads]
    p_v = [_dot(p[h].astype(BF16), vb[h]) for h in heads]
    w_int = [jnp.exp(mp[h] - mm[h]) for h in heads]
    den = [w_int[h] * q_n[h] + jnp.sum(p[h], axis=1, keepdims=True) for h in heads]
    inv = [1.0 / jnp.maximum(jnp.abs(den[h]), jnp.exp(-(b_c[:, h:h + 1] + mm[h]))) for h in heads]
    hval = [(w_int[h] * q_c[h] + p_v[h]) * inv[h] for h in heads]

    if not finalize:
        for h in heads:
            out_ref[0, :, vsl[h]] = hval[h].astype(out_ref.dtype)
        return

    hs = [hval[h] + hb_ref[0, :, vsl[h]].astype(F32) for h in heads]
    mu = [jnp.mean(hs[h], axis=-1, keepdims=True) for h in heads]
    hc = [hs[h] - mu[h] for h in heads]
    var = [jnp.mean(hc[h] * hc[h], axis=-1, keepdims=True) for h in heads]
    for h in heads:
        hn = hc[h] * lax.rsqrt(var[h] + LN_EPS) * mhw_ref[:, vsl[h]]
        o = o_ref[0, :, vsl[h]].astype(F32)
        z = z_ref[0, :, vsl[h]].astype(F32)
        gate = z / ((1.0 + jnp.exp(-o)) * (1.0 + jnp.exp(-z)))
        out_ref[0, :, vsl[h]] = (hn * gate).astype(out_ref.dtype)


def _mlstm_call(reverse, finalize, state, *, p3, small3, gate_b128, conv_w=None, conv_b=None,
                qk=None, hb=None, mh_w=None):
    bsz, t_len, _ = p3.shape
    nc, main, prev, nxt, const2 = _chunk_specs(t_len, reverse)
    c0, n0, m0 = state
    state_specs = [pl.BlockSpec((1, NH_A, DQK_A, DV_A), lambda b, i: (b, 0, 0, 0)),
                   pl.BlockSpec((1, NH_A, DQK_A), lambda b, i: (b, 0, 0)),
                   pl.BlockSpec((1, 1, 128), lambda b, i: (b, 0, 0))]
    state_shapes = [jax.ShapeDtypeStruct(c0.shape, F32), jax.ShapeDtypeStruct(n0.shape, F32),
                    jax.ShapeDtypeStruct(m0.shape, F32)]
    wide = main(0, W_A)
    if finalize:
        in_specs = [wide, main(COL_V, W_A), main(0, N_SMALL), const2((1, N_SMALL))] + state_specs
        in_specs += [wide, main(COL_O, W_A), main(COL_ZA, W_A), const2((1, W_A))]
        args = [qk, p3, small3, gate_b128, c0, n0, m0, hb, p3, p3, mh_w]
        out_specs = [wide] + state_specs
        out_shape = [jax.ShapeDtypeStruct((bsz, t_len, W_A), BF16)] + state_shapes
    else:
        in_specs = [main(COL_Q, CW), prev(COL_Q), nxt(COL_Q),
                    main(COL_K, CW), prev(COL_K), nxt(COL_K),
                    main(COL_V, W_A), main(0, N_SMALL),
                    const2((3, 2 * QK_A)), const2((1, 2 * QK_A)), const2((1, N_SMALL))] + state_specs
        args = [p3, p3, p3, p3, p3, p3, p3, small3, conv_w, conv_b, gate_b128, c0, n0, m0]
        out_specs = [wide, wide] + state_specs
        out_shape = [jax.ShapeDtypeStruct((bsz, t_len, W_A), HB_DTYPE),
                     jax.ShapeDtypeStruct((bsz, t_len, 2 * QK_A), BF16)] + state_shapes
    res = pl.pallas_call(
        functools.partial(_mlstm_kernel, reverse, finalize),
        grid=(bsz, nc), in_specs=in_specs, out_specs=out_specs, out_shape=out_shape,
        compiler_params=_cparams(("arbitrary", "arbitrary")),
        name="mlstm_" + ("bwd" if reverse else "fwd"),
    )(*args)
    if finalize:
        return res[0], tuple(res[1:])
    return res[0], res[1], tuple(res[2:])


def _ssd_kernel(reverse, finalize, *refs):
    if finalize:
        (xc_ref, bcc_ref, gs_ref, dtb_ref, alog_ref, s0_ref, yb_ref, z0_ref, z1_ref, dsk_ref,
         nw_ref, out_ref, sf_ref, y_s) = refs
    else:
        (x0_ref, x0p_ref, x0n_ref, x1_ref, x1p_ref, x1n_ref, bc_ref, bcp_ref, bcn_ref,
         gs_ref, cw_ref, cb_ref, dtb_ref, alog_ref, s0_ref, out_ref, xc_ref, sf_ref) = refs

    i, j, nc = _chunk_pos(reverse)

    @pl.when(i == 0)
    def _():
        sf_ref[...] = s0_ref[...]

    _, tri = _tri_masks(reverse)
    has_prev = j > 0
    has_next = j < nc - 1
    edge = 0 if reverse else CHUNK - 1

    t_i = lax.broadcasted_iota(jnp.int32, (CHUNK, CHUNK), 0)
    l_i = lax.broadcasted_iota(jnp.int32, (CHUNK, CHUNK), 1)
    s_lo = jnp.where(l_i < HALF, l_i, l_i - HALF)
    lo_half = l_i < HALF
    if reverse:
        mask1, mask2 = s_lo >= t_i, s_lo + HALF >= t_i
    else:
        mask1, mask2 = s_lo <= t_i, s_lo + HALF <= t_i
    k_i = lax.broadcasted_iota(jnp.int32, (2 * CHUNK, CHUNK), 0)
    kl_i = lax.broadcasted_iota(jnp.int32, (2 * CHUNK, CHUNK), 1)
    blockdiag = ((k_i // HALF) % 2 == 1) == (kl_i >= HALF)

    dt_all = _softplus(gs_ref[0] + dtb_ref[...])
    da_all = dt_all * (-LOG2E * jnp.exp(alog_ref[...]))
    acum_all = _cumsum_rows(tri, da_all)
    do = H_B if reverse else 0
    acum_t = acum_all.T
    a_even = acum_t[do:do + NPAIR, :]
    a_odd = acum_t[do + NPAIR:do + H_B, :]
    lane16 = lax.broadcasted_iota(jnp.int32, (NPAIR, CHUNK), 1) < HALF
    rv1 = jnp.where(lane16, a_even, pltpu.roll(a_odd, HALF, 1))
    rv2 = jnp.where(lane16, pltpu.roll(a_even, HALF, 1), a_odd)

    if finalize:
        bmat = bcc_ref[0, :, 0:GP_B]
        cmat = bcc_ref[0, :, GP_B:2 * GP_B]
        ssq = jnp.zeros((CHUNK, CHUNK), F32)
    else:
        bmat = _conv_block(bc_ref, bcp_ref, bcn_ref, slice(0, GP_B), has_prev, has_next,
                           cw_ref[:, W_B:W_B + GP_B], cb_ref[:, W_B:W_B + GP_B]).astype(BF16)
        cmat = _conv_block(bc_ref, bcp_ref, bcn_ref, slice(GP_B, 2 * GP_B), has_prev, has_next,
                           cw_ref[:, W_B + GP_B:W_B + 2 * GP_B],
                           cb_ref[:, W_B + GP_B:W_B + 2 * GP_B]).astype(BF16)
        xc_ref[0, :, W_B:W_B + GP_B] = bmat
        xc_ref[0, :, W_B + GP_B:W_B + 2 * GP_B] = cmat

    for g in range(G_B):
        gs = slice(g * GP_B, (g + 1) * GP_B)
        if finalize:
            xg = xc_ref[0, :, gs].astype(F32)
        else:
            xr, xpr, xnr = (x0_ref, x0p_ref, x0n_ref) if g < 2 else (x1_ref, x1p_ref, x1n_ref)
            cs = slice((g % 2) * GP_B, (g % 2 + 1) * GP_B)
            xg = _conv_block(xr, xpr, xnr, cs, has_prev, has_next, cw_ref[:, gs], cb_ref[:, gs])
            xc_ref[0, :, gs] = xg.astype(BF16)
        bg = bmat[:, g * N_B:(g + 1) * N_B]
        cg = cmat[:, g * N_B:(g + 1) * N_B]
        cb1 = _dot_nt(cg, jnp.concatenate([bg[0:HALF], bg[0:HALF]], axis=0))
        cb2 = _dot_nt(cg, jnp.concatenate([bg[HALF:], bg[HALF:]], axis=0))
        s_prev = sf_ref[0, g]
        cs_prev = _dot(cg, s_prev.astype(BF16))
        y_parts, xw_parts, tot_parts = [], [], []
        for pp in range(R_B // 2):
            pr = g * (R_B // 2) + pp
            ps = slice(pp * CHUNK, (pp + 1) * CHUNK)
            idx = jnp.where(lo_half, do + pr, do + NPAIR + pr)
            acum_b = jnp.take_along_axis(acum_all, idx, axis=1)
            dt_b = jnp.take_along_axis(dt_all, idx, axis=1)
            xdt = xg[:, ps] * dt_b
            xdtb = xdt.astype(BF16)
            w1 = cb1 * jnp.exp2(jnp.where(mask1, acum_b - rv1[pr:pr + 1, :], -jnp.inf))
            w2 = cb2 * jnp.exp2(jnp.where(mask2, acum_b - rv2[pr:pr + 1, :], -jnp.inf))
            w12 = jnp.concatenate([w1.astype(BF16), w2.astype(BF16)], axis=1)
            x12 = jnp.concatenate([xdtb[0:HALF], xdtb[0:HALF], xdtb[HALF:], xdtb[HALF:]], axis=0)
            x12 = jnp.where(blockdiag, x12, jnp.zeros_like(x12))
            tot = acum_b[edge:edge + 1, :]
            y_parts.append(_dot(w12, x12) + jnp.exp2(acum_b) * cs_prev[:, ps])
            xw_parts.append((xdt * jnp.exp2(tot - acum_b)).astype(BF16))
            tot_parts.append(tot)
        xw = jnp.concatenate(xw_parts, axis=1)
        etot = jnp.exp2(jnp.concatenate(tot_parts, axis=1))
        sf_ref[0, g] = etot * s_prev + _dot_tn(bg, xw)
        y_grp = jnp.concatenate(y_parts, axis=1)
        if finalize:
            z_ref = z0_ref if g < 2 else z1_ref
            zs = slice((g % 2) * GP_B, (g % 2 + 1) * GP_B)
            y = y_grp + yb_ref[0, :, gs].astype(F32) + dsk_ref[:, gs] * xg
            y = y * _silu(z_ref[0, :, zs].astype(F32))
            y_s[:, gs] = y
            y2 = y * y
            ssq = ssq + (y2[:, 0:CHUNK] + y2[:, CHUNK:2 * CHUNK]
                         + y2[:, 2 * CHUNK:3 * CHUNK] + y2[:, 3 * CHUNK:])
        else:
            out_ref[0, :, gs] = y_grp.astype(out_ref.dtype)

    if finalize:
        inv = lax.rsqrt(jnp.sum(ssq, axis=-1, keepdims=True) * (1.0 / W_B) + LN_EPS)
        for half in range(2):
            cs = slice(half * CW, (half + 1) * CW)
            out_ref[0, :, cs] = (y_s[:, cs] * inv * nw_ref[:, cs]).astype(out_ref.dtype)


def _ssd_call(reverse, finalize, s0, *, small3, dtb128, alog128, p3, conv_w=None, conv_b=None,
              xc=None, yb=None, d_skip=None, norm_w=None):
    bsz, t_len, _ = p3.shape
    nc, main, prev, nxt, const2 = _chunk_specs(t_len, reverse)
    state_spec = pl.BlockSpec((1, G_B, N_B, GP_B), lambda b, i: (b, 0, 0, 0))
    state_shape = jax.ShapeDtypeStruct(s0.shape, F32)
    xc_spec = main(0, W_B + 2 * GP_B)
    if finalize:
        in_specs = [main(0, W_B), main(2, CW), main(0, N_SMALL), const2((1, N_SMALL)),
                    const2((1, N_SMALL)), state_spec,
                    main(0, W_B), main(COL_ZB, CW), main(COL_ZB + 1, CW),
                    const2((1, W_B)), const2((1, W_B))]
        args = [xc, xc, small3, dtb128, alog128, s0, yb, p3, p3, d_skip, norm_w]
        out_specs = [main(0, W_B), state_spec]
        out_shape = [jax.ShapeDtypeStruct((bsz, t_len, W_B), BF16), state_shape]
        scratch = [pltpu.VMEM((CHUNK, W_B), F32)]
    else:
        in_specs = [main(COL_X, CW), prev(COL_X), nxt(COL_X),
                    main(COL_X + 1, CW), prev(COL_X + 1), nxt(COL_X + 1),
                    main(COL_BC, CW), prev(COL_BC), nxt(COL_BC),
                    main(0, N_SMALL),
                    const2((3, 3 * CW)), const2((1, 3 * CW)), const2((1, N_SMALL)),
                    const2((1, N_SMALL)), state_spec]
        args = [p3] * 9 + [small3, conv_w, conv_b, dtb128, alog128, s0]
        out_specs = [main(0, W_B), xc_spec, state_spec]
        out_shape = [jax.ShapeDtypeStruct((bsz, t_len, W_B), HB_DTYPE),
                     jax.ShapeDtypeStruct((bsz, t_len, W_B + 2 * GP_B), BF16), state_shape]
        scratch = []
    return pl.pallas_call(
        functools.partial(_ssd_kernel, reverse, finalize),
        grid=(bsz, nc), in_specs=in_specs, out_specs=out_specs, out_shape=out_shape,
        scratch_shapes=scratch,
        compiler_params=_cparams(("arbitrary", "arbitrary")),
        name="ssd_" + ("bwd" if reverse else "fwd"),
    )(*args)


def _to_colmajor(t, bsz):
    ch = t.shape[-1]
    return t.reshape(bsz, -1, GRID_W, ch).transpose(0, 2, 1, 3).reshape(-1, ch)


def _from_colmajor(t, bsz):
    ch = t.shape[-1]
    return t.reshape(bsz, GRID_W, -1, ch).transpose(0, 2, 1, 3).reshape(-1, ch)


def _mixer_scans(pa, sa, pb, sb, bsz, t_len, lp, states):
    st_mf, st_mb, st_sf, st_sb = states
    pa = pa.reshape(bsz, t_len, N_MAIN_A)
    sa = sa.reshape(bsz, t_len, N_SMALL)
    h_b, qk, st_mb = _mlstm_call(True, False, st_mb, p3=pa, small3=sa, gate_b128=lp["gate_b128"],
                                 conv_w=lp["conv_qk_w"], conv_b=lp["conv_qk_b"])
    y_a, st_mf = _mlstm_call(False, True, st_mf, p3=pa, small3=sa, gate_b128=lp["gate_b128"],
                             qk=qk, hb=h_b, mh_w=lp["mh_w"])

    pb = pb.reshape(bsz, t_len, N_MAIN_B)
    sb = sb.reshape(bsz, t_len, N_SMALL)
    y_bb, xc, st_sb = _ssd_call(True, False, st_sb, small3=sb, dtb128=lp["dtb128"],
                                alog128=lp["alog128"], p3=pb, conv_w=lp["conv_xbc_w"],
                                conv_b=lp["conv_xbc_b"])
    y_b, st_sf = _ssd_call(False, True, st_sf, small3=sb, dtb128=lp["dtb128"],
                           alog128=lp["alog128"], p3=pb, xc=xc, yb=y_bb, d_skip=lp["d_skip"],
                           norm_w=lp["ssm_w"])
    return y_a, y_b, (st_mf, st_mb, st_sf, st_sb)


def _mixer(x2d, bsz, t_len, latent, shift, scale, mod_index, tm, lp, states):
    pa, sa = _in_proj_call(x2d, shift, scale, lp["w_main_a"], lp["w_small_a"], mod_index, tm)
    xs2d = _to_colmajor(x2d, bsz) if latent else x2d
    pb, sb = _in_proj_call(xs2d, shift, scale, lp["w_main_b"], lp["w_small_b"], mod_index, tm)
    y_a, y_b, states = _mixer_scans(pa, sa, pb, sb, bsz, t_len, lp, states)
    y_b = y_b.reshape(bsz * t_len, W_B)
    if latent:
        y_b = _from_colmajor(y_b, bsz)
    return y_a.reshape(bsz * t_len, W_A), y_b, states


def _zero_states(bsz):
    m_state = (jnp.zeros((bsz, NH_A, DQK_A, DV_A), F32), jnp.zeros((bsz, NH_A, DQK_A), F32),
               jnp.zeros((bsz, 1, 128), F32))
    s_state = jnp.zeros((bsz, G_B, N_B, GP_B), F32)
    return (m_state, m_state, s_state, s_state)


def _layer_params(l, w_in, conv_qk_w, conv_qk_b, gate_b, mh_norm_w, conv_xbc_w, conv_xbc_b,
                  dt_bias, a_log, d_skip, ssm_norm_w, w_out):
    w = w_in[l]
    o_q, o_k, o_v, o_o, o_za = 0, QK_A, 2 * QK_A, 2 * QK_A + W_A, 2 * QK_A + 2 * W_A
    o_g = o_za + W_A
    o_x = o_g + 4 * NH_A
    o_b = o_x + W_B
    o_c = o_b + G_B * N_B
    o_dt = o_c + G_B * N_B
    o_zb = o_dt + 2 * H_B
    cols = lambda a, n: w[:, a:a + n]
    w_main_a = jnp.concatenate(
        [cols(o_v, W_A), cols(o_o, W_A), cols(o_za, W_A), cols(o_q, QK_A), cols(o_k, QK_A)],
        axis=1).astype(BF16)
    w_main_b = jnp.concatenate(
        [cols(o_zb, W_B), cols(o_x, W_B), cols(o_b, G_B * N_B), cols(o_c, G_B * N_B)],
        axis=1).astype(BF16)

    def pad_cols(a):
        return jnp.concatenate([a, jnp.zeros((a.shape[0], N_SMALL - a.shape[1]), F32)], axis=1)

    def even_odd(a):
        a4 = a.reshape(a.shape[0], 2, NPAIR, 2)
        return a4.transpose(0, 1, 3, 2).reshape(a.shape[0], 2 * H_B)

    return {
        "w_main_a": w_main_a, "w_main_b": w_main_b,
        "w_small_a": pad_cols(cols(o_g, 4 * NH_A)).astype(BF16),
        "w_small_b": pad_cols(even_odd(cols(o_dt, 2 * H_B))).astype(BF16),
        "conv_qk_w": conv_qk_w[l], "conv_qk_b": conv_qk_b[l].reshape(1, -1),
        "gate_b128": pad_cols(gate_b[l].reshape(1, -1)),
        "mh_w": mh_norm_w[l].reshape(1, W_A),
        "conv_xbc_w": conv_xbc_w[l], "conv_xbc_b": conv_xbc_b[l].reshape(1, -1),
        "dtb128": pad_cols(even_odd(dt_bias[l].reshape(1, -1))),
        "alog128": pad_cols(even_odd(a_log[l].reshape(1, -1))),
        "d_skip": jnp.repeat(d_skip[l], P_B).reshape(1, W_B),
        "ssm_w": ssm_norm_w[l].reshape(1, W_B),
        "w_out_a": w_out[l, :W_A].astype(BF16), "w_out_b": w_out[l, W_A:].astype(BF16),
    }


def kernel(x, c, ctx, c_ctx, w_ada, b_ada, w_in, conv_qk_w, conv_qk_b, gate_b, mh_norm_w,
           conv_xbc_w, conv_xbc_b, dt_bias, a_log, d_skip, ssm_norm_w, w_out, ln_g, ln_b):
    bsz, t_len, d = x.shape
    ctx_len = ctx.shape[1]
    depth = w_in.shape[0]
    assert d == D_MODEL and depth == DEPTH and bsz + 1 <= 8
    assert t_len % (CHUNK * GRID_W) == 0 and ctx_len % CHUNK == 0

    cvec = jnp.concatenate([c, c_ctx[None, :], jnp.zeros((8 - bsz - 1, d), F32)], axis=0)
    mod = _ada_call(cvec, w_ada, b_ada)

    x2d = x.reshape(bsz * t_len, d)
    xc2d = ctx.reshape(bsz * ctx_len, d)
    tm_in, tm_out, tm_ctx = 1024, 256, 256
    lat_in = lambda i: i // (t_len // tm_in)
    lat_out = lambda i: i // (t_len // tm_out)
    ctx_index = lambda i: bsz

    for l in range(depth):
        last = l == depth - 1
        lp = _layer_params(l, w_in, conv_qk_w, conv_qk_b, gate_b, mh_norm_w, conv_xbc_w,
                           conv_xbc_b, dt_bias, a_log, d_skip, ssm_norm_w, w_out)
        shift = mod[l, :, 0:d].reshape(8, 1, d)
        scale = mod[l, :, d:2 * d].reshape(8, 1, d)
        gate = mod[l, :, 2 * d:3 * d].reshape(8, 1, d)

        yc_a, yc_b, ctx_states = _mixer(xc2d, bsz, ctx_len, False, shift, scale, ctx_index,
                                        tm_ctx, lp, _zero_states(bsz))
        y_a, y_b, _ = _mixer(x2d, bsz, t_len, True, shift, scale, lat_in, tm_in, lp, ctx_states)
        x2d = _out_proj_call(y_a, y_b, x2d, gate, lp["w_out_a"], lp["w_out_b"], ln_g[l], ln_b[l],
                             lat_out, tm_out)
        if not last:
            xc2d = _out_proj_call(yc_a, yc_b, xc2d, gate, lp["w_out_a"], lp["w_out_b"], ln_g[l],
                                  ln_b[l], ctx_index, tm_ctx)
    return x2d.reshape(bsz, t_len, d)
```

```python
import functools

import jax
import jax.numpy as jnp
from jax import lax
from jax.experimental import pallas as pl
from jax.experimental.pallas import tpu as pltpu

F32 = jnp.float32
BF16 = jnp.bfloat16

D_MODEL = 2048
DEPTH = 2
GRID_W = 64
NH_A = 8
DV_A = 256
DQK_A = 128
W_A = NH_A * DV_A
QK_A = NH_A * DQK_A
P_B = 64
H_B = 32
G_B = 4
R_B = H_B // G_B
N_B = 128
W_B = H_B * P_B
GP_B = R_B * P_B
NPAIR = H_B // 2
CHUNK = 128
HALF = CHUNK // 2
LN_EPS = 1e-5
DEEPNORM_ALPHA = (2 * DEPTH) ** 0.25
LOG2E = 1.4426950408889634

CW = 1024
COL_Q, COL_K, COL_V, COL_O, COL_ZA = 0, 1, 2, 4, 6
N_MAIN_A = 8 * CW
COL_X, COL_BC, COL_ZB = 0, 2, 3
N_MAIN_B = 5 * CW
N_SMALL = 128

HALO = 16
VMEM_LIMIT = 56 * 1024 * 1024

PROJ_DTYPE = BF16
HB_DTYPE = BF16


def _cparams(sem):
    return pltpu.CompilerParams(dimension_semantics=sem, vmem_limit_bytes=VMEM_LIMIT)


def _softplus(x):
    return jnp.maximum(x, 0.0) + jnp.log1p(jnp.exp(-jnp.abs(x)))


def _log_sigmoid(x):
    return jnp.minimum(x, 0.0) - jnp.log1p(jnp.exp(-jnp.abs(x)))


def _silu(x):
    return x * jax.nn.sigmoid(x)


def _dot(a, b):
    return jnp.dot(a, b, preferred_element_type=F32)


def _dot_nt(a, b):
    return lax.dot_general(a, b, (((1,), (1,)), ((), ())), preferred_element_type=F32)


def _dot_tn(a, b):
    return lax.dot_general(a, b, (((0,), (0,)), ((), ())), preferred_element_type=F32)


def _cumsum_rows(tri, x):
    return jnp.dot(tri, x, preferred_element_type=F32, precision=lax.Precision.HIGHEST)


def _ada_kernel(c_ref, w_ref, b_ref, o_ref):
    n_rows, d, _ = c_ref.shape
    tn = w_ref.shape[2]
    o_ref[...] = jnp.zeros(o_ref.shape, F32)
    for r in range(n_rows):
        s = _silu(c_ref[r]).reshape(d // 8, 8, 128)
        for cb in range(tn // 128):
            cs = slice(cb * 128, (cb + 1) * 128)
            acc = jnp.sum(w_ref[0, :, cs].reshape(d // 8, 8, 128) * s, axis=0)
            o_ref[0, r:r + 1, cs] = jnp.sum(acc, axis=0, keepdims=True) + b_ref[0, :, cs]


def _ada_call(crep, w_ada, b_ada):
    depth, d, n3 = w_ada.shape
    n_rows = crep.shape[0]
    tn = 1024
    return pl.pallas_call(
        _ada_kernel,
        grid=(depth, n3 // tn),
        in_specs=[pl.BlockSpec((n_rows, d, 128), lambda l, j: (0, 0, 0)),
                  pl.BlockSpec((1, d, tn), lambda l, j: (l, 0, j)),
                  pl.BlockSpec((1, 1, tn), lambda l, j: (l, 0, j))],
        out_specs=pl.BlockSpec((1, 8, tn), lambda l, j: (l, 0, j)),
        out_shape=jax.ShapeDtypeStruct((depth, 8, n3), F32),
        compiler_params=_cparams(("arbitrary", "arbitrary")),
        name="adaln_mod",
    )(crep, w_ada, b_ada.reshape(depth, 1, n3))


def _cast_kernel(x_ref, o_ref):
    o_ref[...] = x_ref[...].astype(o_ref.dtype)


def _cast_bf16(w3, tr):
    depth, rows, cols = w3.shape
    spec = pl.BlockSpec((1, tr, cols), lambda l, i: (l, i, 0))
    return pl.pallas_call(
        _cast_kernel, grid=(depth, rows // tr), in_specs=[spec], out_specs=spec,
        out_shape=jax.ShapeDtypeStruct(w3.shape, BF16),
        compiler_params=_cparams(("arbitrary", "arbitrary")),
        name="cast_bf16",
    )(w3)


def _in_proj_kernel(x_ref, shift_ref, scale_ref, w_ref, ws_ref, p_ref, small_ref, u_ref):
    @pl.when(pl.program_id(1) == 0)
    def _():
        x = x_ref[...]
        mu = jnp.mean(x, axis=-1, keepdims=True)
        xc = x - mu
        var = jnp.mean(xc * xc, axis=-1, keepdims=True)
        u = xc * lax.rsqrt(var + LN_EPS) * (1.0 + scale_ref[0]) + shift_ref[0]
        ub = u.astype(BF16)
        u_ref[...] = ub
        small_ref[...] = _dot(ub, ws_ref[...])

    p_ref[...] = _dot(u_ref[...], w_ref[0]).astype(p_ref.dtype)


def _in_proj_call(x2d, shift, scale, w3, layer, n_main, w_small, mod_index, tm):
    rows, d = x2d.shape
    tn = CW
    return pl.pallas_call(
        _in_proj_kernel,
        grid=(rows // tm, n_main // tn),
        in_specs=[pl.BlockSpec((tm, d), lambda i, j: (i, 0)),
                  pl.BlockSpec((1, 1, d), lambda i, j: (mod_index(i), 0, 0)),
                  pl.BlockSpec((1, 1, d), lambda i, j: (mod_index(i), 0, 0)),
                  pl.BlockSpec((1, d, tn), lambda i, j: (layer, 0, j)),
                  pl.BlockSpec((d, N_SMALL), lambda i, j: (0, 0))],
        out_specs=[pl.BlockSpec((tm, tn), lambda i, j: (i, j)),
                   pl.BlockSpec((tm, N_SMALL), lambda i, j: (i, 0))],
        out_shape=[jax.ShapeDtypeStruct((rows, n_main), PROJ_DTYPE),
                   jax.ShapeDtypeStruct((rows, N_SMALL), F32)],
        scratch_shapes=[pltpu.VMEM((tm, d), BF16)],
        compiler_params=_cparams(("arbitrary", "arbitrary")),
        name="in_proj",
    )(x2d, shift, scale, w3, w_small)


def _out_proj_kernel(ya_ref, yb_ref, x_ref, gate_ref, wa_ref, wb_ref, g_ref, b_ref, o_ref):
    acc = _dot(ya_ref[...], wa_ref[0]) + _dot(yb_ref[...], wb_ref[0])
    r = DEEPNORM_ALPHA * x_ref[...] + gate_ref[0] * acc
    mu = jnp.mean(r, axis=-1, keepdims=True)
    rc = r - mu
    var = jnp.mean(rc * rc, axis=-1, keepdims=True)
    o_ref[...] = rc * lax.rsqrt(var + LN_EPS) * g_ref[...] + b_ref[...]


def _out_proj_call(ya, yb, x2d, gate, w_out3, layer, ln_g, ln_b, mod_index, tm):
    rows, d = x2d.shape
    const = lambda i: (0, 0)
    return pl.pallas_call(
        _out_proj_kernel,
        grid=(rows // tm,),
        in_specs=[pl.BlockSpec((tm, W_A), lambda i: (i, 0)),
                  pl.BlockSpec((tm, W_B), lambda i: (i, 0)),
                  pl.BlockSpec((tm, d), lambda i: (i, 0)),
                  pl.BlockSpec((1, 1, d), lambda i: (mod_index(i), 0, 0)),
                  pl.BlockSpec((1, W_A, d), lambda i: (layer, 0, 0)),
                  pl.BlockSpec((1, W_B, d), lambda i: (layer, 1, 0)),
                  pl.BlockSpec((1, d), const),
                  pl.BlockSpec((1, d), const)],
        out_specs=pl.BlockSpec((tm, d), lambda i: (i, 0)),
        out_shape=jax.ShapeDtypeStruct((rows, d), F32),
        compiler_params=_cparams(("arbitrary",)),
        name="out_proj",
    )(ya, yb, x2d, gate, w_out3, w_out3, ln_g.reshape(1, d), ln_b.reshape(1, d))


def _conv_silu(x, prev_row, next_row, w, bias):
    rows = x.shape[0]
    row = lax.broadcasted_iota(jnp.int32, x.shape, 0)
    xm = jnp.where(row == 0, prev_row, pltpu.roll(x, 1, 0))
    xp = jnp.where(row == rows - 1, next_row, pltpu.roll(x, rows - 1, 0))
    y = w[0:1] * xm + w[1:2] * x + w[2:3] * xp + bias
    return _silu(y)


def _conv_block(x_ref, p_ref, n_ref, cs, has_prev, has_next, w, bias):
    pr = jnp.where(has_prev, p_ref[0, HALO - 1:HALO, cs].astype(F32), 0.0)
    nx = jnp.where(has_next, n_ref[0, 0:1, cs].astype(F32), 0.0)
    return _conv_silu(x_ref[0, :, cs].astype(F32), pr, nx, w, bias)


def _tri_masks(reverse):
    t = lax.broadcasted_iota(jnp.int32, (CHUNK, CHUNK), 0)
    s = lax.broadcasted_iota(jnp.int32, (CHUNK, CHUNK), 1)
    mask = (s >= t) if reverse else (s <= t)
    mask_t = (t >= s) if reverse else (t <= s)
    return mask, mask.astype(F32), mask_t.astype(F32)


def _chunk_pos(reverse):
    i = pl.program_id(1)
    nc = pl.num_programs(1)
    j = (nc - 1 - i) if reverse else i
    return i, j, nc


def _chunk_specs(t_len, reverse):
    nc = t_len // CHUNK
    nhb = t_len // HALO
    hpc = CHUNK // HALO

    def jj(i):
        return (nc - 1 - i) if reverse else i

    def main(col, width):
        return pl.BlockSpec((1, CHUNK, width), lambda b, i: (b, jj(i), (col * CW) // width))

    def prev(col):
        return pl.BlockSpec((1, HALO, CW), lambda b, i: (b, jnp.maximum(jj(i) * hpc - 1, 0), col))

    def nxt(col):
        return pl.BlockSpec((1, HALO, CW),
                            lambda b, i: (b, jnp.minimum((jj(i) + 1) * hpc, nhb - 1), col))

    def const2(shape):
        return pl.BlockSpec(shape, lambda b, i: (0, 0))

    return nc, main, prev, nxt, const2


def _mlstm_kernel(reverse, finalize, *refs):
    if finalize:
        (qk_ref, v_ref, gs_ref, gb_ref, c0_ref, m0_ref, hb_ref, o_ref, z_ref, mhw_ref,
         out_ref, cf_ref, mf_ref) = refs
    else:
        (q_ref, qp_ref, qn_ref, k_ref, kp_ref, kn_ref, v_ref, gs_ref, cw_ref, cb_ref, gb_ref,
         c0_ref, m0_ref, out_ref, qk_ref, cf_ref, mf_ref) = refs

    i, j, nc = _chunk_pos(reverse)

    @pl.when(i == 0)
    def _():
        cf_ref[...] = c0_ref[...]
        mf_ref[...] = m0_ref[...]

    mask, tri, tri_t = _tri_masks(reverse)
    has_prev = j > 0
    has_next = j < nc - 1
    edge = 0 if reverse else CHUNK - 1

    gb = gs_ref[0] + gb_ref[...]
    gbt = gb.T
    io = 2 * NH_A if reverse else 0
    b_all = _cumsum_rows(tri, _log_sigmoid(gb))
    li_c = gb[:, io:io + NH_A]
    b_c = b_all[:, io + NH_A:io + 2 * NH_A]
    b_r = jnp.dot(_log_sigmoid(gbt[io + NH_A:io + 2 * NH_A, :]), tri_t,
                  preferred_element_type=F32, precision=lax.Precision.HIGHEST)
    c_r = gbt[io:io + NH_A, :] - b_r

    m_prev = mf_ref[0][:, 0:NH_A]
    g_c = b_c[edge:edge + 1, :]
    a_c = g_c - b_c + li_c
    m_new = jnp.maximum(g_c + m_prev, jnp.max(a_c, axis=0, keepdims=True))
    w_c = jnp.exp(a_c - m_new)
    dec = jnp.exp(g_c + m_prev - m_new)

    heads = range(NH_A)
    ksl = [slice(h * DQK_A, (h + 1) * DQK_A) for h in heads]
    vsl = [slice(h * DV_A, (h + 1) * DV_A) for h in heads]
    kksl = [slice(QK_A + h * DQK_A, QK_A + (h + 1) * DQK_A) for h in heads]

    if finalize:
        qb = [qk_ref[0, :, ksl[h]] for h in heads]
        kb = [qk_ref[0, :, kksl[h]] for h in heads]
        qh = [t.astype(F32) for t in qb]
        kh = [t.astype(F32) for t in kb]
        gate = []
        for h in heads:
            o = o_ref[0, :, vsl[h]].astype(F32)
            z = z_ref[0, :, vsl[h]].astype(F32)
            gate.append(z / ((1.0 + jnp.exp(-o)) * (1.0 + jnp.exp(-z))))
    else:
        qh = [_conv_block(q_ref, qp_ref, qn_ref, ksl[h], has_prev, has_next,
                          cw_ref[:, ksl[h]], cb_ref[:, ksl[h]]) * (DQK_A ** -0.5) for h in heads]
        kh = [_conv_block(k_ref, kp_ref, kn_ref, ksl[h], has_prev, has_next,
                          cw_ref[:, kksl[h]], cb_ref[:, kksl[h]]) for h in heads]
        qb = [t.astype(BF16) for t in qh]
        kb = [t.astype(BF16) for t in kh]
        for h in heads:
            qk_ref[0, :, ksl[h]] = qb[h]
            qk_ref[0, :, kksl[h]] = kb[h]
    ones_l = jnp.ones((CHUNK, CHUNK), BF16)
    vb = [jnp.concatenate([v_ref[0, :, vsl[h]].astype(BF16), ones_l], axis=1) for h in heads]
    c_prev = [cf_ref[0, h] for h in heads]
    mp = [m_prev[:, h:h + 1] for h in heads]

    s_qk = [_dot_nt(qb[h], kb[h]) for h in heads]
    q_c = [_dot(qb[h], c_prev[h].astype(BF16)) for h in heads]
    cm = [jnp.where(mask, c_r[h:h + 1, :], -jnp.inf) for h in heads]
    mm = [jnp.broadcast_to(jnp.maximum(jnp.max(cm[h], axis=1, keepdims=True), mp[h]),
                           (CHUNK, CHUNK)) for h in heads]

    kw = [kh[h] * w_c[:, h:h + 1] for h in heads]
    for h in heads:
        cf_ref[0, h] = dec[:, h:h + 1] * c_prev[h] + _dot_tn(kw[h].astype(BF16), vb[h])
    mf_ref[0, :, 0:NH_A] = m_new

    p = [s_qk[h] * jnp.exp(cm[h] - mm[h]) for h in heads]
    p_v = [_dot(p[h].astype(BF16), vb[h]) for h in heads]
    w_int = [jnp.exp(mp[h] - mm[h]) for h in heads]
    den = [w_int[h] * q_c[h][:, DV_A:] + p_v[h][:, DV_A:] for h in heads]
    b_t = [jnp.broadcast_to(b_c[:, h:h + 1], (CHUNK, CHUNK)) for h in heads]
    inv = [1.0 / jnp.maximum(jnp.abs(den[h]), jnp.exp(-(b_t[h] + mm[h]))) for h in heads]
    hval = [jnp.concatenate(
        [(w_int[h] * q_c[h][:, c * CHUNK:(c + 1) * CHUNK] + p_v[h][:, c * CHUNK:(c + 1) * CHUNK])
         * inv[h] for c in range(DV_A // CHUNK)], axis=1) for h in heads]

    if not finalize:
        for h in heads:
            out_ref[0, :, vsl[h]] = hval[h].astype(out_ref.dtype)
        return

    hs = [hval[h] + hb_ref[0, :, vsl[h]].astype(F32) for h in heads]
    mu = [jnp.mean(hs[h], axis=-1, keepdims=True) for h in heads]
    hc = [hs[h] - mu[h] for h in heads]
    var = [jnp.mean(hc[h] * hc[h], axis=-1, keepdims=True) for h in heads]
    for h in heads:
        hn = hc[h] * lax.rsqrt(var[h] + LN_EPS) * mhw_ref[:, vsl[h]]
        out_ref[0, :, vsl[h]] = (hn * gate[h]).astype(out_ref.dtype)


def _mlstm_call(reverse, finalize, state, *, p3, small3, gate_b128, conv_w=None, conv_b=None,
                qk=None, hb=None, mh_w=None):
    bsz, t_len, _ = p3.shape
    nc, main, prev, nxt, const2 = _chunk_specs(t_len, reverse)
    c0, m0 = state
    state_specs = [pl.BlockSpec((1, NH_A, DQK_A, DV_A + CHUNK), lambda b, i: (b, 0, 0, 0)),
                   pl.BlockSpec((1, 1, 128), lambda b, i: (b, 0, 0))]
    state_shapes = [jax.ShapeDtypeStruct(c0.shape, F32), jax.ShapeDtypeStruct(m0.shape, F32)]
    wide = main(0, W_A)
    if finalize:
        in_specs = [wide, main(COL_V, W_A), main(0, N_SMALL), const2((1, N_SMALL))] + state_specs
        in_specs += [wide, main(COL_O, W_A), main(COL_ZA, W_A), const2((1, W_A))]
        args = [qk, p3, small3, gate_b128, c0, m0, hb, p3, p3, mh_w]
        out_specs = [wide] + state_specs
        out_shape = [jax.ShapeDtypeStruct((bsz, t_len, W_A), BF16)] + state_shapes
    else:
        in_specs = [main(COL_Q, CW), prev(COL_Q), nxt(COL_Q),
                    main(COL_K, CW), prev(COL_K), nxt(COL_K),
                    main(COL_V, W_A), main(0, N_SMALL),
                    const2((3, 2 * QK_A)), const2((1, 2 * QK_A)), const2((1, N_SMALL))] + state_specs
        args = [p3, p3, p3, p3, p3, p3, p3, small3, conv_w, conv_b, gate_b128, c0, m0]
        out_specs = [wide, wide] + state_specs
        out_shape = [jax.ShapeDtypeStruct((bsz, t_len, W_A), HB_DTYPE),
                     jax.ShapeDtypeStruct((bsz, t_len, 2 * QK_A), BF16)] + state_shapes
    res = pl.pallas_call(
        functools.partial(_mlstm_kernel, reverse, finalize),
        grid=(bsz, nc), in_specs=in_specs, out_specs=out_specs, out_shape=out_shape,
        compiler_params=_cparams(("arbitrary", "arbitrary")),
        name="mlstm_" + ("bwd" if reverse else "fwd"),
    )(*args)
    if finalize:
        return res[0], tuple(res[1:])
    return res[0], res[1], tuple(res[2:])


def _ssd_kernel(reverse, finalize, *refs):
    if finalize:
        (xc_ref, bcc_ref, gs_ref, dtb_ref, alog_ref, s0_ref, yb_ref, z0_ref, z1_ref, dsk_ref,
         nw_ref, out_ref, sf_ref, y_s) = refs
    else:
        (x0_ref, x0p_ref, x0n_ref, x1_ref, x1p_ref, x1n_ref, bc_ref, bcp_ref, bcn_ref,
         gs_ref, cw_ref, cb_ref, dtb_ref, alog_ref, s0_ref, out_ref, xc_ref, sf_ref) = refs

    i, j, nc = _chunk_pos(reverse)

    @pl.when(i == 0)
    def _():
        sf_ref[...] = s0_ref[...]

    _, tri, _ = _tri_masks(reverse)
    has_prev = j > 0
    has_next = j < nc - 1
    edge = 0 if reverse else CHUNK - 1

    t_i = lax.broadcasted_iota(jnp.int32, (CHUNK, CHUNK), 0)
    l_i = lax.broadcasted_iota(jnp.int32, (CHUNK, CHUNK), 1)
    s_lo = jnp.where(l_i < HALF, l_i, l_i - HALF)
    lo_half = l_i < HALF
    if reverse:
        mask1, mask2 = s_lo >= t_i, s_lo + HALF >= t_i
    else:
        mask1, mask2 = s_lo <= t_i, s_lo + HALF <= t_i
    k_i = lax.broadcasted_iota(jnp.int32, (2 * CHUNK, CHUNK), 0)
    kl_i = lax.broadcasted_iota(jnp.int32, (2 * CHUNK, CHUNK), 1)
    blockdiag = ((k_i // HALF) % 2 == 1) == (kl_i >= HALF)

    dt_all = _softplus(gs_ref[0] + dtb_ref[...])
    da_all = dt_all * (-LOG2E * jnp.exp(alog_ref[...]))
    acum_all = _cumsum_rows(tri, da_all)
    do = H_B if reverse else 0
    acum_t = acum_all.T
    a_even = acum_t[do:do + NPAIR, :]
    a_odd = acum_t[do + NPAIR:do + H_B, :]
    lane16 = lax.broadcasted_iota(jnp.int32, (NPAIR, CHUNK), 1) < HALF
    rv1 = jnp.where(lane16, a_even, pltpu.roll(a_odd, HALF, 1))
    rv2 = jnp.where(lane16, pltpu.roll(a_even, HALF, 1), a_odd)

    if finalize:
        bmat = bcc_ref[0, :, 0:GP_B]
        cmat = bcc_ref[0, :, GP_B:2 * GP_B]
        ssq = jnp.zeros((CHUNK, CHUNK), F32)
    else:
        bmat = _conv_block(bc_ref, bcp_ref, bcn_ref, slice(0, GP_B), has_prev, has_next,
                           cw_ref[:, W_B:W_B + GP_B], cb_ref[:, W_B:W_B + GP_B]).astype(BF16)
        cmat = _conv_block(bc_ref, bcp_ref, bcn_ref, slice(GP_B, 2 * GP_B), has_prev, has_next,
                           cw_ref[:, W_B + GP_B:W_B + 2 * GP_B],
                           cb_ref[:, W_B + GP_B:W_B + 2 * GP_B]).astype(BF16)
        xc_ref[0, :, W_B:W_B + GP_B] = bmat
        xc_ref[0, :, W_B + GP_B:W_B + 2 * GP_B] = cmat

    for g in range(G_B):
        gs = slice(g * GP_B, (g + 1) * GP_B)
        if finalize:
            xg = xc_ref[0, :, gs].astype(F32)
        else:
            xr, xpr, xnr = (x0_ref, x0p_ref, x0n_ref) if g < 2 else (x1_ref, x1p_ref, x1n_ref)
            cs = slice((g % 2) * GP_B, (g % 2 + 1) * GP_B)
            xg = _conv_block(xr, xpr, xnr, cs, has_prev, has_next, cw_ref[:, gs], cb_ref[:, gs])
            xc_ref[0, :, gs] = xg.astype(BF16)
        bg = bmat[:, g * N_B:(g + 1) * N_B]
        cg = cmat[:, g * N_B:(g + 1) * N_B]
        cb1 = _dot_nt(cg, jnp.concatenate([bg[0:HALF], bg[0:HALF]], axis=0))
        cb2 = _dot_nt(cg, jnp.concatenate([bg[HALF:], bg[HALF:]], axis=0))
        s_prev = sf_ref[0, g]
        cs_prev = _dot(cg, s_prev.astype(BF16))
        y_parts, xw_parts, tot_parts = [], [], []
        for pp in range(R_B // 2):
            pr = g * (R_B // 2) + pp
            ps = slice(pp * CHUNK, (pp + 1) * CHUNK)
            idx = jnp.where(lo_half, do + pr, do + NPAIR + pr)
            acum_b = jnp.take_along_axis(acum_all, idx, axis=1)
            dt_b = jnp.take_along_axis(dt_all, idx, axis=1)
            xdt = xg[:, ps] * dt_b
            xdtb = xdt.astype(BF16)
            w1 = cb1 * jnp.exp2(jnp.where(mask1, acum_b - rv1[pr:pr + 1, :], -jnp.inf))
            w2 = cb2 * jnp.exp2(jnp.where(mask2, acum_b - rv2[pr:pr + 1, :], -jnp.inf))
            w12 = jnp.concatenate([w1.astype(BF16), w2.astype(BF16)], axis=1)
            x12 = jnp.concatenate([xdtb[0:HALF], xdtb[0:HALF], xdtb[HALF:], xdtb[HALF:]], axis=0)
            x12 = jnp.where(blockdiag, x12, jnp.zeros_like(x12))
            tot = acum_b[edge:edge + 1, :]
            y_parts.append(_dot(w12, x12) + jnp.exp2(acum_b) * cs_prev[:, ps])
            xw_parts.append((xdt * jnp.exp2(tot - acum_b)).astype(BF16))
            tot_parts.append(tot)
        xw = jnp.concatenate(xw_parts, axis=1)
        etot = jnp.exp2(jnp.concatenate(tot_parts, axis=1))
        sf_ref[0, g] = etot * s_prev + _dot_tn(bg, xw)
        y_grp = jnp.concatenate(y_parts, axis=1)
        if finalize:
            z_ref = z0_ref if g < 2 else z1_ref
            zs = slice((g % 2) * GP_B, (g % 2 + 1) * GP_B)
            y = y_grp + yb_ref[0, :, gs].astype(F32) + dsk_ref[:, gs] * xg
            y = y * _silu(z_ref[0, :, zs].astype(F32))
            y_s[:, gs] = y
            y2 = y * y
            ssq = ssq + (y2[:, 0:CHUNK] + y2[:, CHUNK:2 * CHUNK]
                         + y2[:, 2 * CHUNK:3 * CHUNK] + y2[:, 3 * CHUNK:])
        else:
            out_ref[0, :, gs] = y_grp.astype(out_ref.dtype)

    if finalize:
        inv = lax.rsqrt(jnp.sum(ssq, axis=-1, keepdims=True) * (1.0 / W_B) + LN_EPS)
        for half in range(2):
            cs = slice(half * CW, (half + 1) * CW)
            out_ref[0, :, cs] = (y_s[:, cs] * inv * nw_ref[:, cs]).astype(out_ref.dtype)


def _ssd_call(reverse, finalize, s0, *, small3, dtb128, alog128, p3, conv_w=None, conv_b=None,
              xc=None, yb=None, d_skip=None, norm_w=None):
    bsz, t_len, _ = p3.shape
    nc, main, prev, nxt, const2 = _chunk_specs(t_len, reverse)
    state_spec = pl.BlockSpec((1, G_B, N_B, GP_B), lambda b, i: (b, 0, 0, 0))
    state_shape = jax.ShapeDtypeStruct(s0.shape, F32)
    xc_spec = main(0, W_B + 2 * GP_B)
    if finalize:
        in_specs = [main(0, W_B), main(2, CW), main(0, N_SMALL), const2((1, N_SMALL)),
                    const2((1, N_SMALL)), state_spec,
                    main(0, W_B), main(COL_ZB, CW), main(COL_ZB + 1, CW),
                    const2((1, W_B)), const2((1, W_B))]
        args = [xc, xc, small3, dtb128, alog128, s0, yb, p3, p3, d_skip, norm_w]
        out_specs = [main(0, W_B), state_spec]
        out_shape = [jax.ShapeDtypeStruct((bsz, t_len, W_B), BF16), state_shape]
        scratch = [pltpu.VMEM((CHUNK, W_B), F32)]
    else:
        in_specs = [main(COL_X, CW), prev(COL_X), nxt(COL_X),
                    main(COL_X + 1, CW), prev(COL_X + 1), nxt(COL_X + 1),
                    main(COL_BC, CW), prev(COL_BC), nxt(COL_BC),
                    main(0, N_SMALL),
                    const2((3, 3 * CW)), const2((1, 3 * CW)), const2((1, N_SMALL)),
                    const2((1, N_SMALL)), state_spec]
        args = [p3] * 9 + [small3, conv_w, conv_b, dtb128, alog128, s0]
        out_specs = [main(0, W_B), xc_spec, state_spec]
        out_shape = [jax.ShapeDtypeStruct((bsz, t_len, W_B), HB_DTYPE),
                     jax.ShapeDtypeStruct((bsz, t_len, W_B + 2 * GP_B), BF16), state_shape]
        scratch = []
    return pl.pallas_call(
        functools.partial(_ssd_kernel, reverse, finalize),
        grid=(bsz, nc), in_specs=in_specs, out_specs=out_specs, out_shape=out_shape,
        scratch_shapes=scratch,
        compiler_params=_cparams(("arbitrary", "arbitrary")),
        name="ssd_" + ("bwd" if reverse else "fwd"),
    )(*args)


def _to_colmajor(t, bsz):
    ch = t.shape[-1]
    return t.reshape(bsz, -1, GRID_W, ch).transpose(0, 2, 1, 3).reshape(-1, ch)


def _from_colmajor(t, bsz):
    ch = t.shape[-1]
    return t.reshape(bsz, GRID_W, -1, ch).transpose(0, 2, 1, 3).reshape(-1, ch)


def _mixer_scans(pa, sa, pb, sb, bsz, t_len, lp, states):
    st_mf, st_mb, st_sf, st_sb = states
    pa = pa.reshape(bsz, t_len, N_MAIN_A)
    sa = sa.reshape(bsz, t_len, N_SMALL)
    h_b, qk, st_mb = _mlstm_call(True, False, st_mb, p3=pa, small3=sa, gate_b128=lp["gate_b128"],
                                 conv_w=lp["conv_qk_w"], conv_b=lp["conv_qk_b"])
    y_a, st_mf = _mlstm_call(False, True, st_mf, p3=pa, small3=sa, gate_b128=lp["gate_b128"],
                             qk=qk, hb=h_b, mh_w=lp["mh_w"])

    pb = pb.reshape(bsz, t_len, N_MAIN_B)
    sb = sb.reshape(bsz, t_len, N_SMALL)
    y_bb, xc, st_sb = _ssd_call(True, False, st_sb, small3=sb, dtb128=lp["dtb128"],
                                alog128=lp["alog128"], p3=pb, conv_w=lp["conv_xbc_w"],
                                conv_b=lp["conv_xbc_b"])
    y_b, st_sf = _ssd_call(False, True, st_sf, small3=sb, dtb128=lp["dtb128"],
                           alog128=lp["alog128"], p3=pb, xc=xc, yb=y_bb, d_skip=lp["d_skip"],
                           norm_w=lp["ssm_w"])
    return y_a, y_b, (st_mf, st_mb, st_sf, st_sb)


def _mixer(x2d, bsz, t_len, latent, shift, scale, mod_index, tm, w_bf, l, lp, states):
    pa, sa = _in_proj_call(x2d, shift, scale, w_bf, l, N_MAIN_A, lp["w_small_a"], mod_index, tm)
    xs2d = _to_colmajor(x2d, bsz) if latent else x2d
    pb, sb = _in_proj_call(xs2d, shift, scale, lp["w_main_b"], 0, N_MAIN_B, lp["w_small_b"],
                           mod_index, tm)
    y_a, y_b, states = _mixer_scans(pa, sa, pb, sb, bsz, t_len, lp, states)
    y_b = y_b.reshape(bsz * t_len, W_B)
    if latent:
        y_b = _from_colmajor(y_b, bsz)
    return y_a.reshape(bsz * t_len, W_A), y_b, states


def _zero_states(bsz):
    m_state = (jnp.zeros((bsz, NH_A, DQK_A, DV_A + CHUNK), F32), jnp.zeros((bsz, 1, 128), F32))
    s_state = jnp.zeros((bsz, G_B, N_B, GP_B), F32)
    return (m_state, m_state, s_state, s_state)


def _layer_params(l, w_bf, conv_qk_w, conv_qk_b, gate_b, mh_norm_w, conv_xbc_w, conv_xbc_b,
                  dt_bias, a_log, d_skip, ssm_norm_w):
    w = w_bf[l]
    o_g = N_MAIN_A
    o_x = o_g + 4 * NH_A
    o_dt = o_x + W_B + 2 * G_B * N_B
    o_zb = o_dt + 2 * H_B
    cols = lambda a, n: w[:, a:a + n]
    w_main_b = jnp.concatenate([cols(o_x, W_B + 2 * G_B * N_B), cols(o_zb, W_B)], axis=1)[None]

    def pad_cols(a):
        return jnp.concatenate([a, jnp.zeros((a.shape[0], N_SMALL - a.shape[1]), a.dtype)], axis=1)

    def even_odd(a):
        a4 = a.reshape(a.shape[0], 2, NPAIR, 2)
        return a4.transpose(0, 1, 3, 2).reshape(a.shape[0], 2 * H_B)

    return {
        "w_main_b": w_main_b,
        "w_small_a": pad_cols(cols(o_g, 4 * NH_A)),
        "w_small_b": pad_cols(even_odd(cols(o_dt, 2 * H_B))),
        "conv_qk_w": conv_qk_w[l], "conv_qk_b": conv_qk_b[l].reshape(1, -1),
        "gate_b128": pad_cols(gate_b[l].reshape(1, -1)),
        "mh_w": mh_norm_w[l].reshape(1, W_A),
        "conv_xbc_w": conv_xbc_w[l], "conv_xbc_b": conv_xbc_b[l].reshape(1, -1),
        "dtb128": pad_cols(even_odd(dt_bias[l].reshape(1, -1))),
        "alog128": pad_cols(even_odd(a_log[l].reshape(1, -1))),
        "d_skip": jnp.repeat(d_skip[l], P_B).reshape(1, W_B),
        "ssm_w": ssm_norm_w[l].reshape(1, W_B),
    }


def kernel(x, c, ctx, c_ctx, w_ada, b_ada, w_in, conv_qk_w, conv_qk_b, gate_b, mh_norm_w,
           conv_xbc_w, conv_xbc_b, dt_bias, a_log, d_skip, ssm_norm_w, w_out, ln_g, ln_b):
    bsz, t_len, d = x.shape
    ctx_len = ctx.shape[1]
    depth = w_in.shape[0]
    assert d == D_MODEL and depth == DEPTH and bsz + 1 <= 8
    assert t_len % (CHUNK * GRID_W) == 0 and ctx_len % CHUNK == 0

    cvec = jnp.concatenate([c, c_ctx[None, :]], axis=0)
    mod = _ada_call(jnp.broadcast_to(cvec[:, :, None], (bsz + 1, d, 128)), w_ada, b_ada)

    w_bf = _cast_bf16(w_in, 256)
    w_out_bf = _cast_bf16(w_out, 512)

    x2d = x.reshape(bsz * t_len, d)
    xc2d = ctx.reshape(bsz * ctx_len, d)
    tm_in, tm_out, tm_ctx_in, tm_ctx_out = 1024, 256, bsz * ctx_len, 256
    lat_in = lambda i: i // (t_len // tm_in)
    lat_out = lambda i: i // (t_len // tm_out)
    ctx_index = lambda i: bsz

    for l in range(depth):
        last = l == depth - 1
        lp = _layer_params(l, w_bf, conv_qk_w, conv_qk_b, gate_b, mh_norm_w, conv_xbc_w,
                           conv_xbc_b, dt_bias, a_log, d_skip, ssm_norm_w)
        shift = mod[l, :, 0:d].reshape(8, 1, d)
        scale = mod[l, :, d:2 * d].reshape(8, 1, d)
        gate = mod[l, :, 2 * d:3 * d].reshape(8, 1, d)

        yc_a, yc_b, ctx_states = _mixer(xc2d, bsz, ctx_len, False, shift, scale, ctx_index,
                                        tm_ctx_in, w_bf, l, lp, _zero_states(bsz))
        y_a, y_b, _ = _mixer(x2d, bsz, t_len, True, shift, scale, lat_in, tm_in, w_bf, l, lp,
                             ctx_states)
        x2d = _out_proj_call(y_a, y_b, x2d, gate, w_out_bf, l, ln_g[l], ln_b[l], lat_out, tm_out)
        if not last:
            xc2d = _out_proj_call(yc_a, yc_b, xc2d, gate, w_out_bf, l, ln_g[l], ln_b[l],
                                  ctx_index, tm_ctx_out)
    return x2d.reshape(bsz, t_len, d)
```

```python
import functools

import jax
import jax.numpy as jnp
from jax import lax
from jax.experimental import pallas as pl
from jax.experimental.pallas import tpu as pltpu

F32 = jnp.float32
BF16 = jnp.bfloat16

D_MODEL = 2048
DEPTH = 2
GRID_W = 64
NH_A = 8
DV_A = 256
DQK_A = 128
W_A = NH_A * DV_A
QK_A = NH_A * DQK_A
P_B = 64
H_B = 32
G_B = 4
R_B = H_B // G_B
N_B = 128
W_B = H_B * P_B
GP_B = R_B * P_B
NPAIR = H_B // 2
CHUNK = 128
HALF = CHUNK // 2
LN_EPS = 1e-5
DEEPNORM_ALPHA = (2 * DEPTH) ** 0.25
LOG2E = 1.4426950408889634

CW = 1024
COL_Q, COL_K, COL_V, COL_O, COL_ZA = 0, 1, 2, 4, 6
N_MAIN_A = 8 * CW
COL_X, COL_BC, COL_ZB = 0, 2, 3
N_MAIN_B = 5 * CW
N_SMALL = 128

HALO = 16
VMEM_LIMIT = 56 * 1024 * 1024

PROJ_DTYPE = BF16
HB_DTYPE = BF16


def _cparams(sem):
    return pltpu.CompilerParams(dimension_semantics=sem, vmem_limit_bytes=VMEM_LIMIT)


def _softplus(x):
    return jnp.maximum(x, 0.0) + jnp.log1p(jnp.exp(-jnp.abs(x)))


def _log_sigmoid(x):
    return jnp.minimum(x, 0.0) - jnp.log1p(jnp.exp(-jnp.abs(x)))


def _silu(x):
    return x * jax.nn.sigmoid(x)


def _dot(a, b):
    return jnp.dot(a, b, preferred_element_type=F32)


def _dot_nt(a, b):
    return lax.dot_general(a, b, (((1,), (1,)), ((), ())), preferred_element_type=F32)


def _dot_tn(a, b):
    return lax.dot_general(a, b, (((0,), (0,)), ((), ())), preferred_element_type=F32)


def _cumsum_rows(tri, x):
    return jnp.dot(tri, x, preferred_element_type=F32, precision=lax.Precision.HIGHEST)


def _ada_kernel(c_ref, w_ref, b_ref, o_ref):
    n_rows, d, _ = c_ref.shape
    tn = w_ref.shape[2]
    o_ref[...] = jnp.zeros(o_ref.shape, F32)
    for r in range(n_rows):
        s = _silu(c_ref[r]).reshape(d // 8, 8, 128)
        for cb in range(tn // 128):
            cs = slice(cb * 128, (cb + 1) * 128)
            acc = jnp.sum(w_ref[0, :, cs].reshape(d // 8, 8, 128) * s, axis=0)
            o_ref[0, r:r + 1, cs] = jnp.sum(acc, axis=0, keepdims=True) + b_ref[0, :, cs]


def _ada_call(crep, w_ada, b_ada):
    depth, d, n3 = w_ada.shape
    n_rows = crep.shape[0]
    tn = 1024
    return pl.pallas_call(
        _ada_kernel,
        grid=(depth, n3 // tn),
        in_specs=[pl.BlockSpec((n_rows, d, 128), lambda l, j: (0, 0, 0)),
                  pl.BlockSpec((1, d, tn), lambda l, j: (l, 0, j)),
                  pl.BlockSpec((1, 1, tn), lambda l, j: (l, 0, j))],
        out_specs=pl.BlockSpec((1, 8, tn), lambda l, j: (l, 0, j)),
        out_shape=jax.ShapeDtypeStruct((depth, 8, n3), F32),
        compiler_params=_cparams(("arbitrary", "arbitrary")),
        name="adaln_mod",
    )(crep, w_ada, b_ada.reshape(depth, 1, n3))


def _cast_kernel(x_ref, o_ref):
    o_ref[...] = x_ref[...].astype(o_ref.dtype)


def _cast_bf16(w3, tr):
    depth, rows, cols = w3.shape
    spec = pl.BlockSpec((1, tr, cols), lambda l, i: (l, i, 0))
    return pl.pallas_call(
        _cast_kernel, grid=(depth, pl.cdiv(rows, tr)), in_specs=[spec], out_specs=spec,
        out_shape=jax.ShapeDtypeStruct(w3.shape, BF16),
        compiler_params=_cparams(("arbitrary", "arbitrary")),
        name="cast_bf16",
    )(w3)


def _in_proj_kernel(x_ref, shift_ref, scale_ref, w_ref, ws_ref, p_ref, small_ref, u_ref):
    @pl.when(pl.program_id(1) == 0)
    def _():
        x = x_ref[...]
        mu = jnp.mean(x, axis=-1, keepdims=True)
        xc = x - mu
        var = jnp.mean(xc * xc, axis=-1, keepdims=True)
        u = xc * lax.rsqrt(var + LN_EPS) * (1.0 + scale_ref[0]) + shift_ref[0]
        ub = u.astype(BF16)
        u_ref[...] = ub
        small_ref[...] = _dot_nt(ub, ws_ref[...])

    p_ref[...] = _dot_nt(u_ref[...], w_ref[0]).astype(p_ref.dtype)


def _in_proj_call(x2d, shift, scale, w3, layer, n_main, w_small, mod_index, tm):
    rows, d = x2d.shape
    tn = CW
    return pl.pallas_call(
        _in_proj_kernel,
        grid=(rows // tm, n_main // tn),
        in_specs=[pl.BlockSpec((tm, d), lambda i, j: (i, 0)),
                  pl.BlockSpec((1, 1, d), lambda i, j: (mod_index(i), 0, 0)),
                  pl.BlockSpec((1, 1, d), lambda i, j: (mod_index(i), 0, 0)),
                  pl.BlockSpec((1, tn, d), lambda i, j: (layer, j, 0)),
                  pl.BlockSpec((N_SMALL, d), lambda i, j: (0, 0))],
        out_specs=[pl.BlockSpec((tm, tn), lambda i, j: (i, j)),
                   pl.BlockSpec((tm, N_SMALL), lambda i, j: (i, 0))],
        out_shape=[jax.ShapeDtypeStruct((rows, n_main), PROJ_DTYPE),
                   jax.ShapeDtypeStruct((rows, N_SMALL), F32)],
        scratch_shapes=[pltpu.VMEM((tm, d), BF16)],
        compiler_params=_cparams(("arbitrary", "arbitrary")),
        name="in_proj",
    )(x2d, shift, scale, w3, w_small)


def _out_proj_kernel(ya_ref, yb_ref, x_ref, gate_ref, wa_ref, wb_ref, g_ref, b_ref, o_ref):
    acc = _dot(ya_ref[...], wa_ref[0]) + _dot(yb_ref[...], wb_ref[0])
    r = DEEPNORM_ALPHA * x_ref[...] + gate_ref[0] * acc
    mu = jnp.mean(r, axis=-1, keepdims=True)
    rc = r - mu
    var = jnp.mean(rc * rc, axis=-1, keepdims=True)
    o_ref[...] = rc * lax.rsqrt(var + LN_EPS) * g_ref[...] + b_ref[...]


def _out_proj_call(ya, yb, x2d, gate, w_out3, layer, ln_g, ln_b, mod_index, tm):
    rows, d = x2d.shape
    const = lambda i: (0, 0)
    return pl.pallas_call(
        _out_proj_kernel,
        grid=(rows // tm,),
        in_specs=[pl.BlockSpec((tm, W_A), lambda i: (i, 0)),
                  pl.BlockSpec((tm, W_B), lambda i: (i, 0)),
                  pl.BlockSpec((tm, d), lambda i: (i, 0)),
                  pl.BlockSpec((1, 1, d), lambda i: (mod_index(i), 0, 0)),
                  pl.BlockSpec((1, W_A, d), lambda i: (layer, 0, 0)),
                  pl.BlockSpec((1, W_B, d), lambda i: (layer, 1, 0)),
                  pl.BlockSpec((1, d), const),
                  pl.BlockSpec((1, d), const)],
        out_specs=pl.BlockSpec((tm, d), lambda i: (i, 0)),
        out_shape=jax.ShapeDtypeStruct((rows, d), F32),
        compiler_params=_cparams(("arbitrary",)),
        name="out_proj",
    )(ya, yb, x2d, gate, w_out3, w_out3, ln_g.reshape(1, d), ln_b.reshape(1, d))


def _conv_silu(x, prev_row, next_row, w, bias):
    rows = x.shape[0]
    row = lax.broadcasted_iota(jnp.int32, x.shape, 0)
    xm = jnp.where(row == 0, prev_row, pltpu.roll(x, 1, 0))
    xp = jnp.where(row == rows - 1, next_row, pltpu.roll(x, rows - 1, 0))
    y = w[0:1] * xm + w[1:2] * x + w[2:3] * xp + bias
    return _silu(y)


def _conv_block(x_ref, p_ref, n_ref, cs, has_prev, has_next, w, bias):
    pr = jnp.where(has_prev, p_ref[0, HALO - 1:HALO, cs].astype(F32), 0.0)
    nx = jnp.where(has_next, n_ref[0, 0:1, cs].astype(F32), 0.0)
    return _conv_silu(x_ref[0, :, cs].astype(F32), pr, nx, w, bias)


def _tri_masks(reverse):
    t = lax.broadcasted_iota(jnp.int32, (CHUNK, CHUNK), 0)
    s = lax.broadcasted_iota(jnp.int32, (CHUNK, CHUNK), 1)
    mask = (s >= t) if reverse else (s <= t)
    mask_t = (t >= s) if reverse else (t <= s)
    return mask, mask.astype(F32), mask_t.astype(F32)


def _chunk_pos(reverse):
    i = pl.program_id(1)
    nc = pl.num_programs(1)
    j = (nc - 1 - i) if reverse else i
    return i, j, nc


def _chunk_specs(t_len, reverse):
    nc = t_len // CHUNK
    nhb = t_len // HALO
    hpc = CHUNK // HALO

    def jj(i):
        return (nc - 1 - i) if reverse else i

    def main(col, width):
        return pl.BlockSpec((1, CHUNK, width), lambda b, i: (b, jj(i), (col * CW) // width))

    def prev(col):
        return pl.BlockSpec((1, HALO, CW), lambda b, i: (b, jnp.maximum(jj(i) * hpc - 1, 0), col))

    def nxt(col):
        return pl.BlockSpec((1, HALO, CW),
                            lambda b, i: (b, jnp.minimum((jj(i) + 1) * hpc, nhb - 1), col))

    def const2(shape):
        return pl.BlockSpec(shape, lambda b, i: (0, 0))

    return nc, main, prev, nxt, const2


def _mlstm_kernel(reverse, finalize, *refs):
    if finalize:
        (qk_ref, v_ref, gs_ref, gb_ref, c0_ref, m0_ref, hb_ref, o_ref, z_ref, mhw_ref,
         out_ref, cf_ref, mf_ref) = refs
    else:
        (q_ref, qp_ref, qn_ref, k_ref, kp_ref, kn_ref, v_ref, gs_ref, cw_ref, cb_ref, gb_ref,
         c0_ref, m0_ref, out_ref, qk_ref, cf_ref, mf_ref) = refs

    i, j, nc = _chunk_pos(reverse)

    @pl.when(i == 0)
    def _():
        cf_ref[...] = c0_ref[...]
        mf_ref[...] = m0_ref[...]

    mask, tri, tri_t = _tri_masks(reverse)
    has_prev = j > 0
    has_next = j < nc - 1
    edge = 0 if reverse else CHUNK - 1

    gb = gs_ref[0] + gb_ref[...]
    gbt = gb.T
    io = 2 * NH_A if reverse else 0
    b_all = _cumsum_rows(tri, _log_sigmoid(gb))
    li_c = gb[:, io:io + NH_A]
    b_c = b_all[:, io + NH_A:io + 2 * NH_A]
    b_r = jnp.dot(_log_sigmoid(gbt[io + NH_A:io + 2 * NH_A, :]), tri_t,
                  preferred_element_type=F32, precision=lax.Precision.HIGHEST)
    c_r = gbt[io:io + NH_A, :] - b_r

    m_prev = mf_ref[0][:, 0:NH_A]
    g_c = b_c[edge:edge + 1, :]
    a_c = g_c - b_c + li_c
    m_new = jnp.maximum(g_c + m_prev, jnp.max(a_c, axis=0, keepdims=True))
    w_c = jnp.exp(a_c - m_new)
    dec = jnp.exp(g_c + m_prev - m_new)

    heads = range(NH_A)
    ksl = [slice(h * DQK_A, (h + 1) * DQK_A) for h in heads]
    vsl = [slice(h * DV_A, (h + 1) * DV_A) for h in heads]
    kksl = [slice(QK_A + h * DQK_A, QK_A + (h + 1) * DQK_A) for h in heads]

    if finalize:
        qb = [qk_ref[0, :, ksl[h]] for h in heads]
        kb = [qk_ref[0, :, kksl[h]] for h in heads]
        qh = [t.astype(F32) for t in qb]
        kh = [t.astype(F32) for t in kb]
        gate = []
        for h in heads:
            o = o_ref[0, :, vsl[h]].astype(F32)
            z = z_ref[0, :, vsl[h]].astype(F32)
            gate.append(z / ((1.0 + jnp.exp(-o)) * (1.0 + jnp.exp(-z))))
    else:
        qh = [_conv_block(q_ref, qp_ref, qn_ref, ksl[h], has_prev, has_next,
                          cw_ref[:, ksl[h]], cb_ref[:, ksl[h]]) * (DQK_A ** -0.5) for h in heads]
        kh = [_conv_block(k_ref, kp_ref, kn_ref, ksl[h], has_prev, has_next,
                          cw_ref[:, kksl[h]], cb_ref[:, kksl[h]]) for h in heads]
        qb = [t.astype(BF16) for t in qh]
        kb = [t.astype(BF16) for t in kh]
        for h in heads:
            qk_ref[0, :, ksl[h]] = qb[h]
            qk_ref[0, :, kksl[h]] = kb[h]
    ones_l = jnp.ones((CHUNK, CHUNK), BF16)
    vb = [jnp.concatenate([v_ref[0, :, vsl[h]].astype(BF16), ones_l], axis=1) for h in heads]
    c_prev = [cf_ref[0, h] for h in heads]
    mp = [m_prev[:, h:h + 1] for h in heads]

    s_qk = [_dot_nt(qb[h], kb[h]) for h in heads]
    q_c = [_dot(qb[h], c_prev[h].astype(BF16)) for h in heads]
    cm = [jnp.where(mask, c_r[h:h + 1, :], -jnp.inf) for h in heads]
    mm = [jnp.broadcast_to(jnp.maximum(jnp.max(cm[h], axis=1, keepdims=True), mp[h]),
                           (CHUNK, CHUNK)) for h in heads]

    kw = [kh[h] * w_c[:, h:h + 1] for h in heads]
    for h in heads:
        cf_ref[0, h] = dec[:, h:h + 1] * c_prev[h] + _dot_tn(kw[h].astype(BF16), vb[h])
    mf_ref[0, :, 0:NH_A] = m_new

    p = [s_qk[h] * jnp.exp(cm[h] - mm[h]) for h in heads]
    p_v = [_dot(p[h].astype(BF16), vb[h]) for h in heads]
    w_int = [jnp.exp(mp[h] - mm[h]) for h in heads]
    den = [w_int[h] * q_c[h][:, DV_A:] + p_v[h][:, DV_A:] for h in heads]
    b_t = [jnp.broadcast_to(b_c[:, h:h + 1], (CHUNK, CHUNK)) for h in heads]
    inv = [1.0 / jnp.maximum(jnp.abs(den[h]), jnp.exp(-(b_t[h] + mm[h]))) for h in heads]
    hval = [jnp.concatenate(
        [(w_int[h] * q_c[h][:, c * CHUNK:(c + 1) * CHUNK] + p_v[h][:, c * CHUNK:(c + 1) * CHUNK])
         * inv[h] for c in range(DV_A // CHUNK)], axis=1) for h in heads]

    if not finalize:
        for h in heads:
            out_ref[0, :, vsl[h]] = hval[h].astype(out_ref.dtype)
        return

    hs = [hval[h] + hb_ref[0, :, vsl[h]].astype(F32) for h in heads]
    mu = [jnp.mean(hs[h], axis=-1, keepdims=True) for h in heads]
    hc = [hs[h] - mu[h] for h in heads]
    var = [jnp.mean(hc[h] * hc[h], axis=-1, keepdims=True) for h in heads]
    for h in heads:
        hn = hc[h] * lax.rsqrt(var[h] + LN_EPS) * mhw_ref[:, vsl[h]]
        out_ref[0, :, vsl[h]] = (hn * gate[h]).astype(out_ref.dtype)


def _mlstm_call(reverse, finalize, state, *, p3, small3, gate_b128, conv_w=None, conv_b=None,
                qk=None, hb=None, mh_w=None):
    bsz, t_len, _ = p3.shape
    nc, main, prev, nxt, const2 = _chunk_specs(t_len, reverse)
    c0, m0 = state
    state_specs = [pl.BlockSpec((1, NH_A, DQK_A, DV_A + CHUNK), lambda b, i: (b, 0, 0, 0)),
                   pl.BlockSpec((1, 1, 128), lambda b, i: (b, 0, 0))]
    state_shapes = [jax.ShapeDtypeStruct(c0.shape, F32), jax.ShapeDtypeStruct(m0.shape, F32)]
    wide = main(0, W_A)
    if finalize:
        in_specs = [wide, main(COL_V, W_A), main(0, N_SMALL), const2((1, N_SMALL))] + state_specs
        in_specs += [wide, main(COL_O, W_A), main(COL_ZA, W_A), const2((1, W_A))]
        args = [qk, p3, small3, gate_b128, c0, m0, hb, p3, p3, mh_w]
        out_specs = [wide] + state_specs
        out_shape = [jax.ShapeDtypeStruct((bsz, t_len, W_A), BF16)] + state_shapes
    else:
        in_specs = [main(COL_Q, CW), prev(COL_Q), nxt(COL_Q),
                    main(COL_K, CW), prev(COL_K), nxt(COL_K),
                    main(COL_V, W_A), main(0, N_SMALL),
                    const2((3, 2 * QK_A)), const2((1, 2 * QK_A)), const2((1, N_SMALL))] + state_specs
        args = [p3, p3, p3, p3, p3, p3, p3, small3, conv_w, conv_b, gate_b128, c0, m0]
        out_specs = [wide, wide] + state_specs
        out_shape = [jax.ShapeDtypeStruct((bsz, t_len, W_A), HB_DTYPE),
                     jax.ShapeDtypeStruct((bsz, t_len, 2 * QK_A), BF16)] + state_shapes
    res = pl.pallas_call(
        functools.partial(_mlstm_kernel, reverse, finalize),
        grid=(bsz, nc), in_specs=in_specs, out_specs=out_specs, out_shape=out_shape,
        compiler_params=_cparams(("arbitrary", "arbitrary")),
        name="mlstm_" + ("bwd" if reverse else "fwd"),
    )(*args)
    if finalize:
        return res[0], tuple(res[1:])
    return res[0], res[1], tuple(res[2:])


def _ssd_kernel(reverse, finalize, *refs):
    if finalize:
        (xc_ref, bcc_ref, gs_ref, dtb_ref, alog_ref, s0_ref, yb_ref, z0_ref, z1_ref, dsk_ref,
         nw_ref, out_ref, sf_ref, y_s) = refs
    else:
        (x0_ref, x0p_ref, x0n_ref, x1_ref, x1p_ref, x1n_ref, bc_ref, bcp_ref, bcn_ref,
         gs_ref, cw_ref, cb_ref, dtb_ref, alog_ref, s0_ref, out_ref, xc_ref, sf_ref) = refs

    i, j, nc = _chunk_pos(reverse)

    @pl.when(i == 0)
    def _():
        sf_ref[...] = s0_ref[...]

    _, tri, _ = _tri_masks(reverse)
    has_prev = j > 0
    has_next = j < nc - 1
    edge = 0 if reverse else CHUNK - 1

    t_i = lax.broadcasted_iota(jnp.int32, (CHUNK, CHUNK), 0)
    l_i = lax.broadcasted_iota(jnp.int32, (CHUNK, CHUNK), 1)
    s_lo = jnp.where(l_i < HALF, l_i, l_i - HALF)
    lo_half = l_i < HALF
    if reverse:
        mask1, mask2 = s_lo >= t_i, s_lo + HALF >= t_i
    else:
        mask1, mask2 = s_lo <= t_i, s_lo + HALF <= t_i
    k_i = lax.broadcasted_iota(jnp.int32, (2 * CHUNK, CHUNK), 0)
    kl_i = lax.broadcasted_iota(jnp.int32, (2 * CHUNK, CHUNK), 1)
    blockdiag = ((k_i // HALF) % 2 == 1) == (kl_i >= HALF)

    dt_all = _softplus(gs_ref[0] + dtb_ref[...])
    da_all = dt_all * (-LOG2E * jnp.exp(alog_ref[...]))
    acum_all = _cumsum_rows(tri, da_all)
    do = H_B if reverse else 0
    acum_t = acum_all.T
    a_even = acum_t[do:do + NPAIR, :]
    a_odd = acum_t[do + NPAIR:do + H_B, :]
    lane16 = lax.broadcasted_iota(jnp.int32, (NPAIR, CHUNK), 1) < HALF
    rv1 = jnp.where(lane16, a_even, pltpu.roll(a_odd, HALF, 1))
    rv2 = jnp.where(lane16, pltpu.roll(a_even, HALF, 1), a_odd)

    if finalize:
        bmat = bcc_ref[0, :, 0:GP_B]
        cmat = bcc_ref[0, :, GP_B:2 * GP_B]
        ssq = jnp.zeros((CHUNK, CHUNK), F32)
    else:
        bmat = _conv_block(bc_ref, bcp_ref, bcn_ref, slice(0, GP_B), has_prev, has_next,
                           cw_ref[:, W_B:W_B + GP_B], cb_ref[:, W_B:W_B + GP_B]).astype(BF16)
        cmat = _conv_block(bc_ref, bcp_ref, bcn_ref, slice(GP_B, 2 * GP_B), has_prev, has_next,
                           cw_ref[:, W_B + GP_B:W_B + 2 * GP_B],
                           cb_ref[:, W_B + GP_B:W_B + 2 * GP_B]).astype(BF16)
        xc_ref[0, :, W_B:W_B + GP_B] = bmat
        xc_ref[0, :, W_B + GP_B:W_B + 2 * GP_B] = cmat

    for g in range(G_B):
        gs = slice(g * GP_B, (g + 1) * GP_B)
        if finalize:
            xg = xc_ref[0, :, gs].astype(F32)
        else:
            xr, xpr, xnr = (x0_ref, x0p_ref, x0n_ref) if g < 2 else (x1_ref, x1p_ref, x1n_ref)
            cs = slice((g % 2) * GP_B, (g % 2 + 1) * GP_B)
            xg = _conv_block(xr, xpr, xnr, cs, has_prev, has_next, cw_ref[:, gs], cb_ref[:, gs])
            xc_ref[0, :, gs] = xg.astype(BF16)
        bg = bmat[:, g * N_B:(g + 1) * N_B]
        cg = cmat[:, g * N_B:(g + 1) * N_B]
        cb1 = _dot_nt(cg, jnp.concatenate([bg[0:HALF], bg[0:HALF]], axis=0))
        cb2 = _dot_nt(cg, jnp.concatenate([bg[HALF:], bg[HALF:]], axis=0))
        s_prev = sf_ref[0, g]
        cs_prev = _dot(cg, s_prev.astype(BF16))
        y_parts, xw_parts, tot_parts = [], [], []
        for pp in range(R_B // 2):
            pr = g * (R_B // 2) + pp
            ps = slice(pp * CHUNK, (pp + 1) * CHUNK)
            idx = jnp.where(lo_half, do + pr, do + NPAIR + pr)
            acum_b = jnp.take_along_axis(acum_all, idx, axis=1)
            dt_b = jnp.take_along_axis(dt_all, idx, axis=1)
            xdt = xg[:, ps] * dt_b
            xdtb = xdt.astype(BF16)
            w1 = cb1 * jnp.exp2(jnp.where(mask1, acum_b - rv1[pr:pr + 1, :], -jnp.inf))
            w2 = cb2 * jnp.exp2(jnp.where(mask2, acum_b - rv2[pr:pr + 1, :], -jnp.inf))
            w12 = jnp.concatenate([w1.astype(BF16), w2.astype(BF16)], axis=1)
            x12 = jnp.concatenate([xdtb[0:HALF], xdtb[0:HALF], xdtb[HALF:], xdtb[HALF:]], axis=0)
            x12 = jnp.where(blockdiag, x12, jnp.zeros_like(x12))
            tot = acum_b[edge:edge + 1, :]
            y_parts.append(_dot(w12, x12) + jnp.exp2(acum_b) * cs_prev[:, ps])
            xw_parts.append((xdt * jnp.exp2(tot - acum_b)).astype(BF16))
            tot_parts.append(tot)
        xw = jnp.concatenate(xw_parts, axis=1)
        etot = jnp.exp2(jnp.concatenate(tot_parts, axis=1))
        sf_ref[0, g] = etot * s_prev + _dot_tn(bg, xw)
        y_grp = jnp.concatenate(y_parts, axis=1)
        if finalize:
            z_ref = z0_ref if g < 2 else z1_ref
            zs = slice((g % 2) * GP_B, (g % 2 + 1) * GP_B)
            y = y_grp + yb_ref[0, :, gs].astype(F32) + dsk_ref[:, gs] * xg
            y = y * _silu(z_ref[0, :, zs].astype(F32))
            y_s[:, gs] = y
            y2 = y * y
            ssq = ssq + (y2[:, 0:CHUNK] + y2[:, CHUNK:2 * CHUNK]
                         + y2[:, 2 * CHUNK:3 * CHUNK] + y2[:, 3 * CHUNK:])
        else:
            out_ref[0, :, gs] = y_grp.astype(out_ref.dtype)

    if finalize:
        inv = lax.rsqrt(jnp.sum(ssq, axis=-1, keepdims=True) * (1.0 / W_B) + LN_EPS)
        for half in range(2):
            cs = slice(half * CW, (half + 1) * CW)
            out_ref[0, :, cs] = (y_s[:, cs] * inv * nw_ref[:, cs]).astype(out_ref.dtype)


def _ssd_call(reverse, finalize, s0, *, small3, dtb128, alog128, p3, conv_w=None, conv_b=None,
              xc=None, yb=None, d_skip=None, norm_w=None):
    bsz, t_len, _ = p3.shape
    nc, main, prev, nxt, const2 = _chunk_specs(t_len, reverse)
    state_spec = pl.BlockSpec((1, G_B, N_B, GP_B), lambda b, i: (b, 0, 0, 0))
    state_shape = jax.ShapeDtypeStruct(s0.shape, F32)
    xc_spec = main(0, W_B + 2 * GP_B)
    if finalize:
        in_specs = [main(0, W_B), main(2, CW), main(0, N_SMALL), const2((1, N_SMALL)),
                    const2((1, N_SMALL)), state_spec,
                    main(0, W_B), main(COL_ZB, CW), main(COL_ZB + 1, CW),
                    const2((1, W_B)), const2((1, W_B))]
        args = [xc, xc, small3, dtb128, alog128, s0, yb, p3, p3, d_skip, norm_w]
        out_specs = [main(0, W_B), state_spec]
        out_shape = [jax.ShapeDtypeStruct((bsz, t_len, W_B), BF16), state_shape]
        scratch = [pltpu.VMEM((CHUNK, W_B), F32)]
    else:
        in_specs = [main(COL_X, CW), prev(COL_X), nxt(COL_X),
                    main(COL_X + 1, CW), prev(COL_X + 1), nxt(COL_X + 1),
                    main(COL_BC, CW), prev(COL_BC), nxt(COL_BC),
                    main(0, N_SMALL),
                    const2((3, 3 * CW)), const2((1, 3 * CW)), const2((1, N_SMALL)),
                    const2((1, N_SMALL)), state_spec]
        args = [p3] * 9 + [small3, conv_w, conv_b, dtb128, alog128, s0]
        out_specs = [main(0, W_B), xc_spec, state_spec]
        out_shape = [jax.ShapeDtypeStruct((bsz, t_len, W_B), HB_DTYPE),
                     jax.ShapeDtypeStruct((bsz, t_len, W_B + 2 * GP_B), BF16), state_shape]
        scratch = []
    return pl.pallas_call(
        functools.partial(_ssd_kernel, reverse, finalize),
        grid=(bsz, nc), in_specs=in_specs, out_specs=out_specs, out_shape=out_shape,
        scratch_shapes=scratch,
        compiler_params=_cparams(("arbitrary", "arbitrary")),
        name="ssd_" + ("bwd" if reverse else "fwd"),
    )(*args)


def _to_colmajor(t, bsz):
    ch = t.shape[-1]
    return t.reshape(bsz, -1, GRID_W, ch).transpose(0, 2, 1, 3).reshape(-1, ch)


def _from_colmajor(t, bsz):
    ch = t.shape[-1]
    return t.reshape(bsz, GRID_W, -1, ch).transpose(0, 2, 1, 3).reshape(-1, ch)


def _mixer_scans(pa, sa, pb, sb, bsz, t_len, lp, states):
    st_mf, st_mb, st_sf, st_sb = states
    pa = pa.reshape(bsz, t_len, N_MAIN_A)
    sa = sa.reshape(bsz, t_len, N_SMALL)
    h_b, qk, st_mb = _mlstm_call(True, False, st_mb, p3=pa, small3=sa, gate_b128=lp["gate_b128"],
                                 conv_w=lp["conv_qk_w"], conv_b=lp["conv_qk_b"])
    y_a, st_mf = _mlstm_call(False, True, st_mf, p3=pa, small3=sa, gate_b128=lp["gate_b128"],
                             qk=qk, hb=h_b, mh_w=lp["mh_w"])

    pb = pb.reshape(bsz, t_len, N_MAIN_B)
    sb = sb.reshape(bsz, t_len, N_SMALL)
    y_bb, xc, st_sb = _ssd_call(True, False, st_sb, small3=sb, dtb128=lp["dtb128"],
                                alog128=lp["alog128"], p3=pb, conv_w=lp["conv_xbc_w"],
                                conv_b=lp["conv_xbc_b"])
    y_b, st_sf = _ssd_call(False, True, st_sf, small3=sb, dtb128=lp["dtb128"],
                           alog128=lp["alog128"], p3=pb, xc=xc, yb=y_bb, d_skip=lp["d_skip"],
                           norm_w=lp["ssm_w"])
    return y_a, y_b, (st_mf, st_mb, st_sf, st_sb)


def _mixer(x2d, bsz, t_len, latent, shift, scale, mod_index, tm, w_bf, l, lp, states):
    pa, sa = _in_proj_call(x2d, shift, scale, w_bf, l, N_MAIN_A, lp["w_small_a"], mod_index, tm)
    xs2d = _to_colmajor(x2d, bsz) if latent else x2d
    pb, sb = _in_proj_call(xs2d, shift, scale, lp["w_main_b"], 0, N_MAIN_B, lp["w_small_b"],
                           mod_index, tm)
    y_a, y_b, states = _mixer_scans(pa, sa, pb, sb, bsz, t_len, lp, states)
    y_b = y_b.reshape(bsz * t_len, W_B)
    if latent:
        y_b = _from_colmajor(y_b, bsz)
    return y_a.reshape(bsz * t_len, W_A), y_b, states


def _zero_states(bsz):
    m_state = (jnp.zeros((bsz, NH_A, DQK_A, DV_A + CHUNK), F32), jnp.zeros((bsz, 1, 128), F32))
    s_state = jnp.zeros((bsz, G_B, N_B, GP_B), F32)
    return (m_state, m_state, s_state, s_state)


def _layer_params(l, wt_bf, conv_qk_w, conv_qk_b, gate_b, mh_norm_w, conv_xbc_w, conv_xbc_b,
                  dt_bias, a_log, d_skip, ssm_norm_w):
    w = wt_bf[l]
    o_g = N_MAIN_A
    o_x = o_g + 4 * NH_A
    o_dt = o_x + W_B + 2 * G_B * N_B
    o_zb = o_dt + 2 * H_B
    feats = lambda a, n: w[a:a + n]
    w_main_b = jnp.concatenate([feats(o_x, W_B + 2 * G_B * N_B), feats(o_zb, W_B)], axis=0)[None]

    def pad_cols(a):
        return jnp.concatenate([a, jnp.zeros((a.shape[0], N_SMALL - a.shape[1]), a.dtype)], axis=1)

    def pad_rows(a):
        return jnp.concatenate([a, jnp.zeros((N_SMALL - a.shape[0], a.shape[1]), a.dtype)], axis=0)

    def even_odd(a):
        a4 = a.reshape(a.shape[0], 2, NPAIR, 2)
        return a4.transpose(0, 1, 3, 2).reshape(a.shape[0], 2 * H_B)

    return {
        "w_main_b": w_main_b,
        "w_small_a": pad_rows(feats(o_g, 4 * NH_A)),
        "w_small_b": pad_rows(even_odd(feats(o_dt, 2 * H_B).T).T),
        "conv_qk_w": conv_qk_w[l], "conv_qk_b": conv_qk_b[l].reshape(1, -1),
        "gate_b128": pad_cols(gate_b[l].reshape(1, -1)),
        "mh_w": mh_norm_w[l].reshape(1, W_A),
        "conv_xbc_w": conv_xbc_w[l], "conv_xbc_b": conv_xbc_b[l].reshape(1, -1),
        "dtb128": pad_cols(even_odd(dt_bias[l].reshape(1, -1))),
        "alog128": pad_cols(even_odd(a_log[l].reshape(1, -1))),
        "d_skip": jnp.repeat(d_skip[l], P_B).reshape(1, W_B),
        "ssm_w": ssm_norm_w[l].reshape(1, W_B),
    }


def kernel(x, c, ctx, c_ctx, w_ada, b_ada, w_in, conv_qk_w, conv_qk_b, gate_b, mh_norm_w,
           conv_xbc_w, conv_xbc_b, dt_bias, a_log, d_skip, ssm_norm_w, w_out, ln_g, ln_b):
    bsz, t_len, d = x.shape
    ctx_len = ctx.shape[1]
    depth = w_in.shape[0]
    assert d == D_MODEL and depth == DEPTH and bsz + 1 <= 8
    assert t_len % (CHUNK * GRID_W) == 0 and ctx_len % CHUNK == 0

    cvec = jnp.concatenate([c, c_ctx[None, :]], axis=0)
    mod = _ada_call(jnp.broadcast_to(cvec[:, :, None], (bsz + 1, d, 128)), w_ada, b_ada)

    w_bf = _cast_bf16(jnp.swapaxes(w_in, 1, 2), 512)
    w_out_bf = _cast_bf16(w_out, 512)

    x2d = x.reshape(bsz * t_len, d)
    xc2d = ctx.reshape(bsz * ctx_len, d)
    tm_in, tm_out, tm_ctx_in, tm_ctx_out = 1024, 512, bsz * ctx_len, 256
    lat_in = lambda i: i // (t_len // tm_in)
    lat_out = lambda i: i // (t_len // tm_out)
    ctx_index = lambda i: bsz

    for l in range(depth):
        last = l == depth - 1
        lp = _layer_params(l, w_bf, conv_qk_w, conv_qk_b, gate_b, mh_norm_w, conv_xbc_w,
                           conv_xbc_b, dt_bias, a_log, d_skip, ssm_norm_w)
        shift = mod[l, :, 0:d].reshape(8, 1, d)
        scale = mod[l, :, d:2 * d].reshape(8, 1, d)
        gate = mod[l, :, 2 * d:3 * d].reshape(8, 1, d)

        yc_a, yc_b, ctx_states = _mixer(xc2d, bsz, ctx_len, False, shift, scale, ctx_index,
                                        tm_ctx_in, w_bf, l, lp, _zero_states(bsz))
        y_a, y_b, _ = _mixer(x2d, bsz, t_len, True, shift, scale, lat_in, tm_in, w_bf, l, lp,
                             ctx_states)
        x2d = _out_proj_call(y_a, y_b, x2d, gate, w_out_bf, l, ln_g[l], ln_b[l], lat_out, tm_out)
        if not last:
            xc2d = _out_proj_call(yc_a, yc_b, xc2d, gate, w_out_bf, l, ln_g[l], ln_b[l],
                                  ctx_index, tm_ctx_out)
    return x2d.reshape(bsz, t_len, d)
```

```python
import functools

import jax
import jax.numpy as jnp
from jax import lax
from jax.experimental import pallas as pl
from jax.experimental.pallas import tpu as pltpu

F32 = jnp.float32
BF16 = jnp.bfloat16

D_MODEL = 2048
DEPTH = 2
GRID_W = 64
NH_A = 8
DV_A = 256
DQK_A = 128
W_A = NH_A * DV_A
QK_A = NH_A * DQK_A
P_B = 64
H_B = 32
G_B = 4
R_B = H_B // G_B
N_B = 128
W_B = H_B * P_B
GP_B = R_B * P_B
NPAIR = H_B // 2
CHUNK = 128
HALF = CHUNK // 2
SUB = 2
BLK = SUB * CHUNK
LN_EPS = 1e-5
DEEPNORM_ALPHA = (2 * DEPTH) ** 0.25
LOG2E = 1.4426950408889634

CW = 1024
COL_Q, COL_K, COL_V, COL_O, COL_ZA = 0, 1, 2, 4, 6
N_MAIN_A = 8 * CW
COL_X, COL_BC, COL_ZB = 0, 2, 3
N_MAIN_B = 5 * CW
N_SMALL = 128

HALO = 16
VMEM_LIMIT = 56 * 1024 * 1024

PROJ_DTYPE = BF16
HB_DTYPE = BF16


def _cparams(sem):
    return pltpu.CompilerParams(dimension_semantics=sem, vmem_limit_bytes=VMEM_LIMIT)


def _softplus(x):
    return jnp.maximum(x, 0.0) + jnp.log1p(jnp.exp(-jnp.abs(x)))


def _log_sigmoid(x):
    return jnp.minimum(x, 0.0) - jnp.log1p(jnp.exp(-jnp.abs(x)))


def _silu(x):
    return x * jax.nn.sigmoid(x)


def _dot(a, b):
    return jnp.dot(a, b, preferred_element_type=F32)


def _dot_nt(a, b):
    return lax.dot_general(a, b, (((1,), (1,)), ((), ())), preferred_element_type=F32)


def _dot_tn(a, b):
    return lax.dot_general(a, b, (((0,), (0,)), ((), ())), preferred_element_type=F32)


def _cumsum_rows(tri, x):
    return jnp.dot(tri, x, preferred_element_type=F32, precision=lax.Precision.HIGHEST)


def _ada_kernel(c_ref, w_ref, b_ref, o_ref):
    n_rows, d, _ = c_ref.shape
    tn = w_ref.shape[2]
    o_ref[...] = jnp.zeros(o_ref.shape, F32)
    for r in range(n_rows):
        s = _silu(c_ref[r]).reshape(d // 8, 8, 128)
        for cb in range(tn // 128):
            cs = slice(cb * 128, (cb + 1) * 128)
            acc = jnp.sum(w_ref[0, :, cs].reshape(d // 8, 8, 128) * s, axis=0)
            o_ref[0, r:r + 1, cs] = jnp.sum(acc, axis=0, keepdims=True) + b_ref[0, :, cs]


def _ada_call(crep, w_ada, b_ada):
    depth, d, n3 = w_ada.shape
    n_rows = crep.shape[0]
    tn = 1024
    return pl.pallas_call(
        _ada_kernel,
        grid=(depth, n3 // tn),
        in_specs=[pl.BlockSpec((n_rows, d, 128), lambda l, j: (0, 0, 0)),
                  pl.BlockSpec((1, d, tn), lambda l, j: (l, 0, j)),
                  pl.BlockSpec((1, 1, tn), lambda l, j: (l, 0, j))],
        out_specs=pl.BlockSpec((1, 8, tn), lambda l, j: (l, 0, j)),
        out_shape=jax.ShapeDtypeStruct((depth, 8, n3), F32),
        compiler_params=_cparams(("arbitrary", "arbitrary")),
        name="adaln_mod",
    )(crep, w_ada, b_ada.reshape(depth, 1, n3))


def _cast_kernel(x_ref, o_ref):
    o_ref[...] = x_ref[...].astype(o_ref.dtype)


def _cast_bf16(w3, tr):
    depth, rows, cols = w3.shape
    spec = pl.BlockSpec((1, tr, cols), lambda l, i: (l, i, 0))
    return pl.pallas_call(
        _cast_kernel, grid=(depth, pl.cdiv(rows, tr)), in_specs=[spec], out_specs=spec,
        out_shape=jax.ShapeDtypeStruct(w3.shape, BF16),
        compiler_params=_cparams(("arbitrary", "arbitrary")),
        name="cast_bf16",
    )(w3)


def _cast_rows_bf16(w3, segments, tr):
    depth, _, cols = w3.shape
    starts = []
    for start, length in segments:
        assert length % tr == 0
        starts += [start + k * tr for k in range(length // tr)]
    n_blk = len(starts)

    def src_row(i):
        row = starts[0]
        for k in range(1, n_blk):
            row = jnp.where(i >= k, starts[k], row)
        return pl.multiple_of(row, 32)

    return pl.pallas_call(
        _cast_kernel, grid=(depth, n_blk),
        in_specs=[pl.BlockSpec((pl.Element(1), pl.Element(tr), pl.Element(cols)),
                               lambda l, i: (l, src_row(i), 0))],
        out_specs=pl.BlockSpec((1, tr, cols), lambda l, i: (l, i, 0)),
        out_shape=jax.ShapeDtypeStruct((depth, n_blk * tr, cols), BF16),
        compiler_params=_cparams(("arbitrary", "arbitrary")),
        name="cast_rows_bf16",
    )(w3)


def _in_proj_kernel(x_ref, shift_ref, scale_ref, w_ref, ws_ref, p_ref, small_ref, u_ref):
    @pl.when(pl.program_id(1) == 0)
    def _():
        x = x_ref[...]
        mu = jnp.mean(x, axis=-1, keepdims=True)
        xc = x - mu
        var = jnp.mean(xc * xc, axis=-1, keepdims=True)
        u = xc * lax.rsqrt(var + LN_EPS) * (1.0 + scale_ref[0]) + shift_ref[0]
        ub = u.astype(BF16)
        u_ref[...] = ub
        small_ref[...] = _dot_nt(ub, ws_ref[...])

    p_ref[...] = _dot_nt(u_ref[...], w_ref[0]).astype(p_ref.dtype)


def _proj_kernel(u_ref, w_ref, ws_ref, p_ref, small_ref):
    @pl.when(pl.program_id(1) == 0)
    def _():
        small_ref[...] = _dot_nt(u_ref[...], ws_ref[...])

    p_ref[...] = _dot_nt(u_ref[...], w_ref[0]).astype(p_ref.dtype)


def _in_proj_call(x2d, shift, scale, w3, layer, n_main, w_small, mod_index, tm):
    rows, d = x2d.shape
    tn = CW
    return pl.pallas_call(
        _in_proj_kernel,
        grid=(rows // tm, n_main // tn),
        in_specs=[pl.BlockSpec((tm, d), lambda i, j: (i, 0)),
                  pl.BlockSpec((1, 1, d), lambda i, j: (mod_index(i), 0, 0)),
                  pl.BlockSpec((1, 1, d), lambda i, j: (mod_index(i), 0, 0)),
                  pl.BlockSpec((1, tn, d), lambda i, j: (layer, j, 0)),
                  pl.BlockSpec((N_SMALL, d), lambda i, j: (0, 0))],
        out_specs=[pl.BlockSpec((tm, tn), lambda i, j: (i, j)),
                   pl.BlockSpec((tm, N_SMALL), lambda i, j: (i, 0)),
                   pl.BlockSpec((tm, d), lambda i, j: (i, 0))],
        out_shape=[jax.ShapeDtypeStruct((rows, n_main), PROJ_DTYPE),
                   jax.ShapeDtypeStruct((rows, N_SMALL), F32),
                   jax.ShapeDtypeStruct((rows, d), BF16)],
        compiler_params=_cparams(("arbitrary", "arbitrary")),
        name="in_proj",
    )(x2d, shift, scale, w3, w_small)


def _proj_call(u2d, w3, layer, n_main, w_small, tm):
    rows, d = u2d.shape
    tn = CW
    return pl.pallas_call(
        _proj_kernel,
        grid=(rows // tm, n_main // tn),
        in_specs=[pl.BlockSpec((tm, d), lambda i, j: (i, 0)),
                  pl.BlockSpec((1, tn, d), lambda i, j: (layer, j, 0)),
                  pl.BlockSpec((N_SMALL, d), lambda i, j: (0, 0))],
        out_specs=[pl.BlockSpec((tm, tn), lambda i, j: (i, j)),
                   pl.BlockSpec((tm, N_SMALL), lambda i, j: (i, 0))],
        out_shape=[jax.ShapeDtypeStruct((rows, n_main), PROJ_DTYPE),
                   jax.ShapeDtypeStruct((rows, N_SMALL), F32)],
        compiler_params=_cparams(("arbitrary", "arbitrary")),
        name="ssd_proj",
    )(u2d, w3, w_small)


def _out_proj_kernel(ya_ref, yb_ref, x_ref, gate_ref, wa_ref, wb_ref, g_ref, b_ref, o_ref):
    acc = _dot(ya_ref[...], wa_ref[0]) + _dot(yb_ref[...], wb_ref[0])
    r = DEEPNORM_ALPHA * x_ref[...] + gate_ref[0] * acc
    mu = jnp.mean(r, axis=-1, keepdims=True)
    rc = r - mu
    var = jnp.mean(rc * rc, axis=-1, keepdims=True)
    o_ref[...] = rc * lax.rsqrt(var + LN_EPS) * g_ref[...] + b_ref[...]


def _out_proj_call(ya, yb, x2d, gate, w_out3, layer, ln_g, ln_b, mod_index, tm):
    rows, d = x2d.shape
    const = lambda i: (0, 0)
    return pl.pallas_call(
        _out_proj_kernel,
        grid=(rows // tm,),
        in_specs=[pl.BlockSpec((tm, W_A), lambda i: (i, 0)),
                  pl.BlockSpec((tm, W_B), lambda i: (i, 0)),
                  pl.BlockSpec((tm, d), lambda i: (i, 0)),
                  pl.BlockSpec((1, 1, d), lambda i: (mod_index(i), 0, 0)),
                  pl.BlockSpec((1, W_A, d), lambda i: (layer, 0, 0)),
                  pl.BlockSpec((1, W_B, d), lambda i: (layer, 1, 0)),
                  pl.BlockSpec((1, d), const),
                  pl.BlockSpec((1, d), const)],
        out_specs=pl.BlockSpec((tm, d), lambda i: (i, 0)),
        out_shape=jax.ShapeDtypeStruct((rows, d), F32),
        compiler_params=_cparams(("arbitrary",)),
        name="out_proj",
    )(ya, yb, x2d, gate, w_out3, w_out3, ln_g.reshape(1, d), ln_b.reshape(1, d))


def _conv_silu(x, prev_row, next_row, w, bias):
    rows = x.shape[0]
    row = lax.broadcasted_iota(jnp.int32, x.shape, 0)
    xm = jnp.where(row == 0, prev_row, pltpu.roll(x, 1, 0))
    xp = jnp.where(row == rows - 1, next_row, pltpu.roll(x, rows - 1, 0))
    y = w[0:1] * xm + w[1:2] * x + w[2:3] * xp + bias
    return _silu(y)


def _conv_block(x_ref, p_ref, n_ref, sc, cs, has_prev, has_next, w, bias):
    r0 = sc * CHUNK
    if sc > 0:
        pr = x_ref[0, r0 - 1:r0, cs].astype(F32)
    else:
        pr = jnp.where(has_prev, p_ref[0, HALO - 1:HALO, cs].astype(F32), 0.0)
    if sc < SUB - 1:
        nx = x_ref[0, r0 + CHUNK:r0 + CHUNK + 1, cs].astype(F32)
    else:
        nx = jnp.where(has_next, n_ref[0, 0:1, cs].astype(F32), 0.0)
    return _conv_silu(x_ref[0, r0:r0 + CHUNK, cs].astype(F32), pr, nx, w, bias)


def _tri_masks(reverse):
    t = lax.broadcasted_iota(jnp.int32, (CHUNK, CHUNK), 0)
    s = lax.broadcasted_iota(jnp.int32, (CHUNK, CHUNK), 1)
    mask = (s >= t) if reverse else (s <= t)
    mask_t = (t >= s) if reverse else (t <= s)
    return mask, mask.astype(F32), mask_t.astype(F32)


def _chunk_pos(reverse):
    i = pl.program_id(1)
    nc = pl.num_programs(1)
    j = (nc - 1 - i) if reverse else i
    return i, j, nc


def _chunk_specs(t_len, reverse):
    nc = t_len // BLK
    nhb = t_len // HALO
    hpc = BLK // HALO

    def jj(i):
        return (nc - 1 - i) if reverse else i

    def main(col, width):
        return pl.BlockSpec((1, BLK, width), lambda b, i: (b, jj(i), (col * CW) // width))

    def prev(col):
        return pl.BlockSpec((1, HALO, CW), lambda b, i: (b, jnp.maximum(jj(i) * hpc - 1, 0), col))

    def nxt(col):
        return pl.BlockSpec((1, HALO, CW),
                            lambda b, i: (b, jnp.minimum((jj(i) + 1) * hpc, nhb - 1), col))

    def const2(shape):
        return pl.BlockSpec(shape, lambda b, i: (0, 0))

    return nc, main, prev, nxt, const2


def _mlstm_kernel(reverse, finalize, *refs):
    if finalize:
        (qk_ref, v_ref, gs_ref, gb_ref, c0_ref, m0_ref, hb_ref, o_ref, z_ref, mhw_ref,
         out_ref, cf_ref, mf_ref) = refs
    else:
        (q_ref, qp_ref, qn_ref, k_ref, kp_ref, kn_ref, v_ref, gs_ref, cw_ref, cb_ref, gb_ref,
         c0_ref, m0_ref, out_ref, qk_ref, cf_ref, mf_ref) = refs

    i, j, nc = _chunk_pos(reverse)

    @pl.when(i == 0)
    def _():
        cf_ref[...] = c0_ref[...]
        mf_ref[...] = m0_ref[...]

    has_prev = j > 0
    has_next = j < nc - 1
    for sc in (reversed(range(SUB)) if reverse else range(SUB)):
        _mlstm_chunk(reverse, finalize, sc, has_prev, has_next, refs)


def _mlstm_chunk(reverse, finalize, sc, has_prev, has_next, refs):
    if finalize:
        (qk_ref, v_ref, gs_ref, gb_ref, c0_ref, m0_ref, hb_ref, o_ref, z_ref, mhw_ref,
         out_ref, cf_ref, mf_ref) = refs
    else:
        (q_ref, qp_ref, qn_ref, k_ref, kp_ref, kn_ref, v_ref, gs_ref, cw_ref, cb_ref, gb_ref,
         c0_ref, m0_ref, out_ref, qk_ref, cf_ref, mf_ref) = refs
    rows = slice(sc * CHUNK, (sc + 1) * CHUNK)
    mask, tri, tri_t = _tri_masks(reverse)
    edge = 0 if reverse else CHUNK - 1

    gb = gs_ref[0, rows, :] + gb_ref[...]
    gbt = gb.T
    io = 2 * NH_A if reverse else 0
    b_all = _cumsum_rows(tri, _log_sigmoid(gb))
    li_c = gb[:, io:io + NH_A]
    b_c = b_all[:, io + NH_A:io + 2 * NH_A]
    b_r = jnp.dot(_log_sigmoid(gbt[io + NH_A:io + 2 * NH_A, :]), tri_t,
                  preferred_element_type=F32, precision=lax.Precision.HIGHEST)
    c_r = gbt[io:io + NH_A, :] - b_r

    m_prev = mf_ref[0][:, 0:NH_A]
    g_c = b_c[edge:edge + 1, :]
    a_c = g_c - b_c + li_c
    m_new = jnp.maximum(g_c + m_prev, jnp.max(a_c, axis=0, keepdims=True))
    w_c = jnp.exp(a_c - m_new)
    dec = jnp.exp(g_c + m_prev - m_new)

    heads = range(NH_A)
    ksl = [slice(h * DQK_A, (h + 1) * DQK_A) for h in heads]
    vsl = [slice(h * DV_A, (h + 1) * DV_A) for h in heads]
    kksl = [slice(QK_A + h * DQK_A, QK_A + (h + 1) * DQK_A) for h in heads]

    if finalize:
        qb = [qk_ref[0, rows, ksl[h]] for h in heads]
        kb = [qk_ref[0, rows, kksl[h]] for h in heads]
        qh = [t.astype(F32) for t in qb]
        kh = [t.astype(F32) for t in kb]
        gate = []
        for h in heads:
            o = o_ref[0, rows, vsl[h]].astype(F32)
            z = z_ref[0, rows, vsl[h]].astype(F32)
            gate.append(z / ((1.0 + jnp.exp(-o)) * (1.0 + jnp.exp(-z))))
    else:
        qh = [_conv_block(q_ref, qp_ref, qn_ref, sc, ksl[h], has_prev, has_next,
                          cw_ref[:, ksl[h]], cb_ref[:, ksl[h]]) * (DQK_A ** -0.5) for h in heads]
        kh = [_conv_block(k_ref, kp_ref, kn_ref, sc, ksl[h], has_prev, has_next,
                          cw_ref[:, kksl[h]], cb_ref[:, kksl[h]]) for h in heads]
        qb = [t.astype(BF16) for t in qh]
        kb = [t.astype(BF16) for t in kh]
        for h in heads:
            qk_ref[0, rows, ksl[h]] = qb[h]
            qk_ref[0, rows, kksl[h]] = kb[h]
    ones_l = jnp.ones((CHUNK, CHUNK), BF16)
    vb = [jnp.concatenate([v_ref[0, rows, vsl[h]].astype(BF16), ones_l], axis=1) for h in heads]
    c_prev = [cf_ref[0, h] for h in heads]
    mp = [m_prev[:, h:h + 1] for h in heads]

    s_qk = [_dot_nt(qb[h], kb[h]) for h in heads]
    q_c = [_dot(qb[h], c_prev[h].astype(BF16)) for h in heads]
    cm = [jnp.where(mask, c_r[h:h + 1, :], -jnp.inf) for h in heads]
    mm = [jnp.broadcast_to(jnp.maximum(jnp.max(cm[h], axis=1, keepdims=True), mp[h]),
                           (CHUNK, CHUNK)) for h in heads]

    kw = [kh[h] * w_c[:, h:h + 1] for h in heads]
    for h in heads:
        cf_ref[0, h] = dec[:, h:h + 1] * c_prev[h] + _dot_tn(kw[h].astype(BF16), vb[h])
    mf_ref[0, :, 0:NH_A] = m_new

    p = [s_qk[h] * jnp.exp(cm[h] - mm[h]) for h in heads]
    p_v = [_dot(p[h].astype(BF16), vb[h]) for h in heads]
    w_int = [jnp.exp(mp[h] - mm[h]) for h in heads]
    den = [w_int[h] * q_c[h][:, DV_A:] + p_v[h][:, DV_A:] for h in heads]
    b_t = [jnp.broadcast_to(b_c[:, h:h + 1], (CHUNK, CHUNK)) for h in heads]
    inv = [1.0 / jnp.maximum(jnp.abs(den[h]), jnp.exp(-(b_t[h] + mm[h]))) for h in heads]
    hval = [jnp.concatenate(
        [(w_int[h] * q_c[h][:, c * CHUNK:(c + 1) * CHUNK] + p_v[h][:, c * CHUNK:(c + 1) * CHUNK])
         * inv[h] for c in range(DV_A // CHUNK)], axis=1) for h in heads]

    if not finalize:
        for h in heads:
            out_ref[0, rows, vsl[h]] = hval[h].astype(out_ref.dtype)
        return

    hs = [hval[h] + hb_ref[0, rows, vsl[h]].astype(F32) for h in heads]
    mu = [jnp.mean(hs[h], axis=-1, keepdims=True) for h in heads]
    hc = [hs[h] - mu[h] for h in heads]
    var = [jnp.mean(hc[h] * hc[h], axis=-1, keepdims=True) for h in heads]
    for h in heads:
        hn = hc[h] * lax.rsqrt(var[h] + LN_EPS) * mhw_ref[:, vsl[h]]
        out_ref[0, rows, vsl[h]] = (hn * gate[h]).astype(out_ref.dtype)


def _mlstm_call(reverse, finalize, state, *, p3, small3, gate_b128, conv_w=None, conv_b=None,
                qk=None, hb=None, mh_w=None):
    bsz, t_len, _ = p3.shape
    nc, main, prev, nxt, const2 = _chunk_specs(t_len, reverse)
    c0, m0 = state
    state_specs = [pl.BlockSpec((1, NH_A, DQK_A, DV_A + CHUNK), lambda b, i: (b, 0, 0, 0)),
                   pl.BlockSpec((1, 1, 128), lambda b, i: (b, 0, 0))]
    state_shapes = [jax.ShapeDtypeStruct(c0.shape, F32), jax.ShapeDtypeStruct(m0.shape, F32)]
    wide = main(0, W_A)
    if finalize:
        in_specs = [wide, main(COL_V, W_A), main(0, N_SMALL), const2((1, N_SMALL))] + state_specs
        in_specs += [wide, main(COL_O, W_A), main(COL_ZA, W_A), const2((1, W_A))]
        args = [qk, p3, small3, gate_b128, c0, m0, hb, p3, p3, mh_w]
        out_specs = [wide] + state_specs
        out_shape = [jax.ShapeDtypeStruct((bsz, t_len, W_A), BF16)] + state_shapes
    else:
        in_specs = [main(COL_Q, CW), prev(COL_Q), nxt(COL_Q),
                    main(COL_K, CW), prev(COL_K), nxt(COL_K),
                    main(COL_V, W_A), main(0, N_SMALL),
                    const2((3, 2 * QK_A)), const2((1, 2 * QK_A)), const2((1, N_SMALL))] + state_specs
        args = [p3, p3, p3, p3, p3, p3, p3, small3, conv_w, conv_b, gate_b128, c0, m0]
        out_specs = [wide, wide] + state_specs
        out_shape = [jax.ShapeDtypeStruct((bsz, t_len, W_A), HB_DTYPE),
                     jax.ShapeDtypeStruct((bsz, t_len, 2 * QK_A), BF16)] + state_shapes
    res = pl.pallas_call(
        functools.partial(_mlstm_kernel, reverse, finalize),
        grid=(bsz, nc), in_specs=in_specs, out_specs=out_specs, out_shape=out_shape,
        compiler_params=_cparams(("arbitrary", "arbitrary")),
        name="mlstm_" + ("bwd" if reverse else "fwd"),
    )(*args)
    if finalize:
        return res[0], tuple(res[1:])
    return res[0], res[1], tuple(res[2:])


def _ssd_kernel(reverse, finalize, *refs):
    if finalize:
        (xc_ref, bcc_ref, gs_ref, dtb_ref, alog_ref, s0_ref, yb_ref, z0_ref, z1_ref, dsk_ref,
         nw_ref, out_ref, sf_ref, y_s) = refs
    else:
        (x0_ref, x0p_ref, x0n_ref, x1_ref, x1p_ref, x1n_ref, bc_ref, bcp_ref, bcn_ref,
         gs_ref, cw_ref, cb_ref, dtb_ref, alog_ref, s0_ref, out_ref, xc_ref, sf_ref) = refs

    i, j, nc = _chunk_pos(reverse)

    @pl.when(i == 0)
    def _():
        sf_ref[...] = s0_ref[...]

    has_prev = j > 0
    has_next = j < nc - 1
    for sc in (reversed(range(SUB)) if reverse else range(SUB)):
        _ssd_chunk(reverse, finalize, sc, has_prev, has_next, refs)


def _ssd_chunk(reverse, finalize, sc, has_prev, has_next, refs):
    if finalize:
        (xc_ref, bcc_ref, gs_ref, dtb_ref, alog_ref, s0_ref, yb_ref, z0_ref, z1_ref, dsk_ref,
         nw_ref, out_ref, sf_ref, y_s) = refs
    else:
        (x0_ref, x0p_ref, x0n_ref, x1_ref, x1p_ref, x1n_ref, bc_ref, bcp_ref, bcn_ref,
         gs_ref, cw_ref, cb_ref, dtb_ref, alog_ref, s0_ref, out_ref, xc_ref, sf_ref) = refs
    rows = slice(sc * CHUNK, (sc + 1) * CHUNK)
    _, tri, _ = _tri_masks(reverse)
    edge = 0 if reverse else CHUNK - 1

    t_i = lax.broadcasted_iota(jnp.int32, (CHUNK, CHUNK), 0)
    l_i = lax.broadcasted_iota(jnp.int32, (CHUNK, CHUNK), 1)
    s_lo = jnp.where(l_i < HALF, l_i, l_i - HALF)
    lo_half = l_i < HALF
    if reverse:
        mask1, mask2 = s_lo >= t_i, s_lo + HALF >= t_i
    else:
        mask1, mask2 = s_lo <= t_i, s_lo + HALF <= t_i
    k_i = lax.broadcasted_iota(jnp.int32, (2 * CHUNK, CHUNK), 0)
    kl_i = lax.broadcasted_iota(jnp.int32, (2 * CHUNK, CHUNK), 1)
    blockdiag = ((k_i // HALF) % 2 == 1) == (kl_i >= HALF)

    dt_all = _softplus(gs_ref[0, rows, :] + dtb_ref[...])
    da_all = dt_all * (-LOG2E * jnp.exp(alog_ref[...]))
    acum_all = _cumsum_rows(tri, da_all)
    do = H_B if reverse else 0
    acum_t = acum_all.T
    a_even = acum_t[do:do + NPAIR, :]
    a_odd = acum_t[do + NPAIR:do + H_B, :]
    lane16 = lax.broadcasted_iota(jnp.int32, (NPAIR, CHUNK), 1) < HALF
    rv1 = jnp.where(lane16, a_even, pltpu.roll(a_odd, HALF, 1))
    rv2 = jnp.where(lane16, pltpu.roll(a_even, HALF, 1), a_odd)

    if finalize:
        bmat = bcc_ref[0, rows, 0:GP_B]
        cmat = bcc_ref[0, rows, GP_B:2 * GP_B]
        ssq = jnp.zeros((CHUNK, CHUNK), F32)
    else:
        bmat = _conv_block(bc_ref, bcp_ref, bcn_ref, sc, slice(0, GP_B), has_prev, has_next,
                           cw_ref[:, W_B:W_B + GP_B], cb_ref[:, W_B:W_B + GP_B]).astype(BF16)
        cmat = _conv_block(bc_ref, bcp_ref, bcn_ref, sc, slice(GP_B, 2 * GP_B), has_prev, has_next,
                           cw_ref[:, W_B + GP_B:W_B + 2 * GP_B],
                           cb_ref[:, W_B + GP_B:W_B + 2 * GP_B]).astype(BF16)
        xc_ref[0, rows, W_B:W_B + GP_B] = bmat
        xc_ref[0, rows, W_B + GP_B:W_B + 2 * GP_B] = cmat

    for g in range(G_B):
        gs = slice(g * GP_B, (g + 1) * GP_B)
        if finalize:
            xg = xc_ref[0, rows, gs].astype(F32)
        else:
            xr, xpr, xnr = (x0_ref, x0p_ref, x0n_ref) if g < 2 else (x1_ref, x1p_ref, x1n_ref)
            cs = slice((g % 2) * GP_B, (g % 2 + 1) * GP_B)
            xg = _conv_block(xr, xpr, xnr, sc, cs, has_prev, has_next, cw_ref[:, gs], cb_ref[:, gs])
            xc_ref[0, rows, gs] = xg.astype(BF16)
        bg = bmat[:, g * N_B:(g + 1) * N_B]
        cg = cmat[:, g * N_B:(g + 1) * N_B]
        cb1 = _dot_nt(cg, jnp.concatenate([bg[0:HALF], bg[0:HALF]], axis=0))
        cb2 = _dot_nt(cg, jnp.concatenate([bg[HALF:], bg[HALF:]], axis=0))
        s_prev = sf_ref[0, g]
        cs_prev = _dot(cg, s_prev.astype(BF16))
        y_parts, xw_parts, tot_parts = [], [], []
        for pp in range(R_B // 2):
            pr = g * (R_B // 2) + pp
            ps = slice(pp * CHUNK, (pp + 1) * CHUNK)
            idx = jnp.where(lo_half, do + pr, do + NPAIR + pr)
            acum_b = jnp.take_along_axis(acum_all, idx, axis=1)
            dt_b = jnp.take_along_axis(dt_all, idx, axis=1)
            xdt = xg[:, ps] * dt_b
            xdtb = xdt.astype(BF16)
            w1 = cb1 * jnp.exp2(jnp.where(mask1, acum_b - rv1[pr:pr + 1, :], -jnp.inf))
            w2 = cb2 * jnp.exp2(jnp.where(mask2, acum_b - rv2[pr:pr + 1, :], -jnp.inf))
            w12 = jnp.concatenate([w1.astype(BF16), w2.astype(BF16)], axis=1)
            x12 = jnp.concatenate([xdtb[0:HALF], xdtb[0:HALF], xdtb[HALF:], xdtb[HALF:]], axis=0)
            x12 = jnp.where(blockdiag, x12, jnp.zeros_like(x12))
            tot = acum_b[edge:edge + 1, :]
            y_parts.append(_dot(w12, x12) + jnp.exp2(acum_b) * cs_prev[:, ps])
            xw_parts.append((xdt * jnp.exp2(tot - acum_b)).astype(BF16))
            tot_parts.append(tot)
        xw = jnp.concatenate(xw_parts, axis=1)
        etot = jnp.exp2(jnp.concatenate(tot_parts, axis=1))
        sf_ref[0, g] = etot * s_prev + _dot_tn(bg, xw)
        y_grp = jnp.concatenate(y_parts, axis=1)
        if finalize:
            z_ref = z0_ref if g < 2 else z1_ref
            zs = slice((g % 2) * GP_B, (g % 2 + 1) * GP_B)
            y = y_grp + yb_ref[0, rows, gs].astype(F32) + dsk_ref[:, gs] * xg
            y = y * _silu(z_ref[0, rows, zs].astype(F32))
            y_s[:, gs] = y
            y2 = y * y
            ssq = ssq + (y2[:, 0:CHUNK] + y2[:, CHUNK:2 * CHUNK]
                         + y2[:, 2 * CHUNK:3 * CHUNK] + y2[:, 3 * CHUNK:])
        else:
            out_ref[0, rows, gs] = y_grp.astype(out_ref.dtype)

    if finalize:
        inv = lax.rsqrt(jnp.sum(ssq, axis=-1, keepdims=True) * (1.0 / W_B) + LN_EPS)
        for half in range(2):
            cs = slice(half * CW, (half + 1) * CW)
            out_ref[0, rows, cs] = (y_s[:, cs] * inv * nw_ref[:, cs]).astype(out_ref.dtype)


def _ssd_call(reverse, finalize, s0, *, small3, dtb128, alog128, p3, conv_w=None, conv_b=None,
              xc=None, yb=None, d_skip=None, norm_w=None):
    bsz, t_len, _ = p3.shape
    nc, main, prev, nxt, const2 = _chunk_specs(t_len, reverse)
    state_spec = pl.BlockSpec((1, G_B, N_B, GP_B), lambda b, i: (b, 0, 0, 0))
    state_shape = jax.ShapeDtypeStruct(s0.shape, F32)
    xc_spec = main(0, W_B + 2 * GP_B)
    if finalize:
        in_specs = [main(0, W_B), main(2, CW), main(0, N_SMALL), const2((1, N_SMALL)),
                    const2((1, N_SMALL)), state_spec,
                    main(0, W_B), main(COL_ZB, CW), main(COL_ZB + 1, CW),
                    const2((1, W_B)), const2((1, W_B))]
        args = [xc, xc, small3, dtb128, alog128, s0, yb, p3, p3, d_skip, norm_w]
        out_specs = [main(0, W_B), state_spec]
        out_shape = [jax.ShapeDtypeStruct((bsz, t_len, W_B), BF16), state_shape]
        scratch = [pltpu.VMEM((CHUNK, W_B), F32)]
    else:
        in_specs = [main(COL_X, CW), prev(COL_X), nxt(COL_X),
                    main(COL_X + 1, CW), prev(COL_X + 1), nxt(COL_X + 1),
                    main(COL_BC, CW), prev(COL_BC), nxt(COL_BC),
                    main(0, N_SMALL),
                    const2((3, 3 * CW)), const2((1, 3 * CW)), const2((1, N_SMALL)),
                    const2((1, N_SMALL)), state_spec]
        args = [p3] * 9 + [small3, conv_w, conv_b, dtb128, alog128, s0]
        out_specs = [main(0, W_B), xc_spec, state_spec]
        out_shape = [jax.ShapeDtypeStruct((bsz, t_len, W_B), HB_DTYPE),
                     jax.ShapeDtypeStruct((bsz, t_len, W_B + 2 * GP_B), BF16), state_shape]
        scratch = []
    return pl.pallas_call(
        functools.partial(_ssd_kernel, reverse, finalize),
        grid=(bsz, nc), in_specs=in_specs, out_specs=out_specs, out_shape=out_shape,
        scratch_shapes=scratch,
        compiler_params=_cparams(("arbitrary", "arbitrary")),
        name="ssd_" + ("bwd" if reverse else "fwd"),
    )(*args)


def _to_colmajor(t, bsz):
    ch = t.shape[-1]
    return t.reshape(bsz, -1, GRID_W, ch).transpose(0, 2, 1, 3).reshape(-1, ch)


def _from_colmajor(t, bsz):
    ch = t.shape[-1]
    return t.reshape(bsz, GRID_W, -1, ch).transpose(0, 2, 1, 3).reshape(-1, ch)


def _mixer_scans(pa, sa, pb, sb, bsz, t_len, lp, states):
    st_mf, st_mb, st_sf, st_sb = states
    pa = pa.reshape(bsz, t_len, N_MAIN_A)
    sa = sa.reshape(bsz, t_len, N_SMALL)
    h_b, qk, st_mb = _mlstm_call(True, False, st_mb, p3=pa, small3=sa, gate_b128=lp["gate_b128"],
                                 conv_w=lp["conv_qk_w"], conv_b=lp["conv_qk_b"])
    y_a, st_mf = _mlstm_call(False, True, st_mf, p3=pa, small3=sa, gate_b128=lp["gate_b128"],
                             qk=qk, hb=h_b, mh_w=lp["mh_w"])

    pb = pb.reshape(bsz, t_len, N_MAIN_B)
    sb = sb.reshape(bsz, t_len, N_SMALL)
    y_bb, xc, st_sb = _ssd_call(True, False, st_sb, small3=sb, dtb128=lp["dtb128"],
                                alog128=lp["alog128"], p3=pb, conv_w=lp["conv_xbc_w"],
                                conv_b=lp["conv_xbc_b"])
    y_b, st_sf = _ssd_call(False, True, st_sf, small3=sb, dtb128=lp["dtb128"],
                           alog128=lp["alog128"], p3=pb, xc=xc, yb=y_bb, d_skip=lp["d_skip"],
                           norm_w=lp["ssm_w"])
    return y_a, y_b, (st_mf, st_mb, st_sf, st_sb)


def _mixer(x2d, bsz, t_len, latent, shift, scale, mod_index, tm, w_bf, w_b_bf, l, lp, states):
    pa, sa, u2d = _in_proj_call(x2d, shift, scale, w_bf, l, N_MAIN_A, lp["w_small_a"],
                                mod_index, tm)
    us2d = _to_colmajor(u2d, bsz) if latent else u2d
    pb, sb = _proj_call(us2d, w_b_bf, l, N_MAIN_B, lp["w_small_b"], tm)
    y_a, y_b, states = _mixer_scans(pa, sa, pb, sb, bsz, t_len, lp, states)
    y_b = y_b.reshape(bsz * t_len, W_B)
    if latent:
        y_b = _from_colmajor(y_b, bsz)
    return y_a.reshape(bsz * t_len, W_A), y_b, states


def _zero_states(bsz):
    m_state = (jnp.zeros((bsz, NH_A, DQK_A, DV_A + CHUNK), F32), jnp.zeros((bsz, 1, 128), F32))
    s_state = jnp.zeros((bsz, G_B, N_B, GP_B), F32)
    return (m_state, m_state, s_state, s_state)


def _layer_params(l, wt_bf, conv_qk_w, conv_qk_b, gate_b, mh_norm_w, conv_xbc_w, conv_xbc_b,
                  dt_bias, a_log, d_skip, ssm_norm_w):
    w = wt_bf[l]
    o_g = N_MAIN_A
    o_x = o_g + 4 * NH_A
    o_dt = o_x + W_B + 2 * G_B * N_B
    o_zb = o_dt + 2 * H_B
    feats = lambda a, n: w[a:a + n]

    def pad_cols(a):
        return jnp.concatenate([a, jnp.zeros((a.shape[0], N_SMALL - a.shape[1]), a.dtype)], axis=1)

    def pad_rows(a):
        return jnp.concatenate([a, jnp.zeros((N_SMALL - a.shape[0], a.shape[1]), a.dtype)], axis=0)

    def even_odd(a):
        a4 = a.reshape(a.shape[0], 2, NPAIR, 2)
        return a4.transpose(0, 1, 3, 2).reshape(a.shape[0], 2 * H_B)

    return {
        "w_small_a": pad_rows(feats(o_g, 4 * NH_A)),
        "w_small_b": pad_rows(even_odd(feats(o_dt, 2 * H_B).T).T),
        "conv_qk_w": conv_qk_w[l], "conv_qk_b": conv_qk_b[l].reshape(1, -1),
        "gate_b128": pad_cols(gate_b[l].reshape(1, -1)),
        "mh_w": mh_norm_w[l].reshape(1, W_A),
        "conv_xbc_w": conv_xbc_w[l], "conv_xbc_b": conv_xbc_b[l].reshape(1, -1),
        "dtb128": pad_cols(even_odd(dt_bias[l].reshape(1, -1))),
        "alog128": pad_cols(even_odd(a_log[l].reshape(1, -1))),
        "d_skip": jnp.repeat(d_skip[l], P_B).reshape(1, W_B),
        "ssm_w": ssm_norm_w[l].reshape(1, W_B),
    }


def kernel(x, c, ctx, c_ctx, w_ada, b_ada, w_in, conv_qk_w, conv_qk_b, gate_b, mh_norm_w,
           conv_xbc_w, conv_xbc_b, dt_bias, a_log, d_skip, ssm_norm_w, w_out, ln_g, ln_b):
    bsz, t_len, d = x.shape
    ctx_len = ctx.shape[1]
    depth = w_in.shape[0]
    assert d == D_MODEL and depth == DEPTH and bsz + 1 <= 8
    assert t_len % (CHUNK * GRID_W) == 0 and ctx_len % CHUNK == 0

    cvec = jnp.concatenate([c, c_ctx[None, :]], axis=0)
    mod = _ada_call(jnp.broadcast_to(cvec[:, :, None], (bsz + 1, d, 128)), w_ada, b_ada)

    w_t = jnp.swapaxes(w_in, 1, 2)
    w_bf = _cast_bf16(w_t, 2048)
    o_xbc = N_MAIN_A + 4 * NH_A
    o_zb = o_xbc + W_B + 2 * G_B * N_B + 2 * H_B
    w_b_bf = _cast_rows_bf16(w_t, [(o_xbc, W_B + 2 * G_B * N_B), (o_zb, W_B)], 512)
    w_out_bf = _cast_bf16(w_out, 512)

    x2d = x.reshape(bsz * t_len, d)
    xc2d = ctx.reshape(bsz * ctx_len, d)
    tm_in, tm_out, tm_ctx_in, tm_ctx_out = 1024, 512, bsz * ctx_len, 256
    lat_in = lambda i: i // (t_len // tm_in)
    lat_out = lambda i: i // (t_len // tm_out)
    ctx_index = lambda i: bsz

    for l in range(depth):
        last = l == depth - 1
        lp = _layer_params(l, w_bf, conv_qk_w, conv_qk_b, gate_b, mh_norm_w, conv_xbc_w,
                           conv_xbc_b, dt_bias, a_log, d_skip, ssm_norm_w)
        shift = mod[l, :, 0:d].reshape(8, 1, d)
        scale = mod[l, :, d:2 * d].reshape(8, 1, d)
        gate = mod[l, :, 2 * d:3 * d].reshape(8, 1, d)

        yc_a, yc_b, ctx_states = _mixer(xc2d, bsz, ctx_len, False, shift, scale, ctx_index,
                                        tm_ctx_in, w_bf, w_b_bf, l, lp, _zero_states(bsz))
        y_a, y_b, _ = _mixer(x2d, bsz, t_len, True, shift, scale, lat_in, tm_in, w_bf, w_b_bf, l,
                             lp, ctx_states)
        x2d = _out_proj_call(y_a, y_b, x2d, gate, w_out_bf, l, ln_g[l], ln_b[l], lat_out, tm_out)
        if not last:
            xc2d = _out_proj_call(yc_a, yc_b, xc2d, gate, w_out_bf, l, ln_g[l], ln_b[l],
                                  ctx_index, tm_ctx_out)
    return x2d.reshape(bsz, t_len, d)
```

```python
import functools

import jax
import jax.numpy as jnp
from jax import lax
from jax.experimental import pallas as pl
from jax.experimental.pallas import tpu as pltpu

F32 = jnp.float32
BF16 = jnp.bfloat16

D_MODEL = 2048
DEPTH = 2
GRID_W = 64
NH_A = 8
DV_A = 256
DQK_A = 128
W_A = NH_A * DV_A
QK_A = NH_A * DQK_A
P_B = 64
H_B = 32
G_B = 4
R_B = H_B // G_B
N_B = 128
W_B = H_B * P_B
GP_B = R_B * P_B
NPAIR = H_B // 2
CHUNK = 128
HALF = CHUNK // 2
SUB = 2
BLK = SUB * CHUNK
LN_EPS = 1e-5
DEEPNORM_ALPHA = (2 * DEPTH) ** 0.25
LOG2E = 1.4426950408889634

CW = 1024
COL_Q, COL_K, COL_V, COL_O, COL_ZA = 0, 1, 2, 4, 6
N_MAIN_A = 8 * CW
COL_X, COL_BC, COL_ZB = 0, 2, 3
N_MAIN_B = 5 * CW
N_SMALL = 128
O_GATE = N_MAIN_A
O_XBC = O_GATE + 4 * NH_A
O_DT = O_XBC + W_B + 2 * G_B * N_B
O_ZB = O_DT + 2 * H_B
A_STARTS = [k * CW for k in range(N_MAIN_A // CW)]
B_STARTS = ([O_XBC + k * CW for k in range((W_B + 2 * G_B * N_B) // CW)]
            + [O_ZB + k * CW for k in range(W_B // CW)])

HALO = 16
VMEM_LIMIT = 56 * 1024 * 1024

PROJ_DTYPE = BF16
HB_DTYPE = BF16


def _cparams(sem):
    return pltpu.CompilerParams(dimension_semantics=sem, vmem_limit_bytes=VMEM_LIMIT)


def _softplus(x):
    return jnp.maximum(x, 0.0) + jnp.log1p(jnp.exp(-jnp.abs(x)))


def _log_sigmoid(x):
    return jnp.minimum(x, 0.0) - jnp.log1p(jnp.exp(-jnp.abs(x)))


def _silu(x):
    return x * jax.nn.sigmoid(x)


def _dot(a, b):
    return jnp.dot(a, b, preferred_element_type=F32)


def _dot_nt(a, b):
    return lax.dot_general(a, b, (((1,), (1,)), ((), ())), preferred_element_type=F32)


def _dot_tn(a, b):
    return lax.dot_general(a, b, (((0,), (0,)), ((), ())), preferred_element_type=F32)


def _cumsum_rows(tri, x):
    return jnp.dot(tri, x, preferred_element_type=F32, precision=lax.Precision.HIGHEST)


def _ada_kernel(c_ref, w_ref, b_ref, o_ref):
    n_rows, d, _ = c_ref.shape
    tn = w_ref.shape[2]
    o_ref[...] = jnp.zeros(o_ref.shape, F32)
    for r in range(n_rows):
        s = _silu(c_ref[r]).reshape(d // 8, 8, 128)
        for cb in range(tn // 128):
            cs = slice(cb * 128, (cb + 1) * 128)
            acc = jnp.sum(w_ref[0, :, cs].reshape(d // 8, 8, 128) * s, axis=0)
            o_ref[0, r:r + 1, cs] = jnp.sum(acc, axis=0, keepdims=True) + b_ref[0, :, cs]


def _ada_call(crep, w_ada, b_ada):
    depth, d, n3 = w_ada.shape
    n_rows = crep.shape[0]
    tn = 1024
    return pl.pallas_call(
        _ada_kernel,
        grid=(depth, n3 // tn),
        in_specs=[pl.BlockSpec((n_rows, d, 128), lambda l, j: (0, 0, 0)),
                  pl.BlockSpec((1, d, tn), lambda l, j: (l, 0, j)),
                  pl.BlockSpec((1, 1, tn), lambda l, j: (l, 0, j))],
        out_specs=pl.BlockSpec((1, 8, tn), lambda l, j: (l, 0, j)),
        out_shape=jax.ShapeDtypeStruct((depth, 8, n3), F32),
        compiler_params=_cparams(("arbitrary", "arbitrary")),
        name="adaln_mod",
    )(crep, w_ada, b_ada.reshape(depth, 1, n3))


def _cast_kernel(x_ref, o_ref):
    o_ref[...] = x_ref[...].astype(o_ref.dtype)


def _cast_bf16(w3, tr):
    depth, rows, cols = w3.shape
    spec = pl.BlockSpec((1, tr, cols), lambda l, i: (l, i, 0))
    return pl.pallas_call(
        _cast_kernel, grid=(depth, pl.cdiv(rows, tr)), in_specs=[spec], out_specs=spec,
        out_shape=jax.ShapeDtypeStruct(w3.shape, BF16),
        compiler_params=_cparams(("arbitrary", "arbitrary")),
        name="cast_bf16",
    )(w3)


def _proj_tile(cast_w, u_ref, w_ref, p_ref, wout):
    if cast_w:
        w = w_ref[0].astype(BF16)
        wout[0][...] = w
    else:
        w = w_ref[...]
    p_ref[...] = _dot_nt(u_ref[...], w).astype(p_ref.dtype)


def _in_proj_kernel(cast_w, x_ref, shift_ref, scale_ref, w_ref, ws_ref, p_ref, small_ref, u_ref,
                    *wout):
    @pl.when(pl.program_id(1) == 0)
    def _():
        x = x_ref[...]
        mu = jnp.mean(x, axis=-1, keepdims=True)
        xc = x - mu
        var = jnp.mean(xc * xc, axis=-1, keepdims=True)
        u = xc * lax.rsqrt(var + LN_EPS) * (1.0 + scale_ref[0]) + shift_ref[0]
        ub = u.astype(BF16)
        u_ref[...] = ub
        small_ref[...] = _dot_nt(ub, ws_ref[...].astype(BF16))

    _proj_tile(cast_w, u_ref, w_ref, p_ref, wout)


def _proj_kernel(cast_w, u_ref, w_ref, ws_ref, p_ref, small_ref, *wout):
    @pl.when(pl.program_id(1) == 0)
    def _():
        small_ref[...] = _dot_nt(u_ref[...], ws_ref[...].astype(BF16))

    _proj_tile(cast_w, u_ref, w_ref, p_ref, wout)


def _weight_specs(w, starts, layer, rows, tm, tn, d):
    if layer is None:
        return False, pl.BlockSpec((tn, d), lambda i, j: (j, 0)), [], []
    assert rows == tm

    def start(j):
        row = starts[0]
        for k in range(1, len(starts)):
            row = jnp.where(j >= k, starts[k], row)
        return pl.multiple_of(row, 32)

    spec = pl.BlockSpec((pl.Element(1), pl.Element(tn), pl.Element(d)),
                        lambda i, j: (layer, start(j), 0))
    return (True, spec, [pl.BlockSpec((tn, d), lambda i, j: (j, 0))],
            [jax.ShapeDtypeStruct((len(starts) * tn, d), BF16)])


def _in_proj_call(x2d, shift, scale, w, starts, w_small, mod_index, tm, layer=None):
    rows, d = x2d.shape
    tn = CW
    n_main = len(starts) * tn
    cast_w, w_spec, wout_specs, wout_shapes = _weight_specs(w, starts, layer, rows, tm, tn, d)
    return pl.pallas_call(
        functools.partial(_in_proj_kernel, cast_w),
        grid=(rows // tm, n_main // tn),
        in_specs=[pl.BlockSpec((tm, d), lambda i, j: (i, 0)),
                  pl.BlockSpec((1, 1, d), lambda i, j: (mod_index(i), 0, 0)),
                  pl.BlockSpec((1, 1, d), lambda i, j: (mod_index(i), 0, 0)),
                  w_spec,
                  pl.BlockSpec((N_SMALL, d), lambda i, j: (0, 0))],
        out_specs=[pl.BlockSpec((tm, tn), lambda i, j: (i, j)),
                   pl.BlockSpec((tm, N_SMALL), lambda i, j: (i, 0)),
                   pl.BlockSpec((tm, d), lambda i, j: (i, 0))] + wout_specs,
        out_shape=[jax.ShapeDtypeStruct((rows, n_main), PROJ_DTYPE),
                   jax.ShapeDtypeStruct((rows, N_SMALL), F32),
                   jax.ShapeDtypeStruct((rows, d), BF16)] + wout_shapes,
        compiler_params=_cparams(("arbitrary", "arbitrary")),
        name="in_proj",
    )(x2d, shift, scale, w, w_small)


def _proj_call(u2d, w, starts, w_small, tm, layer=None):
    rows, d = u2d.shape
    tn = CW
    n_main = len(starts) * tn
    cast_w, w_spec, wout_specs, wout_shapes = _weight_specs(w, starts, layer, rows, tm, tn, d)
    return pl.pallas_call(
        functools.partial(_proj_kernel, cast_w),
        grid=(rows // tm, n_main // tn),
        in_specs=[pl.BlockSpec((tm, d), lambda i, j: (i, 0)),
                  w_spec,
                  pl.BlockSpec((N_SMALL, d), lambda i, j: (0, 0))],
        out_specs=[pl.BlockSpec((tm, tn), lambda i, j: (i, j)),
                   pl.BlockSpec((tm, N_SMALL), lambda i, j: (i, 0))] + wout_specs,
        out_shape=[jax.ShapeDtypeStruct((rows, n_main), PROJ_DTYPE),
                   jax.ShapeDtypeStruct((rows, N_SMALL), F32)] + wout_shapes,
        compiler_params=_cparams(("arbitrary", "arbitrary")),
        name="ssd_proj",
    )(u2d, w, w_small)


def _out_proj_kernel(ya_ref, yb_ref, x_ref, gate_ref, wa_ref, wb_ref, g_ref, b_ref, o_ref):
    acc = _dot(ya_ref[...], wa_ref[0]) + _dot(yb_ref[...], wb_ref[0])
    r = DEEPNORM_ALPHA * x_ref[...] + gate_ref[0] * acc
    mu = jnp.mean(r, axis=-1, keepdims=True)
    rc = r - mu
    var = jnp.mean(rc * rc, axis=-1, keepdims=True)
    o_ref[...] = rc * lax.rsqrt(var + LN_EPS) * g_ref[...] + b_ref[...]


def _out_proj_call(ya, yb, x2d, gate, w_out3, layer, ln_g, ln_b, mod_index, tm):
    rows, d = x2d.shape
    const = lambda i: (0, 0)
    return pl.pallas_call(
        _out_proj_kernel,
        grid=(rows // tm,),
        in_specs=[pl.BlockSpec((tm, W_A), lambda i: (i, 0)),
                  pl.BlockSpec((tm, W_B), lambda i: (i, 0)),
                  pl.BlockSpec((tm, d), lambda i: (i, 0)),
                  pl.BlockSpec((1, 1, d), lambda i: (mod_index(i), 0, 0)),
                  pl.BlockSpec((1, W_A, d), lambda i: (layer, 0, 0)),
                  pl.BlockSpec((1, W_B, d), lambda i: (layer, 1, 0)),
                  pl.BlockSpec((1, d), const),
                  pl.BlockSpec((1, d), const)],
        out_specs=pl.BlockSpec((tm, d), lambda i: (i, 0)),
        out_shape=jax.ShapeDtypeStruct((rows, d), F32),
        compiler_params=_cparams(("arbitrary",)),
        name="out_proj",
    )(ya, yb, x2d, gate, w_out3, w_out3, ln_g.reshape(1, d), ln_b.reshape(1, d))


def _conv_silu(x, prev_row, next_row, w, bias):
    rows = x.shape[0]
    row = lax.broadcasted_iota(jnp.int32, x.shape, 0)
    xm = jnp.where(row == 0, prev_row, pltpu.roll(x, 1, 0))
    xp = jnp.where(row == rows - 1, next_row, pltpu.roll(x, rows - 1, 0))
    y = w[0:1] * xm + w[1:2] * x + w[2:3] * xp + bias
    return _silu(y)


def _conv_block(x_ref, p_ref, n_ref, sc, cs, has_prev, has_next, w, bias):
    r0 = sc * CHUNK
    if sc > 0:
        pr = x_ref[0, r0 - 1:r0, cs].astype(F32)
    else:
        pr = jnp.where(has_prev, p_ref[0, HALO - 1:HALO, cs].astype(F32), 0.0)
    if sc < SUB - 1:
        nx = x_ref[0, r0 + CHUNK:r0 + CHUNK + 1, cs].astype(F32)
    else:
        nx = jnp.where(has_next, n_ref[0, 0:1, cs].astype(F32), 0.0)
    return _conv_silu(x_ref[0, r0:r0 + CHUNK, cs].astype(F32), pr, nx, w, bias)


def _tri_masks(reverse):
    t = lax.broadcasted_iota(jnp.int32, (CHUNK, CHUNK), 0)
    s = lax.broadcasted_iota(jnp.int32, (CHUNK, CHUNK), 1)
    mask = (s >= t) if reverse else (s <= t)
    mask_t = (t >= s) if reverse else (t <= s)
    return mask, mask.astype(F32), mask_t.astype(F32)


def _chunk_pos(reverse):
    i = pl.program_id(1)
    nc = pl.num_programs(1)
    j = (nc - 1 - i) if reverse else i
    return i, j, nc


def _chunk_specs(t_len, reverse):
    nc = t_len // BLK
    nhb = t_len // HALO
    hpc = BLK // HALO

    def jj(i):
        return (nc - 1 - i) if reverse else i

    def main(col, width):
        return pl.BlockSpec((1, BLK, width), lambda b, i: (b, jj(i), (col * CW) // width))

    def prev(col):
        return pl.BlockSpec((1, HALO, CW), lambda b, i: (b, jnp.maximum(jj(i) * hpc - 1, 0), col))

    def nxt(col):
        return pl.BlockSpec((1, HALO, CW),
                            lambda b, i: (b, jnp.minimum((jj(i) + 1) * hpc, nhb - 1), col))

    def const2(shape):
        return pl.BlockSpec(shape, lambda b, i: (0, 0))

    return nc, main, prev, nxt, const2


def _mlstm_kernel(reverse, finalize, *refs):
    if finalize:
        (qk_ref, v_ref, gs_ref, gb_ref, c0_ref, m0_ref, hb_ref, o_ref, z_ref, mhw_ref,
         out_ref, cf_ref, mf_ref) = refs
    else:
        (q_ref, qp_ref, qn_ref, k_ref, kp_ref, kn_ref, v_ref, gs_ref, cw_ref, cb_ref, gb_ref,
         c0_ref, m0_ref, out_ref, qk_ref, cf_ref, mf_ref) = refs

    i, j, nc = _chunk_pos(reverse)

    @pl.when(i == 0)
    def _():
        cf_ref[...] = c0_ref[...]
        mf_ref[...] = m0_ref[...]

    has_prev = j > 0
    has_next = j < nc - 1
    for sc in (reversed(range(SUB)) if reverse else range(SUB)):
        _mlstm_chunk(reverse, finalize, sc, has_prev, has_next, refs)


def _mlstm_chunk(reverse, finalize, sc, has_prev, has_next, refs):
    if finalize:
        (qk_ref, v_ref, gs_ref, gb_ref, c0_ref, m0_ref, hb_ref, o_ref, z_ref, mhw_ref,
         out_ref, cf_ref, mf_ref) = refs
    else:
        (q_ref, qp_ref, qn_ref, k_ref, kp_ref, kn_ref, v_ref, gs_ref, cw_ref, cb_ref, gb_ref,
         c0_ref, m0_ref, out_ref, qk_ref, cf_ref, mf_ref) = refs
    rows = slice(sc * CHUNK, (sc + 1) * CHUNK)
    mask, tri, tri_t = _tri_masks(reverse)
    edge = 0 if reverse else CHUNK - 1

    gb = gs_ref[0, rows, :] + gb_ref[...]
    gbt = gb.T
    io = 2 * NH_A if reverse else 0
    b_all = _cumsum_rows(tri, _log_sigmoid(gb))
    li_c = gb[:, io:io + NH_A]
    b_c = b_all[:, io + NH_A:io + 2 * NH_A]
    b_r = jnp.dot(_log_sigmoid(gbt[io + NH_A:io + 2 * NH_A, :]), tri_t,
                  preferred_element_type=F32, precision=lax.Precision.HIGHEST)
    c_r = gbt[io:io + NH_A, :] - b_r

    m_prev = mf_ref[0][:, 0:NH_A]
    g_c = b_c[edge:edge + 1, :]
    a_c = g_c - b_c + li_c
    m_new = jnp.maximum(g_c + m_prev, jnp.max(a_c, axis=0, keepdims=True))
    w_c = jnp.exp(a_c - m_new)
    dec = jnp.exp(g_c + m_prev - m_new)

    heads = range(NH_A)
    ksl = [slice(h * DQK_A, (h + 1) * DQK_A) for h in heads]
    vsl = [slice(h * DV_A, (h + 1) * DV_A) for h in heads]
    kksl = [slice(QK_A + h * DQK_A, QK_A + (h + 1) * DQK_A) for h in heads]

    if finalize:
        qb = [qk_ref[0, rows, ksl[h]] for h in heads]
        kb = [qk_ref[0, rows, kksl[h]] for h in heads]
        qh = [t.astype(F32) for t in qb]
        kh = [t.astype(F32) for t in kb]
        gate = []
        for h in heads:
            o = o_ref[0, rows, vsl[h]].astype(F32)
            z = z_ref[0, rows, vsl[h]].astype(F32)
            gate.append(z / ((1.0 + jnp.exp(-o)) * (1.0 + jnp.exp(-z))))
    else:
        qh = [_conv_block(q_ref, qp_ref, qn_ref, sc, ksl[h], has_prev, has_next,
                          cw_ref[:, ksl[h]], cb_ref[:, ksl[h]]) * (DQK_A ** -0.5) for h in heads]
        kh = [_conv_block(k_ref, kp_ref, kn_ref, sc, ksl[h], has_prev, has_next,
                          cw_ref[:, kksl[h]], cb_ref[:, kksl[h]]) for h in heads]
        qb = [t.astype(BF16) for t in qh]
        kb = [t.astype(BF16) for t in kh]
        for h in heads:
            qk_ref[0, rows, ksl[h]] = qb[h]
            qk_ref[0, rows, kksl[h]] = kb[h]
    ones_l = jnp.ones((CHUNK, CHUNK), BF16)
    vb = [jnp.concatenate([v_ref[0, rows, vsl[h]].astype(BF16), ones_l], axis=1) for h in heads]
    c_prev = [cf_ref[0, h] for h in heads]
    mp = [m_prev[:, h:h + 1] for h in heads]

    s_qk = [_dot_nt(qb[h], kb[h]) for h in heads]
    q_c = [_dot(qb[h], c_prev[h].astype(BF16)) for h in heads]
    cm = [jnp.where(mask, c_r[h:h + 1, :], -jnp.inf) for h in heads]
    mm = [jnp.broadcast_to(jnp.maximum(jnp.max(cm[h], axis=1, keepdims=True), mp[h]),
                           (CHUNK, CHUNK)) for h in heads]

    kw = [kh[h] * w_c[:, h:h + 1] for h in heads]
    for h in heads:
        cf_ref[0, h] = dec[:, h:h + 1] * c_prev[h] + _dot_tn(kw[h].astype(BF16), vb[h])
    mf_ref[0, :, 0:NH_A] = m_new

    p = [s_qk[h] * jnp.exp(cm[h] - mm[h]) for h in heads]
    p_v = [_dot(p[h].astype(BF16), vb[h]) for h in heads]
    w_int = [jnp.exp(mp[h] - mm[h]) for h in heads]
    den = [w_int[h] * q_c[h][:, DV_A:] + p_v[h][:, DV_A:] for h in heads]
    b_t = [jnp.broadcast_to(b_c[:, h:h + 1], (CHUNK, CHUNK)) for h in heads]
    inv = [1.0 / jnp.maximum(jnp.abs(den[h]), jnp.exp(-(b_t[h] + mm[h]))) for h in heads]
    hval = [jnp.concatenate(
        [(w_int[h] * q_c[h][:, c * CHUNK:(c + 1) * CHUNK] + p_v[h][:, c * CHUNK:(c + 1) * CHUNK])
         * inv[h] for c in range(DV_A // CHUNK)], axis=1) for h in heads]

    if not finalize:
        for h in heads:
            out_ref[0, rows, vsl[h]] = hval[h].astype(out_ref.dtype)
        return

    hs = [hval[h] + hb_ref[0, rows, vsl[h]].astype(F32) for h in heads]
    mu = [jnp.mean(hs[h], axis=-1, keepdims=True) for h in heads]
    hc = [hs[h] - mu[h] for h in heads]
    var = [jnp.mean(hc[h] * hc[h], axis=-1, keepdims=True) for h in heads]
    for h in heads:
        hn = hc[h] * lax.rsqrt(var[h] + LN_EPS) * mhw_ref[:, vsl[h]]
        out_ref[0, rows, vsl[h]] = (hn * gate[h]).astype(out_ref.dtype)


def _mlstm_call(reverse, finalize, state, *, p3, small3, gate_b128, conv_w=None, conv_b=None,
                qk=None, hb=None, mh_w=None):
    bsz, t_len, _ = p3.shape
    nc, main, prev, nxt, const2 = _chunk_specs(t_len, reverse)
    c0, m0 = state
    state_specs = [pl.BlockSpec((1, NH_A, DQK_A, DV_A + CHUNK), lambda b, i: (b, 0, 0, 0)),
                   pl.BlockSpec((1, 1, 128), lambda b, i: (b, 0, 0))]
    state_shapes = [jax.ShapeDtypeStruct(c0.shape, F32), jax.ShapeDtypeStruct(m0.shape, F32)]
    wide = main(0, W_A)
    if finalize:
        in_specs = [wide, main(COL_V, W_A), main(0, N_SMALL), const2((1, N_SMALL))] + state_specs
        in_specs += [wide, main(COL_O, W_A), main(COL_ZA, W_A), const2((1, W_A))]
        args = [qk, p3, small3, gate_b128, c0, m0, hb, p3, p3, mh_w]
        out_specs = [wide] + state_specs
        out_shape = [jax.ShapeDtypeStruct((bsz, t_len, W_A), BF16)] + state_shapes
    else:
        in_specs = [main(COL_Q, CW), prev(COL_Q), nxt(COL_Q),
                    main(COL_K, CW), prev(COL_K), nxt(COL_K),
                    main(COL_V, W_A), main(0, N_SMALL),
                    const2((3, 2 * QK_A)), const2((1, 2 * QK_A)), const2((1, N_SMALL))] + state_specs
        args = [p3, p3, p3, p3, p3, p3, p3, small3, conv_w, conv_b, gate_b128, c0, m0]
        out_specs = [wide, wide] + state_specs
        out_shape = [jax.ShapeDtypeStruct((bsz, t_len, W_A), HB_DTYPE),
                     jax.ShapeDtypeStruct((bsz, t_len, 2 * QK_A), BF16)] + state_shapes
    res = pl.pallas_call(
        functools.partial(_mlstm_kernel, reverse, finalize),
        grid=(bsz, nc), in_specs=in_specs, out_specs=out_specs, out_shape=out_shape,
        compiler_params=_cparams(("arbitrary", "arbitrary")),
        name="mlstm_" + ("bwd" if reverse else "fwd"),
    )(*args)
    if finalize:
        return res[0], tuple(res[1:])
    return res[0], res[1], tuple(res[2:])


def _ssd_kernel(reverse, finalize, *refs):
    if finalize:
        (xc_ref, bcc_ref, gs_ref, dtb_ref, alog_ref, s0_ref, yb_ref, z0_ref, z1_ref, dsk_ref,
         nw_ref, out_ref, sf_ref, y_s) = refs
    else:
        (x0_ref, x0p_ref, x0n_ref, x1_ref, x1p_ref, x1n_ref, bc_ref, bcp_ref, bcn_ref,
         gs_ref, cw_ref, cb_ref, dtb_ref, alog_ref, s0_ref, out_ref, xc_ref, sf_ref) = refs

    i, j, nc = _chunk_pos(reverse)

    @pl.when(i == 0)
    def _():
        sf_ref[...] = s0_ref[...]

    has_prev = j > 0
    has_next = j < nc - 1
    for sc in (reversed(range(SUB)) if reverse else range(SUB)):
        _ssd_chunk(reverse, finalize, sc, has_prev, has_next, refs)


def _ssd_chunk(reverse, finalize, sc, has_prev, has_next, refs):
    if finalize:
        (xc_ref, bcc_ref, gs_ref, dtb_ref, alog_ref, s0_ref, yb_ref, z0_ref, z1_ref, dsk_ref,
         nw_ref, out_ref, sf_ref, y_s) = refs
    else:
        (x0_ref, x0p_ref, x0n_ref, x1_ref, x1p_ref, x1n_ref, bc_ref, bcp_ref, bcn_ref,
         gs_ref, cw_ref, cb_ref, dtb_ref, alog_ref, s0_ref, out_ref, xc_ref, sf_ref) = refs
    rows = slice(sc * CHUNK, (sc + 1) * CHUNK)
    _, tri, _ = _tri_masks(reverse)
    edge = 0 if reverse else CHUNK - 1

    t_i = lax.broadcasted_iota(jnp.int32, (CHUNK, CHUNK), 0)
    l_i = lax.broadcasted_iota(jnp.int32, (CHUNK, CHUNK), 1)
    s_lo = jnp.where(l_i < HALF, l_i, l_i - HALF)
    lo_half = l_i < HALF
    if reverse:
        mask1, mask2 = s_lo >= t_i, s_lo + HALF >= t_i
    else:
        mask1, mask2 = s_lo <= t_i, s_lo + HALF <= t_i
    k_i = lax.broadcasted_iota(jnp.int32, (2 * CHUNK, CHUNK), 0)
    kl_i = lax.broadcasted_iota(jnp.int32, (2 * CHUNK, CHUNK), 1)
    blockdiag = ((k_i // HALF) % 2 == 1) == (kl_i >= HALF)

    dt_all = _softplus(gs_ref[0, rows, :] + dtb_ref[...])
    da_all = dt_all * (-LOG2E * jnp.exp(alog_ref[...]))
    acum_all = _cumsum_rows(tri, da_all)
    do = H_B if reverse else 0
    acum_t = acum_all.T
    a_even = acum_t[do:do + NPAIR, :]
    a_odd = acum_t[do + NPAIR:do + H_B, :]
    lane16 = lax.broadcasted_iota(jnp.int32, (NPAIR, CHUNK), 1) < HALF
    rv1 = jnp.where(lane16, a_even, pltpu.roll(a_odd, HALF, 1))
    rv2 = jnp.where(lane16, pltpu.roll(a_even, HALF, 1), a_odd)

    if finalize:
        bmat = bcc_ref[0, rows, 0:GP_B]
        cmat = bcc_ref[0, rows, GP_B:2 * GP_B]
        ssq = jnp.zeros((CHUNK, CHUNK), F32)
    else:
        bmat = _conv_block(bc_ref, bcp_ref, bcn_ref, sc, slice(0, GP_B), has_prev, has_next,
                           cw_ref[:, W_B:W_B + GP_B], cb_ref[:, W_B:W_B + GP_B]).astype(BF16)
        cmat = _conv_block(bc_ref, bcp_ref, bcn_ref, sc, slice(GP_B, 2 * GP_B), has_prev, has_next,
                           cw_ref[:, W_B + GP_B:W_B + 2 * GP_B],
                           cb_ref[:, W_B + GP_B:W_B + 2 * GP_B]).astype(BF16)
        xc_ref[0, rows, W_B:W_B + GP_B] = bmat
        xc_ref[0, rows, W_B + GP_B:W_B + 2 * GP_B] = cmat

    for g in range(G_B):
        gs = slice(g * GP_B, (g + 1) * GP_B)
        if finalize:
            xg = xc_ref[0, rows, gs].astype(F32)
        else:
            xr, xpr, xnr = (x0_ref, x0p_ref, x0n_ref) if g < 2 else (x1_ref, x1p_ref, x1n_ref)
            cs = slice((g % 2) * GP_B, (g % 2 + 1) * GP_B)
            xg = _conv_block(xr, xpr, xnr, sc, cs, has_prev, has_next, cw_ref[:, gs], cb_ref[:, gs])
            xc_ref[0, rows, gs] = xg.astype(BF16)
        bg = bmat[:, g * N_B:(g + 1) * N_B]
        cg = cmat[:, g * N_B:(g + 1) * N_B]
        cb1 = _dot_nt(cg, jnp.concatenate([bg[0:HALF], bg[0:HALF]], axis=0))
        cb2 = _dot_nt(cg, jnp.concatenate([bg[HALF:], bg[HALF:]], axis=0))
        s_prev = sf_ref[0, g]
        cs_prev = _dot(cg, s_prev.astype(BF16))
        y_parts, xw_parts, tot_parts = [], [], []
        for pp in range(R_B // 2):
            pr = g * (R_B // 2) + pp
            ps = slice(pp * CHUNK, (pp + 1) * CHUNK)
            idx = jnp.where(lo_half, do + pr, do + NPAIR + pr)
            acum_b = jnp.take_along_axis(acum_all, idx, axis=1)
            dt_b = jnp.take_along_axis(dt_all, idx, axis=1)
            xdt = xg[:, ps] * dt_b
            xdtb = xdt.astype(BF16)
            w1 = cb1 * jnp.exp2(jnp.where(mask1, acum_b - rv1[pr:pr + 1, :], -jnp.inf))
            w2 = cb2 * jnp.exp2(jnp.where(mask2, acum_b - rv2[pr:pr + 1, :], -jnp.inf))
            w12 = jnp.concatenate([w1.astype(BF16), w2.astype(BF16)], axis=1)
            x12 = jnp.concatenate([xdtb[0:HALF], xdtb[0:HALF], xdtb[HALF:], xdtb[HALF:]], axis=0)
            x12 = jnp.where(blockdiag, x12, jnp.zeros_like(x12))
            tot = acum_b[edge:edge + 1, :]
            y_parts.append(_dot(w12, x12) + jnp.exp2(acum_b) * cs_prev[:, ps])
            xw_parts.append((xdt * jnp.exp2(tot - acum_b)).astype(BF16))
            tot_parts.append(tot)
        xw = jnp.concatenate(xw_parts, axis=1)
        etot = jnp.exp2(jnp.concatenate(tot_parts, axis=1))
        sf_ref[0, g] = etot * s_prev + _dot_tn(bg, xw)
        y_grp = jnp.concatenate(y_parts, axis=1)
        if finalize:
            z_ref = z0_ref if g < 2 else z1_ref
            zs = slice((g % 2) * GP_B, (g % 2 + 1) * GP_B)
            y = y_grp + yb_ref[0, rows, gs].astype(F32) + dsk_ref[:, gs] * xg
            y = y * _silu(z_ref[0, rows, zs].astype(F32))
            y_s[:, gs] = y
            y2 = y * y
            ssq = ssq + (y2[:, 0:CHUNK] + y2[:, CHUNK:2 * CHUNK]
                         + y2[:, 2 * CHUNK:3 * CHUNK] + y2[:, 3 * CHUNK:])
        else:
            out_ref[0, rows, gs] = y_grp.astype(out_ref.dtype)

    if finalize:
        inv = lax.rsqrt(jnp.sum(ssq, axis=-1, keepdims=True) * (1.0 / W_B) + LN_EPS)
        for half in range(2):
            cs = slice(half * CW, (half + 1) * CW)
            out_ref[0, rows, cs] = (y_s[:, cs] * inv * nw_ref[:, cs]).astype(out_ref.dtype)


def _ssd_call(reverse, finalize, s0, *, small3, dtb128, alog128, p3, conv_w=None, conv_b=None,
              xc=None, yb=None, d_skip=None, norm_w=None):
    bsz, t_len, _ = p3.shape
    nc, main, prev, nxt, const2 = _chunk_specs(t_len, reverse)
    state_spec = pl.BlockSpec((1, G_B, N_B, GP_B), lambda b, i: (b, 0, 0, 0))
    state_shape = jax.ShapeDtypeStruct(s0.shape, F32)
    xc_spec = main(0, W_B + 2 * GP_B)
    if finalize:
        in_specs = [main(0, W_B), main(2, CW), main(0, N_SMALL), const2((1, N_SMALL)),
                    const2((1, N_SMALL)), state_spec,
                    main(0, W_B), main(COL_ZB, CW), main(COL_ZB + 1, CW),
                    const2((1, W_B)), const2((1, W_B))]
        args = [xc, xc, small3, dtb128, alog128, s0, yb, p3, p3, d_skip, norm_w]
        out_specs = [main(0, W_B), state_spec]
        out_shape = [jax.ShapeDtypeStruct((bsz, t_len, W_B), BF16), state_shape]
        scratch = [pltpu.VMEM((CHUNK, W_B), F32)]
    else:
        in_specs = [main(COL_X, CW), prev(COL_X), nxt(COL_X),
                    main(COL_X + 1, CW), prev(COL_X + 1), nxt(COL_X + 1),
                    main(COL_BC, CW), prev(COL_BC), nxt(COL_BC),
                    main(0, N_SMALL),
                    const2((3, 3 * CW)), const2((1, 3 * CW)), const2((1, N_SMALL)),
                    const2((1, N_SMALL)), state_spec]
        args = [p3] * 9 + [small3, conv_w, conv_b, dtb128, alog128, s0]
        out_specs = [main(0, W_B), xc_spec, state_spec]
        out_shape = [jax.ShapeDtypeStruct((bsz, t_len, W_B), HB_DTYPE),
                     jax.ShapeDtypeStruct((bsz, t_len, W_B + 2 * GP_B), BF16), state_shape]
        scratch = []
    return pl.pallas_call(
        functools.partial(_ssd_kernel, reverse, finalize),
        grid=(bsz, nc), in_specs=in_specs, out_specs=out_specs, out_shape=out_shape,
        scratch_shapes=scratch,
        compiler_params=_cparams(("arbitrary", "arbitrary")),
        name="ssd_" + ("bwd" if reverse else "fwd"),
    )(*args)


def _to_colmajor(t, bsz):
    ch = t.shape[-1]
    return t.reshape(bsz, -1, GRID_W, ch).transpose(0, 2, 1, 3).reshape(-1, ch)


def _from_colmajor(t, bsz):
    ch = t.shape[-1]
    return t.reshape(bsz, GRID_W, -1, ch).transpose(0, 2, 1, 3).reshape(-1, ch)


def _mixer_scans(pa, sa, pb, sb, bsz, t_len, lp, states):
    st_mf, st_mb, st_sf, st_sb = states
    pa = pa.reshape(bsz, t_len, N_MAIN_A)
    sa = sa.reshape(bsz, t_len, N_SMALL)
    h_b, qk, st_mb = _mlstm_call(True, False, st_mb, p3=pa, small3=sa, gate_b128=lp["gate_b128"],
                                 conv_w=lp["conv_qk_w"], conv_b=lp["conv_qk_b"])
    y_a, st_mf = _mlstm_call(False, True, st_mf, p3=pa, small3=sa, gate_b128=lp["gate_b128"],
                             qk=qk, hb=h_b, mh_w=lp["mh_w"])

    pb = pb.reshape(bsz, t_len, N_MAIN_B)
    sb = sb.reshape(bsz, t_len, N_SMALL)
    y_bb, xc, st_sb = _ssd_call(True, False, st_sb, small3=sb, dtb128=lp["dtb128"],
                                alog128=lp["alog128"], p3=pb, conv_w=lp["conv_xbc_w"],
                                conv_b=lp["conv_xbc_b"])
    y_b, st_sf = _ssd_call(False, True, st_sf, small3=sb, dtb128=lp["dtb128"],
                           alog128=lp["alog128"], p3=pb, xc=xc, yb=y_bb, d_skip=lp["d_skip"],
                           norm_w=lp["ssm_w"])
    return y_a, y_b, (st_mf, st_mb, st_sf, st_sb)


def _mixer(x2d, bsz, t_len, latent, shift, scale, mod_index, tm, w_a, w_b, lp, states,
           cast_layer=None):
    res_a = _in_proj_call(x2d, shift, scale, w_a, A_STARTS, lp["w_small_a"], mod_index, tm,
                          cast_layer)
    pa, sa, u2d = res_a[:3]
    us2d = _to_colmajor(u2d, bsz) if latent else u2d
    res_b = _proj_call(us2d, w_b, B_STARTS, lp["w_small_b"], tm, cast_layer)
    pb, sb = res_b[:2]
    if cast_layer is not None:
        w_a, w_b = res_a[3], res_b[2]
    y_a, y_b, states = _mixer_scans(pa, sa, pb, sb, bsz, t_len, lp, states)
    y_b = y_b.reshape(bsz * t_len, W_B)
    if latent:
        y_b = _from_colmajor(y_b, bsz)
    return y_a.reshape(bsz * t_len, W_A), y_b, states, w_a, w_b


def _zero_states(bsz):
    m_state = (jnp.zeros((bsz, NH_A, DQK_A, DV_A + CHUNK), F32), jnp.zeros((bsz, 1, 128), F32))
    s_state = jnp.zeros((bsz, G_B, N_B, GP_B), F32)
    return (m_state, m_state, s_state, s_state)


def _layer_params(l, w_t, conv_qk_w, conv_qk_b, gate_b, mh_norm_w, conv_xbc_w, conv_xbc_b,
                  dt_bias, a_log, d_skip, ssm_norm_w):
    feats = lambda a, n: lax.slice(w_t, (l, a, 0), (l + 1, a + n, w_t.shape[2]))[0]

    def pad_cols(a):
        return jnp.concatenate([a, jnp.zeros((a.shape[0], N_SMALL - a.shape[1]), a.dtype)], axis=1)

    def pad_rows(a):
        return jnp.concatenate([a, jnp.zeros((N_SMALL - a.shape[0], a.shape[1]), a.dtype)], axis=0)

    def even_odd(a):
        a4 = a.reshape(a.shape[0], 2, NPAIR, 2)
        return a4.transpose(0, 1, 3, 2).reshape(a.shape[0], 2 * H_B)

    return {
        "w_small_a": pad_rows(feats(O_GATE, 4 * NH_A)),
        "w_small_b": pad_rows(even_odd(feats(O_DT, 2 * H_B).T).T),
        "conv_qk_w": conv_qk_w[l], "conv_qk_b": conv_qk_b[l].reshape(1, -1),
        "gate_b128": pad_cols(gate_b[l].reshape(1, -1)),
        "mh_w": mh_norm_w[l].reshape(1, W_A),
        "conv_xbc_w": conv_xbc_w[l], "conv_xbc_b": conv_xbc_b[l].reshape(1, -1),
        "dtb128": pad_cols(even_odd(dt_bias[l].reshape(1, -1))),
        "alog128": pad_cols(even_odd(a_log[l].reshape(1, -1))),
        "d_skip": jnp.repeat(d_skip[l], P_B).reshape(1, W_B),
        "ssm_w": ssm_norm_w[l].reshape(1, W_B),
    }


def kernel(x, c, ctx, c_ctx, w_ada, b_ada, w_in, conv_qk_w, conv_qk_b, gate_b, mh_norm_w,
           conv_xbc_w, conv_xbc_b, dt_bias, a_log, d_skip, ssm_norm_w, w_out, ln_g, ln_b):
    bsz, t_len, d = x.shape
    ctx_len = ctx.shape[1]
    depth = w_in.shape[0]
    assert d == D_MODEL and depth == DEPTH and bsz + 1 <= 8
    assert t_len % (CHUNK * GRID_W) == 0 and ctx_len % CHUNK == 0

    cvec = jnp.concatenate([c, c_ctx[None, :]], axis=0)
    mod = _ada_call(jnp.broadcast_to(cvec[:, :, None], (bsz + 1, d, 128)), w_ada, b_ada)

    w_t = jnp.swapaxes(w_in, 1, 2)
    w_out_bf = _cast_bf16(w_out, 512)

    x2d = x.reshape(bsz * t_len, d)
    xc2d = ctx.reshape(bsz * ctx_len, d)
    tm_in, tm_out, tm_ctx_in, tm_ctx_out = 1024, 512, bsz * ctx_len, 256
    lat_in = lambda i: i // (t_len // tm_in)
    lat_out = lambda i: i // (t_len // tm_out)
    ctx_index = lambda i: bsz

    for l in range(depth):
        last = l == depth - 1
        lp = _layer_params(l, w_t, conv_qk_w, conv_qk_b, gate_b, mh_norm_w, conv_xbc_w,
                           conv_xbc_b, dt_bias, a_log, d_skip, ssm_norm_w)
        shift = mod[l, :, 0:d].reshape(8, 1, d)
        scale = mod[l, :, d:2 * d].reshape(8, 1, d)
        gate = mod[l, :, 2 * d:3 * d].reshape(8, 1, d)

        yc_a, yc_b, ctx_states, w_a, w_b = _mixer(xc2d, bsz, ctx_len, False, shift, scale,
                                                  ctx_index, tm_ctx_in, w_t, w_t, lp,
                                                  _zero_states(bsz), cast_layer=l)
        y_a, y_b, _, _, _ = _mixer(x2d, bsz, t_len, True, shift, scale, lat_in, tm_in, w_a, w_b,
                                   lp, ctx_states)
        x2d = _out_proj_call(y_a, y_b, x2d, gate, w_out_bf, l, ln_g[l], ln_b[l], lat_out, tm_out)
        if not last:
            xc2d = _out_proj_call(yc_a, yc_b, xc2d, gate, w_out_bf, l, ln_g[l], ln_b[l],
                                  ctx_index, tm_ctx_out)
    return x2d.reshape(bsz, t_len, d)
```

```python
import functools

import jax
import jax.numpy as jnp
from jax import lax
from jax.experimental import pallas as pl
from jax.experimental.pallas import tpu as pltpu

F32 = jnp.float32
BF16 = jnp.bfloat16

D_MODEL = 2048
DEPTH = 2
GRID_W = 64
NH_A = 8
DV_A = 256
DQK_A = 128
W_A = NH_A * DV_A
QK_A = NH_A * DQK_A
P_B = 64
H_B = 32
G_B = 4
R_B = H_B // G_B
N_B = 128
W_B = H_B * P_B
GP_B = R_B * P_B
NPAIR = H_B // 2
CHUNK = 128
HALF = CHUNK // 2
SUB_MLSTM = 1
SUB_SSD = 2
LN_EPS = 1e-5
DEEPNORM_ALPHA = (2 * DEPTH) ** 0.25
LOG2E = 1.4426950408889634

CW = 1024
COL_Q, COL_K, COL_V, COL_O, COL_ZA = 0, 1, 2, 4, 6
N_MAIN_A = 8 * CW
COL_X, COL_BC, COL_ZB = 0, 2, 3
N_MAIN_B = 5 * CW
N_SMALL = 128
O_GATE = N_MAIN_A
O_XBC = O_GATE + 4 * NH_A
O_DT = O_XBC + W_B + 2 * G_B * N_B
O_ZB = O_DT + 2 * H_B
A_STARTS = [k * CW for k in range(N_MAIN_A // CW)]
B_STARTS = ([O_XBC + k * CW for k in range((W_B + 2 * G_B * N_B) // CW)]
            + [O_ZB + k * CW for k in range(W_B // CW)])

HALO = 16
VMEM_LIMIT = 56 * 1024 * 1024

PROJ_DTYPE = BF16
HB_DTYPE = BF16


def _cparams(sem):
    return pltpu.CompilerParams(dimension_semantics=sem, vmem_limit_bytes=VMEM_LIMIT)


def _softplus(x):
    return jnp.maximum(x, 0.0) + jnp.log1p(jnp.exp(-jnp.abs(x)))


def _log_sigmoid(x):
    return jnp.minimum(x, 0.0) - jnp.log1p(jnp.exp(-jnp.abs(x)))


def _silu(x):
    return x * jax.nn.sigmoid(x)


def _dot(a, b):
    return jnp.dot(a, b, preferred_element_type=F32)


def _dot_nt(a, b):
    return lax.dot_general(a, b, (((1,), (1,)), ((), ())), preferred_element_type=F32)


def _dot_tn(a, b):
    return lax.dot_general(a, b, (((0,), (0,)), ((), ())), preferred_element_type=F32)


def _cumsum_rows(tri, x):
    return jnp.dot(tri, x, preferred_element_type=F32, precision=lax.Precision.HIGHEST)


def _ada_kernel(c_ref, w_ref, b_ref, o_ref):
    n_rows, d, _ = c_ref.shape
    tn = w_ref.shape[2]
    o_ref[...] = jnp.zeros(o_ref.shape, F32)
    for r in range(n_rows):
        s = _silu(c_ref[r]).reshape(d // 8, 8, 128)
        for cb in range(tn // 128):
            cs = slice(cb * 128, (cb + 1) * 128)
            acc = jnp.sum(w_ref[0, :, cs].reshape(d // 8, 8, 128) * s, axis=0)
            o_ref[0, r:r + 1, cs] = jnp.sum(acc, axis=0, keepdims=True) + b_ref[0, :, cs]


def _ada_call(crep, w_ada, b_ada):
    depth, d, n3 = w_ada.shape
    n_rows = crep.shape[0]
    tn = 1024
    return pl.pallas_call(
        _ada_kernel,
        grid=(depth, n3 // tn),
        in_specs=[pl.BlockSpec((n_rows, d, 128), lambda l, j: (0, 0, 0)),
                  pl.BlockSpec((1, d, tn), lambda l, j: (l, 0, j)),
                  pl.BlockSpec((1, 1, tn), lambda l, j: (l, 0, j))],
        out_specs=pl.BlockSpec((1, 8, tn), lambda l, j: (l, 0, j)),
        out_shape=jax.ShapeDtypeStruct((depth, 8, n3), F32),
        compiler_params=_cparams(("arbitrary", "arbitrary")),
        name="adaln_mod",
    )(crep, w_ada, b_ada.reshape(depth, 1, n3))


def _cast_kernel(x_ref, o_ref):
    o_ref[...] = x_ref[...].astype(o_ref.dtype)


def _cast_bf16(w3, tr):
    depth, rows, cols = w3.shape
    spec = pl.BlockSpec((1, tr, cols), lambda l, i: (l, i, 0))
    return pl.pallas_call(
        _cast_kernel, grid=(depth, pl.cdiv(rows, tr)), in_specs=[spec], out_specs=spec,
        out_shape=jax.ShapeDtypeStruct(w3.shape, BF16),
        compiler_params=_cparams(("arbitrary", "arbitrary")),
        name="cast_bf16",
    )(w3)


def _proj_tile(cast_w, u_ref, w_ref, p_ref, wout):
    if cast_w:
        w = w_ref[0].astype(BF16)
        wout[0][...] = w
    else:
        w = w_ref[...]
    p_ref[...] = _dot_nt(u_ref[...], w).astype(p_ref.dtype)


def _in_proj_kernel(cast_w, x_ref, shift_ref, scale_ref, w_ref, ws_ref, p_ref, small_ref, u_ref,
                    *wout):
    @pl.when(pl.program_id(1) == 0)
    def _():
        x = x_ref[...]
        mu = jnp.mean(x, axis=-1, keepdims=True)
        xc = x - mu
        var = jnp.mean(xc * xc, axis=-1, keepdims=True)
        u = xc * lax.rsqrt(var + LN_EPS) * (1.0 + scale_ref[0]) + shift_ref[0]
        ub = u.astype(BF16)
        u_ref[...] = ub
        small_ref[...] = _dot_nt(ub, ws_ref[...].astype(BF16))

    _proj_tile(cast_w, u_ref, w_ref, p_ref, wout)


def _proj_kernel(cast_w, u_ref, w_ref, ws_ref, p_ref, small_ref, *wout):
    @pl.when(pl.program_id(1) == 0)
    def _():
        small_ref[...] = _dot_nt(u_ref[...], ws_ref[...].astype(BF16))

    _proj_tile(cast_w, u_ref, w_ref, p_ref, wout)


def _weight_specs(w, starts, layer, rows, tm, tn, d):
    if layer is None:
        return False, pl.BlockSpec((tn, d), lambda i, j: (j, 0)), [], []
    assert rows == tm

    def start(j):
        row = starts[0]
        for k in range(1, len(starts)):
            row = jnp.where(j >= k, starts[k], row)
        return pl.multiple_of(row, 32)

    spec = pl.BlockSpec((pl.Element(1), pl.Element(tn), pl.Element(d)),
                        lambda i, j: (layer, start(j), 0))
    return (True, spec, [pl.BlockSpec((tn, d), lambda i, j: (j, 0))],
            [jax.ShapeDtypeStruct((len(starts) * tn, d), BF16)])


def _in_proj_call(x2d, shift, scale, w, starts, w_small, mod_index, tm, layer=None):
    rows, d = x2d.shape
    tn = CW
    n_main = len(starts) * tn
    cast_w, w_spec, wout_specs, wout_shapes = _weight_specs(w, starts, layer, rows, tm, tn, d)
    return pl.pallas_call(
        functools.partial(_in_proj_kernel, cast_w),
        grid=(rows // tm, n_main // tn),
        in_specs=[pl.BlockSpec((tm, d), lambda i, j: (i, 0)),
                  pl.BlockSpec((1, 1, d), lambda i, j: (mod_index(i), 0, 0)),
                  pl.BlockSpec((1, 1, d), lambda i, j: (mod_index(i), 0, 0)),
                  w_spec,
                  pl.BlockSpec((N_SMALL, d), lambda i, j: (0, 0))],
        out_specs=[pl.BlockSpec((tm, tn), lambda i, j: (i, j)),
                   pl.BlockSpec((tm, N_SMALL), lambda i, j: (i, 0)),
                   pl.BlockSpec((tm, d), lambda i, j: (i, 0))] + wout_specs,
        out_shape=[jax.ShapeDtypeStruct((rows, n_main), PROJ_DTYPE),
                   jax.ShapeDtypeStruct((rows, N_SMALL), F32),
                   jax.ShapeDtypeStruct((rows, d), BF16)] + wout_shapes,
        compiler_params=_cparams(("arbitrary", "arbitrary")),
        name="in_proj",
    )(x2d, shift, scale, w, w_small)


def _proj_call(u2d, w, starts, w_small, tm, layer=None):
    rows, d = u2d.shape
    tn = CW
    n_main = len(starts) * tn
    cast_w, w_spec, wout_specs, wout_shapes = _weight_specs(w, starts, layer, rows, tm, tn, d)
    return pl.pallas_call(
        functools.partial(_proj_kernel, cast_w),
        grid=(rows // tm, n_main // tn),
        in_specs=[pl.BlockSpec((tm, d), lambda i, j: (i, 0)),
                  w_spec,
                  pl.BlockSpec((N_SMALL, d), lambda i, j: (0, 0))],
        out_specs=[pl.BlockSpec((tm, tn), lambda i, j: (i, j)),
                   pl.BlockSpec((tm, N_SMALL), lambda i, j: (i, 0))] + wout_specs,
        out_shape=[jax.ShapeDtypeStruct((rows, n_main), PROJ_DTYPE),
                   jax.ShapeDtypeStruct((rows, N_SMALL), F32)] + wout_shapes,
        compiler_params=_cparams(("arbitrary", "arbitrary")),
        name="ssd_proj",
    )(u2d, w, w_small)


def _out_proj_kernel(ya_ref, yb_ref, x_ref, gate_ref, wa_ref, wb_ref, g_ref, b_ref, o_ref):
    acc = _dot(ya_ref[...], wa_ref[0]) + _dot(yb_ref[...], wb_ref[0])
    r = DEEPNORM_ALPHA * x_ref[...] + gate_ref[0] * acc
    mu = jnp.mean(r, axis=-1, keepdims=True)
    rc = r - mu
    var = jnp.mean(rc * rc, axis=-1, keepdims=True)
    o_ref[...] = rc * lax.rsqrt(var + LN_EPS) * g_ref[...] + b_ref[...]


def _out_proj_call(ya, yb, x2d, gate, w_out3, layer, ln_g, ln_b, mod_index, tm):
    rows, d = x2d.shape
    const = lambda i: (0, 0)
    return pl.pallas_call(
        _out_proj_kernel,
        grid=(rows // tm,),
        in_specs=[pl.BlockSpec((tm, W_A), lambda i: (i, 0)),
                  pl.BlockSpec((tm, W_B), lambda i: (i, 0)),
                  pl.BlockSpec((tm, d), lambda i: (i, 0)),
                  pl.BlockSpec((1, 1, d), lambda i: (mod_index(i), 0, 0)),
                  pl.BlockSpec((1, W_A, d), lambda i: (layer, 0, 0)),
                  pl.BlockSpec((1, W_B, d), lambda i: (layer, 1, 0)),
                  pl.BlockSpec((1, d), const),
                  pl.BlockSpec((1, d), const)],
        out_specs=pl.BlockSpec((tm, d), lambda i: (i, 0)),
        out_shape=jax.ShapeDtypeStruct((rows, d), F32),
        compiler_params=_cparams(("arbitrary",)),
        name="out_proj",
    )(ya, yb, x2d, gate, w_out3, w_out3, ln_g.reshape(1, d), ln_b.reshape(1, d))


def _conv_silu(x, prev_row, next_row, w, bias):
    rows = x.shape[0]
    row = lax.broadcasted_iota(jnp.int32, x.shape, 0)
    xm = jnp.where(row == 0, prev_row, pltpu.roll(x, 1, 0))
    xp = jnp.where(row == rows - 1, next_row, pltpu.roll(x, rows - 1, 0))
    y = w[0:1] * xm + w[1:2] * x + w[2:3] * xp + bias
    return _silu(y)


def _conv_block(x_ref, p_ref, n_ref, sc, sub, cs, has_prev, has_next, w, bias):
    r0 = sc * CHUNK
    if sc > 0:
        pr = x_ref[0, r0 - 1:r0, cs].astype(F32)
    else:
        pr = jnp.where(has_prev, p_ref[0, HALO - 1:HALO, cs].astype(F32), 0.0)
    if sc < sub - 1:
        nx = x_ref[0, r0 + CHUNK:r0 + CHUNK + 1, cs].astype(F32)
    else:
        nx = jnp.where(has_next, n_ref[0, 0:1, cs].astype(F32), 0.0)
    return _conv_silu(x_ref[0, r0:r0 + CHUNK, cs].astype(F32), pr, nx, w, bias)


def _tri_masks(reverse):
    t = lax.broadcasted_iota(jnp.int32, (CHUNK, CHUNK), 0)
    s = lax.broadcasted_iota(jnp.int32, (CHUNK, CHUNK), 1)
    mask = (s >= t) if reverse else (s <= t)
    mask_t = (t >= s) if reverse else (t <= s)
    return mask, mask.astype(F32), mask_t.astype(F32)


def _chunk_pos(reverse):
    i = pl.program_id(1)
    nc = pl.num_programs(1)
    j = (nc - 1 - i) if reverse else i
    return i, j, nc


def _chunk_specs(t_len, reverse, sub):
    blk = sub * CHUNK
    nc = t_len // blk
    nhb = t_len // HALO
    hpc = blk // HALO

    def jj(i):
        return (nc - 1 - i) if reverse else i

    def main(col, width):
        return pl.BlockSpec((1, blk, width), lambda b, i: (b, jj(i), (col * CW) // width))

    def prev(col):
        return pl.BlockSpec((1, HALO, CW), lambda b, i: (b, jnp.maximum(jj(i) * hpc - 1, 0), col))

    def nxt(col):
        return pl.BlockSpec((1, HALO, CW),
                            lambda b, i: (b, jnp.minimum((jj(i) + 1) * hpc, nhb - 1), col))

    def const2(shape):
        return pl.BlockSpec(shape, lambda b, i: (0, 0))

    return nc, main, prev, nxt, const2


def _mlstm_kernel(reverse, finalize, *refs):
    if finalize:
        (qk_ref, v_ref, gs_ref, gb_ref, c0_ref, m0_ref, hb_ref, o_ref, z_ref, mhw_ref,
         out_ref, cf_ref, mf_ref) = refs
    else:
        (q_ref, qp_ref, qn_ref, k_ref, kp_ref, kn_ref, v_ref, gs_ref, cw_ref, cb_ref, gb_ref,
         c0_ref, m0_ref, out_ref, qk_ref, cf_ref, mf_ref) = refs

    i, j, nc = _chunk_pos(reverse)

    @pl.when(i == 0)
    def _():
        cf_ref[...] = c0_ref[...]
        mf_ref[...] = m0_ref[...]

    has_prev = j > 0
    has_next = j < nc - 1
    for sc in (reversed(range(SUB_MLSTM)) if reverse else range(SUB_MLSTM)):
        _mlstm_chunk(reverse, finalize, sc, has_prev, has_next, refs)


def _mlstm_chunk(reverse, finalize, sc, has_prev, has_next, refs):
    if finalize:
        (qk_ref, v_ref, gs_ref, gb_ref, c0_ref, m0_ref, hb_ref, o_ref, z_ref, mhw_ref,
         out_ref, cf_ref, mf_ref) = refs
    else:
        (q_ref, qp_ref, qn_ref, k_ref, kp_ref, kn_ref, v_ref, gs_ref, cw_ref, cb_ref, gb_ref,
         c0_ref, m0_ref, out_ref, qk_ref, cf_ref, mf_ref) = refs
    rows = slice(sc * CHUNK, (sc + 1) * CHUNK)
    mask, tri, tri_t = _tri_masks(reverse)
    edge = 0 if reverse else CHUNK - 1

    gb = gs_ref[0, rows, :] + gb_ref[...]
    gbt = gb.T
    io = 2 * NH_A if reverse else 0
    b_all = _cumsum_rows(tri, LOG2E * _log_sigmoid(gb))
    li_c = LOG2E * gb[:, io:io + NH_A]
    b_c = b_all[:, io + NH_A:io + 2 * NH_A]
    b_r = jnp.dot(LOG2E * _log_sigmoid(gbt[io + NH_A:io + 2 * NH_A, :]), tri_t,
                  preferred_element_type=F32, precision=lax.Precision.HIGHEST)
    c_r = LOG2E * gbt[io:io + NH_A, :] - b_r

    m_prev = mf_ref[0][:, 0:NH_A]
    g_c = b_c[edge:edge + 1, :]
    a_c = g_c - b_c + li_c
    m_new = jnp.maximum(g_c + m_prev, jnp.max(a_c, axis=0, keepdims=True))
    w_c = jnp.exp2(a_c - m_new)
    dec = jnp.exp2(g_c + m_prev - m_new)

    heads = range(NH_A)
    ksl = [slice(h * DQK_A, (h + 1) * DQK_A) for h in heads]
    vsl = [slice(h * DV_A, (h + 1) * DV_A) for h in heads]
    kksl = [slice(QK_A + h * DQK_A, QK_A + (h + 1) * DQK_A) for h in heads]

    if finalize:
        qb = [qk_ref[0, rows, ksl[h]] for h in heads]
        kb = [qk_ref[0, rows, kksl[h]] for h in heads]
        qh = [t.astype(F32) for t in qb]
        kh = [t.astype(F32) for t in kb]
        gate = []
        for h in heads:
            o = o_ref[0, rows, vsl[h]].astype(F32)
            z = z_ref[0, rows, vsl[h]].astype(F32)
            gate.append(z / ((1.0 + jnp.exp(-o)) * (1.0 + jnp.exp(-z))))
    else:
        qh = [_conv_block(q_ref, qp_ref, qn_ref, sc, SUB_MLSTM, ksl[h], has_prev, has_next,
                          cw_ref[:, ksl[h]], cb_ref[:, ksl[h]]) * (DQK_A ** -0.5) for h in heads]
        kh = [_conv_block(k_ref, kp_ref, kn_ref, sc, SUB_MLSTM, ksl[h], has_prev, has_next,
                          cw_ref[:, kksl[h]], cb_ref[:, kksl[h]]) for h in heads]
        qb = [t.astype(BF16) for t in qh]
        kb = [t.astype(BF16) for t in kh]
        for h in heads:
            qk_ref[0, rows, ksl[h]] = qb[h]
            qk_ref[0, rows, kksl[h]] = kb[h]
    ones_l = jnp.ones((CHUNK, CHUNK), BF16)
    vb = [jnp.concatenate([v_ref[0, rows, vsl[h]].astype(BF16), ones_l], axis=1) for h in heads]
    c_prev = [cf_ref[0, h] for h in heads]
    mp = [m_prev[:, h:h + 1] for h in heads]

    s_qk = [_dot_nt(qb[h], kb[h]) for h in heads]
    q_c = [_dot(qb[h], c_prev[h].astype(BF16)) for h in heads]
    cm = [jnp.where(mask, c_r[h:h + 1, :], -jnp.inf) for h in heads]
    mm = [jnp.broadcast_to(jnp.maximum(jnp.max(cm[h], axis=1, keepdims=True), mp[h]),
                           (CHUNK, CHUNK)) for h in heads]

    kw = [kh[h] * w_c[:, h:h + 1] for h in heads]
    for h in heads:
        cf_ref[0, h] = dec[:, h:h + 1] * c_prev[h] + _dot_tn(kw[h].astype(BF16), vb[h])
    mf_ref[0, :, 0:NH_A] = m_new

    p = [s_qk[h] * jnp.exp2(cm[h] - mm[h]) for h in heads]
    p_v = [_dot(p[h].astype(BF16), vb[h]) for h in heads]
    w_int = [jnp.exp2(mp[h] - mm[h]) for h in heads]
    den = [w_int[h] * q_c[h][:, DV_A:] + p_v[h][:, DV_A:] for h in heads]
    b_t = [jnp.broadcast_to(b_c[:, h:h + 1], (CHUNK, CHUNK)) for h in heads]
    inv = [1.0 / jnp.maximum(jnp.abs(den[h]), jnp.exp2(-(b_t[h] + mm[h]))) for h in heads]
    hval = [jnp.concatenate(
        [(w_int[h] * q_c[h][:, c * CHUNK:(c + 1) * CHUNK] + p_v[h][:, c * CHUNK:(c + 1) * CHUNK])
         * inv[h] for c in range(DV_A // CHUNK)], axis=1) for h in heads]

    if not finalize:
        for h in heads:
            out_ref[0, rows, vsl[h]] = hval[h].astype(out_ref.dtype)
        return

    hs = [hval[h] + hb_ref[0, rows, vsl[h]].astype(F32) for h in heads]
    mu = [jnp.mean(hs[h], axis=-1, keepdims=True) for h in heads]
    hc = [hs[h] - mu[h] for h in heads]
    var = [jnp.mean(hc[h] * hc[h], axis=-1, keepdims=True) for h in heads]
    for h in heads:
        hn = hc[h] * lax.rsqrt(var[h] + LN_EPS) * mhw_ref[:, vsl[h]]
        out_ref[0, rows, vsl[h]] = (hn * gate[h]).astype(out_ref.dtype)


def _mlstm_call(reverse, finalize, state, *, p3, small3, gate_b128, conv_w=None, conv_b=None,
                qk=None, hb=None, mh_w=None):
    bsz, t_len, _ = p3.shape
    nc, main, prev, nxt, const2 = _chunk_specs(t_len, reverse, SUB_MLSTM)
    c0, m0 = state
    state_specs = [pl.BlockSpec((1, NH_A, DQK_A, DV_A + CHUNK), lambda b, i: (b, 0, 0, 0)),
                   pl.BlockSpec((1, 1, 128), lambda b, i: (b, 0, 0))]
    state_shapes = [jax.ShapeDtypeStruct(c0.shape, F32), jax.ShapeDtypeStruct(m0.shape, F32)]
    wide = main(0, W_A)
    if finalize:
        in_specs = [wide, main(COL_V, W_A), main(0, N_SMALL), const2((1, N_SMALL))] + state_specs
        in_specs += [wide, main(COL_O, W_A), main(COL_ZA, W_A), const2((1, W_A))]
        args = [qk, p3, small3, gate_b128, c0, m0, hb, p3, p3, mh_w]
        out_specs = [wide] + state_specs
        out_shape = [jax.ShapeDtypeStruct((bsz, t_len, W_A), BF16)] + state_shapes
    else:
        in_specs = [main(COL_Q, CW), prev(COL_Q), nxt(COL_Q),
                    main(COL_K, CW), prev(COL_K), nxt(COL_K),
                    main(COL_V, W_A), main(0, N_SMALL),
                    const2((3, 2 * QK_A)), const2((1, 2 * QK_A)), const2((1, N_SMALL))] + state_specs
        args = [p3, p3, p3, p3, p3, p3, p3, small3, conv_w, conv_b, gate_b128, c0, m0]
        out_specs = [wide, wide] + state_specs
        out_shape = [jax.ShapeDtypeStruct((bsz, t_len, W_A), HB_DTYPE),
                     jax.ShapeDtypeStruct((bsz, t_len, 2 * QK_A), BF16)] + state_shapes
    res = pl.pallas_call(
        functools.partial(_mlstm_kernel, reverse, finalize),
        grid=(bsz, nc), in_specs=in_specs, out_specs=out_specs, out_shape=out_shape,
        compiler_params=_cparams(("arbitrary", "arbitrary")),
        name="mlstm_" + ("bwd" if reverse else "fwd"),
    )(*args)
    if finalize:
        return res[0], tuple(res[1:])
    return res[0], res[1], tuple(res[2:])


def _ssd_kernel(reverse, finalize, *refs):
    if finalize:
        (xc_ref, bcc_ref, gs_ref, dtb_ref, alog_ref, s0_ref, yb_ref, z0_ref, z1_ref, dsk_ref,
         nw_ref, out_ref, sf_ref, y_s) = refs
    else:
        (x0_ref, x0p_ref, x0n_ref, x1_ref, x1p_ref, x1n_ref, bc_ref, bcp_ref, bcn_ref,
         gs_ref, cw_ref, cb_ref, dtb_ref, alog_ref, s0_ref, out_ref, xc_ref, sf_ref) = refs

    i, j, nc = _chunk_pos(reverse)

    @pl.when(i == 0)
    def _():
        sf_ref[...] = s0_ref[...]

    has_prev = j > 0
    has_next = j < nc - 1
    for sc in (reversed(range(SUB_SSD)) if reverse else range(SUB_SSD)):
        _ssd_chunk(reverse, finalize, sc, has_prev, has_next, refs)


def _ssd_chunk(reverse, finalize, sc, has_prev, has_next, refs):
    if finalize:
        (xc_ref, bcc_ref, gs_ref, dtb_ref, alog_ref, s0_ref, yb_ref, z0_ref, z1_ref, dsk_ref,
         nw_ref, out_ref, sf_ref, y_s) = refs
    else:
        (x0_ref, x0p_ref, x0n_ref, x1_ref, x1p_ref, x1n_ref, bc_ref, bcp_ref, bcn_ref,
         gs_ref, cw_ref, cb_ref, dtb_ref, alog_ref, s0_ref, out_ref, xc_ref, sf_ref) = refs
    rows = slice(sc * CHUNK, (sc + 1) * CHUNK)
    _, tri, _ = _tri_masks(reverse)
    edge = 0 if reverse else CHUNK - 1

    t_i = lax.broadcasted_iota(jnp.int32, (CHUNK, CHUNK), 0)
    l_i = lax.broadcasted_iota(jnp.int32, (CHUNK, CHUNK), 1)
    s_lo = jnp.where(l_i < HALF, l_i, l_i - HALF)
    lo_half = l_i < HALF
    if reverse:
        mask1, mask2 = s_lo >= t_i, s_lo + HALF >= t_i
    else:
        mask1, mask2 = s_lo <= t_i, s_lo + HALF <= t_i
    k_i = lax.broadcasted_iota(jnp.int32, (2 * CHUNK, CHUNK), 0)
    kl_i = lax.broadcasted_iota(jnp.int32, (2 * CHUNK, CHUNK), 1)
    blockdiag = ((k_i // HALF) % 2 == 1) == (kl_i >= HALF)

    dt_all = _softplus(gs_ref[0, rows, :] + dtb_ref[...])
    da_all = dt_all * (-LOG2E * jnp.exp(alog_ref[...]))
    acum_all = _cumsum_rows(tri, da_all)
    do = H_B if reverse else 0
    acum_t = acum_all.T
    a_even = acum_t[do:do + NPAIR, :]
    a_odd = acum_t[do + NPAIR:do + H_B, :]
    lane16 = lax.broadcasted_iota(jnp.int32, (NPAIR, CHUNK), 1) < HALF
    rv1 = jnp.where(lane16, a_even, pltpu.roll(a_odd, HALF, 1))
    rv2 = jnp.where(lane16, pltpu.roll(a_even, HALF, 1), a_odd)

    if finalize:
        bmat = bcc_ref[0, rows, 0:GP_B]
        cmat = bcc_ref[0, rows, GP_B:2 * GP_B]
        ssq = jnp.zeros((CHUNK, CHUNK), F32)
    else:
        bmat = _conv_block(bc_ref, bcp_ref, bcn_ref, sc, SUB_SSD, slice(0, GP_B), has_prev, has_next,
                           cw_ref[:, W_B:W_B + GP_B], cb_ref[:, W_B:W_B + GP_B]).astype(BF16)
        cmat = _conv_block(bc_ref, bcp_ref, bcn_ref, sc, SUB_SSD, slice(GP_B, 2 * GP_B), has_prev, has_next,
                           cw_ref[:, W_B + GP_B:W_B + 2 * GP_B],
                           cb_ref[:, W_B + GP_B:W_B + 2 * GP_B]).astype(BF16)
        xc_ref[0, rows, W_B:W_B + GP_B] = bmat
        xc_ref[0, rows, W_B + GP_B:W_B + 2 * GP_B] = cmat

    for g in range(G_B):
        gs = slice(g * GP_B, (g + 1) * GP_B)
        if finalize:
            xg = xc_ref[0, rows, gs].astype(F32)
        else:
            xr, xpr, xnr = (x0_ref, x0p_ref, x0n_ref) if g < 2 else (x1_ref, x1p_ref, x1n_ref)
            cs = slice((g % 2) * GP_B, (g % 2 + 1) * GP_B)
            xg = _conv_block(xr, xpr, xnr, sc, SUB_SSD, cs, has_prev, has_next, cw_ref[:, gs], cb_ref[:, gs])
            xc_ref[0, rows, gs] = xg.astype(BF16)
        bg = bmat[:, g * N_B:(g + 1) * N_B]
        cg = cmat[:, g * N_B:(g + 1) * N_B]
        cb1 = _dot_nt(cg, jnp.concatenate([bg[0:HALF], bg[0:HALF]], axis=0))
        cb2 = _dot_nt(cg, jnp.concatenate([bg[HALF:], bg[HALF:]], axis=0))
        s_prev = sf_ref[0, g]
        cs_prev = _dot(cg, s_prev.astype(BF16))
        y_parts, xw_parts, tot_parts = [], [], []
        for pp in range(R_B // 2):
            pr = g * (R_B // 2) + pp
            ps = slice(pp * CHUNK, (pp + 1) * CHUNK)
            idx = jnp.where(lo_half, do + pr, do + NPAIR + pr)
            acum_b = jnp.take_along_axis(acum_all, idx, axis=1)
            dt_b = jnp.take_along_axis(dt_all, idx, axis=1)
            xdt = xg[:, ps] * dt_b
            xdtb = xdt.astype(BF16)
            w1 = cb1 * jnp.exp2(jnp.where(mask1, acum_b - rv1[pr:pr + 1, :], -jnp.inf))
            w2 = cb2 * jnp.exp2(jnp.where(mask2, acum_b - rv2[pr:pr + 1, :], -jnp.inf))
            w12 = jnp.concatenate([w1.astype(BF16), w2.astype(BF16)], axis=1)
            x12 = jnp.concatenate([xdtb[0:HALF], xdtb[0:HALF], xdtb[HALF:], xdtb[HALF:]], axis=0)
            x12 = jnp.where(blockdiag, x12, jnp.zeros_like(x12))
            tot = acum_b[edge:edge + 1, :]
            y_parts.append(_dot(w12, x12) + jnp.exp2(acum_b) * cs_prev[:, ps])
            xw_parts.append((xdt * jnp.exp2(tot - acum_b)).astype(BF16))
            tot_parts.append(tot)
        xw = jnp.concatenate(xw_parts, axis=1)
        etot = jnp.exp2(jnp.concatenate(tot_parts, axis=1))
        sf_ref[0, g] = etot * s_prev + _dot_tn(bg, xw)
        y_grp = jnp.concatenate(y_parts, axis=1)
        if finalize:
            z_ref = z0_ref if g < 2 else z1_ref
            zs = slice((g % 2) * GP_B, (g % 2 + 1) * GP_B)
            y = y_grp + yb_ref[0, rows, gs].astype(F32) + dsk_ref[:, gs] * xg
            y = y * _silu(z_ref[0, rows, zs].astype(F32))
            y_s[:, gs] = y
            y2 = y * y
            ssq = ssq + (y2[:, 0:CHUNK] + y2[:, CHUNK:2 * CHUNK]
                         + y2[:, 2 * CHUNK:3 * CHUNK] + y2[:, 3 * CHUNK:])
        else:
            out_ref[0, rows, gs] = y_grp.astype(out_ref.dtype)

    if finalize:
        inv = lax.rsqrt(jnp.sum(ssq, axis=-1, keepdims=True) * (1.0 / W_B) + LN_EPS)
        for half in range(2):
            cs = slice(half * CW, (half + 1) * CW)
            out_ref[0, rows, cs] = (y_s[:, cs] * inv * nw_ref[:, cs]).astype(out_ref.dtype)


def _ssd_call(reverse, finalize, s0, *, small3, dtb128, alog128, p3, conv_w=None, conv_b=None,
              xc=None, yb=None, d_skip=None, norm_w=None):
    bsz, t_len, _ = p3.shape
    nc, main, prev, nxt, const2 = _chunk_specs(t_len, reverse, SUB_SSD)
    state_spec = pl.BlockSpec((1, G_B, N_B, GP_B), lambda b, i: (b, 0, 0, 0))
    state_shape = jax.ShapeDtypeStruct(s0.shape, F32)
    xc_spec = main(0, W_B + 2 * GP_B)
    if finalize:
        in_specs = [main(0, W_B), main(2, CW), main(0, N_SMALL), const2((1, N_SMALL)),
                    const2((1, N_SMALL)), state_spec,
                    main(0, W_B), main(COL_ZB, CW), main(COL_ZB + 1, CW),
                    const2((1, W_B)), const2((1, W_B))]
        args = [xc, xc, small3, dtb128, alog128, s0, yb, p3, p3, d_skip, norm_w]
        out_specs = [main(0, W_B), state_spec]
        out_shape = [jax.ShapeDtypeStruct((bsz, t_len, W_B), BF16), state_shape]
        scratch = [pltpu.VMEM((CHUNK, W_B), F32)]
    else:
        in_specs = [main(COL_X, CW), prev(COL_X), nxt(COL_X),
                    main(COL_X + 1, CW), prev(COL_X + 1), nxt(COL_X + 1),
                    main(COL_BC, CW), prev(COL_BC), nxt(COL_BC),
                    main(0, N_SMALL),
                    const2((3, 3 * CW)), const2((1, 3 * CW)), const2((1, N_SMALL)),
                    const2((1, N_SMALL)), state_spec]
        args = [p3] * 9 + [small3, conv_w, conv_b, dtb128, alog128, s0]
        out_specs = [main(0, W_B), xc_spec, state_spec]
        out_shape = [jax.ShapeDtypeStruct((bsz, t_len, W_B), HB_DTYPE),
                     jax.ShapeDtypeStruct((bsz, t_len, W_B + 2 * GP_B), BF16), state_shape]
        scratch = []
    return pl.pallas_call(
        functools.partial(_ssd_kernel, reverse, finalize),
        grid=(bsz, nc), in_specs=in_specs, out_specs=out_specs, out_shape=out_shape,
        scratch_shapes=scratch,
        compiler_params=_cparams(("arbitrary", "arbitrary")),
        name="ssd_" + ("bwd" if reverse else "fwd"),
    )(*args)


def _to_colmajor(t, bsz):
    ch = t.shape[-1]
    return t.reshape(bsz, -1, GRID_W, ch).transpose(0, 2, 1, 3).reshape(-1, ch)


def _from_colmajor(t, bsz):
    ch = t.shape[-1]
    return t.reshape(bsz, GRID_W, -1, ch).transpose(0, 2, 1, 3).reshape(-1, ch)


def _mixer_scans(pa, sa, pb, sb, bsz, t_len, lp, states):
    st_mf, st_mb, st_sf, st_sb = states
    pa = pa.reshape(bsz, t_len, N_MAIN_A)
    sa = sa.reshape(bsz, t_len, N_SMALL)
    h_b, qk, st_mb = _mlstm_call(True, False, st_mb, p3=pa, small3=sa, gate_b128=lp["gate_b128"],
                                 conv_w=lp["conv_qk_w"], conv_b=lp["conv_qk_b"])
    y_a, st_mf = _mlstm_call(False, True, st_mf, p3=pa, small3=sa, gate_b128=lp["gate_b128"],
                             qk=qk, hb=h_b, mh_w=lp["mh_w"])

    pb = pb.reshape(bsz, t_len, N_MAIN_B)
    sb = sb.reshape(bsz, t_len, N_SMALL)
    y_bb, xc, st_sb = _ssd_call(True, False, st_sb, small3=sb, dtb128=lp["dtb128"],
                                alog128=lp["alog128"], p3=pb, conv_w=lp["conv_xbc_w"],
                                conv_b=lp["conv_xbc_b"])
    y_b, st_sf = _ssd_call(False, True, st_sf, small3=sb, dtb128=lp["dtb128"],
                           alog128=lp["alog128"], p3=pb, xc=xc, yb=y_bb, d_skip=lp["d_skip"],
                           norm_w=lp["ssm_w"])
    return y_a, y_b, (st_mf, st_mb, st_sf, st_sb)


def _mixer(x2d, bsz, t_len, latent, shift, scale, mod_index, tm, tm_b, w_a, w_b, lp, states,
           cast_layer=None):
    res_a = _in_proj_call(x2d, shift, scale, w_a, A_STARTS, lp["w_small_a"], mod_index, tm,
                          cast_layer)
    pa, sa, u2d = res_a[:3]
    us2d = _to_colmajor(u2d, bsz) if latent else u2d
    res_b = _proj_call(us2d, w_b, B_STARTS, lp["w_small_b"], tm_b, cast_layer)
    pb, sb = res_b[:2]
    if cast_layer is not None:
        w_a, w_b = res_a[3], res_b[2]
    y_a, y_b, states = _mixer_scans(pa, sa, pb, sb, bsz, t_len, lp, states)
    y_b = y_b.reshape(bsz * t_len, W_B)
    if latent:
        y_b = _from_colmajor(y_b, bsz)
    return y_a.reshape(bsz * t_len, W_A), y_b, states, w_a, w_b


def _zero_states(bsz):
    m_state = (jnp.zeros((bsz, NH_A, DQK_A, DV_A + CHUNK), F32), jnp.zeros((bsz, 1, 128), F32))
    s_state = jnp.zeros((bsz, G_B, N_B, GP_B), F32)
    return (m_state, m_state, s_state, s_state)


def _layer_params(l, w_t, conv_qk_w, conv_qk_b, gate_b, mh_norm_w, conv_xbc_w, conv_xbc_b,
                  dt_bias, a_log, d_skip, ssm_norm_w):
    feats = lambda a, n: lax.slice(w_t, (l, a, 0), (l + 1, a + n, w_t.shape[2]))[0]

    def pad_cols(a):
        return jnp.concatenate([a, jnp.zeros((a.shape[0], N_SMALL - a.shape[1]), a.dtype)], axis=1)

    def pad_rows(a):
        return jnp.concatenate([a, jnp.zeros((N_SMALL - a.shape[0], a.shape[1]), a.dtype)], axis=0)

    def even_odd(a):
        a4 = a.reshape(a.shape[0], 2, NPAIR, 2)
        return a4.transpose(0, 1, 3, 2).reshape(a.shape[0], 2 * H_B)

    return {
        "w_small_a": pad_rows(feats(O_GATE, 4 * NH_A)),
        "w_small_b": pad_rows(even_odd(feats(O_DT, 2 * H_B).T).T),
        "conv_qk_w": conv_qk_w[l], "conv_qk_b": conv_qk_b[l].reshape(1, -1),
        "gate_b128": pad_cols(gate_b[l].reshape(1, -1)),
        "mh_w": mh_norm_w[l].reshape(1, W_A),
        "conv_xbc_w": conv_xbc_w[l], "conv_xbc_b": conv_xbc_b[l].reshape(1, -1),
        "dtb128": pad_cols(even_odd(dt_bias[l].reshape(1, -1))),
        "alog128": pad_cols(even_odd(a_log[l].reshape(1, -1))),
        "d_skip": jnp.repeat(d_skip[l], P_B).reshape(1, W_B),
        "ssm_w": ssm_norm_w[l].reshape(1, W_B),
    }


def kernel(x, c, ctx, c_ctx, w_ada, b_ada, w_in, conv_qk_w, conv_qk_b, gate_b, mh_norm_w,
           conv_xbc_w, conv_xbc_b, dt_bias, a_log, d_skip, ssm_norm_w, w_out, ln_g, ln_b):
    bsz, t_len, d = x.shape
    ctx_len = ctx.shape[1]
    depth = w_in.shape[0]
    assert d == D_MODEL and depth == DEPTH and bsz + 1 <= 8
    assert t_len % (CHUNK * GRID_W) == 0 and ctx_len % CHUNK == 0

    cvec = jnp.concatenate([c, c_ctx[None, :]], axis=0)
    mod = _ada_call(jnp.broadcast_to(cvec[:, :, None], (bsz + 1, d, 128)), w_ada, b_ada)

    w_t = jnp.swapaxes(w_in, 1, 2)
    w_out_bf = _cast_bf16(w_out, 2048)

    x2d = x.reshape(bsz * t_len, d)
    xc2d = ctx.reshape(bsz * ctx_len, d)
    tm_in, tm_ssd, tm_out, tm_ctx_in, tm_ctx_out = 1024, 2048, 512, bsz * ctx_len, 256
    lat_in = lambda i: i // (t_len // tm_in)
    lat_out = lambda i: i // (t_len // tm_out)
    ctx_index = lambda i: bsz

    for l in range(depth):
        last = l == depth - 1
        lp = _layer_params(l, w_t, conv_qk_w, conv_qk_b, gate_b, mh_norm_w, conv_xbc_w,
                           conv_xbc_b, dt_bias, a_log, d_skip, ssm_norm_w)
        shift = mod[l, :, 0:d].reshape(8, 1, d)
        scale = mod[l, :, d:2 * d].reshape(8, 1, d)
        gate = mod[l, :, 2 * d:3 * d].reshape(8, 1, d)

        yc_a, yc_b, ctx_states, w_a, w_b = _mixer(xc2d, bsz, ctx_len, False, shift, scale,
                                                  ctx_index, tm_ctx_in, tm_ctx_in, w_t, w_t,
                                                  lp, _zero_states(bsz), cast_layer=l)
        y_a, y_b, _, _, _ = _mixer(x2d, bsz, t_len, True, shift, scale, lat_in, tm_in, tm_ssd,
                                   w_a, w_b, lp, ctx_states)
        x2d = _out_proj_call(y_a, y_b, x2d, gate, w_out_bf, l, ln_g[l], ln_b[l], lat_out, tm_out)
        if not last:
            xc2d = _out_proj_call(yc_a, yc_b, xc2d, gate, w_out_bf, l, ln_g[l], ln_b[l],
                                  ctx_index, tm_ctx_out)
    return x2d.reshape(bsz, t_len, d)
```

```python
import functools

import jax
import jax.numpy as jnp
from jax import lax
from jax.experimental import pallas as pl
from jax.experimental.pallas import tpu as pltpu

F32 = jnp.float32
BF16 = jnp.bfloat16

D_MODEL = 2048
DEPTH = 2
GRID_W = 64
NH_A = 8
DV_A = 256
DQK_A = 128
W_A = NH_A * DV_A
QK_A = NH_A * DQK_A
P_B = 64
H_B = 32
G_B = 4
R_B = H_B // G_B
N_B = 128
W_B = H_B * P_B
GP_B = R_B * P_B
NPAIR = H_B // 2
CHUNK = 128
HALF = CHUNK // 2
SUB_MLSTM = 1
HEAD_GROUP = 4
SUB_SSD = 2
LN_EPS = 1e-5
DEEPNORM_ALPHA = (2 * DEPTH) ** 0.25
LOG2E = 1.4426950408889634

CW = 1024
COL_Q, COL_K, COL_V, COL_O, COL_ZA = 0, 1, 2, 4, 6
N_MAIN_A = 8 * CW
COL_X, COL_BC, COL_ZB = 0, 2, 3
N_MAIN_B = 5 * CW
N_SMALL = 128
O_GATE = N_MAIN_A
O_XBC = O_GATE + 4 * NH_A
O_DT = O_XBC + W_B + 2 * G_B * N_B
O_ZB = O_DT + 2 * H_B
A_STARTS = [k * CW for k in range(N_MAIN_A // CW)]
B_STARTS = ([O_XBC + k * CW for k in range((W_B + 2 * G_B * N_B) // CW)]
            + [O_ZB + k * CW for k in range(W_B // CW)])

HALO = 16
VMEM_LIMIT = 56 * 1024 * 1024

PROJ_DTYPE = BF16
HB_DTYPE = BF16


def _cparams(sem):
    return pltpu.CompilerParams(dimension_semantics=sem, vmem_limit_bytes=VMEM_LIMIT)


def _softplus(x):
    return jnp.maximum(x, 0.0) + jnp.log1p(jnp.exp(-jnp.abs(x)))


def _log_sigmoid(x):
    return jnp.minimum(x, 0.0) - jnp.log1p(jnp.exp(-jnp.abs(x)))


def _silu(x):
    return x * jax.nn.sigmoid(x)


def _dot(a, b):
    return jnp.dot(a, b, preferred_element_type=F32)


def _dot_nt(a, b):
    return lax.dot_general(a, b, (((1,), (1,)), ((), ())), preferred_element_type=F32)


def _dot_tn(a, b):
    return lax.dot_general(a, b, (((0,), (0,)), ((), ())), preferred_element_type=F32)


def _cumsum_rows(tri, x):
    return jnp.dot(tri, x, preferred_element_type=F32, precision=lax.Precision.HIGHEST)


def _ada_kernel(c_ref, w_ref, b_ref, o_ref):
    n_rows, d, _ = c_ref.shape
    tn = w_ref.shape[2]
    o_ref[...] = jnp.zeros(o_ref.shape, F32)
    for r in range(n_rows):
        s = _silu(c_ref[r]).reshape(d // 8, 8, 128)
        for cb in range(tn // 128):
            cs = slice(cb * 128, (cb + 1) * 128)
            acc = jnp.sum(w_ref[0, :, cs].reshape(d // 8, 8, 128) * s, axis=0)
            o_ref[0, r:r + 1, cs] = jnp.sum(acc, axis=0, keepdims=True) + b_ref[0, :, cs]


def _ada_call(crep, w_ada, b_ada):
    depth, d, n3 = w_ada.shape
    n_rows = crep.shape[0]
    tn = 1024
    return pl.pallas_call(
        _ada_kernel,
        grid=(depth, n3 // tn),
        in_specs=[pl.BlockSpec((n_rows, d, 128), lambda l, j: (0, 0, 0)),
                  pl.BlockSpec((1, d, tn), lambda l, j: (l, 0, j)),
                  pl.BlockSpec((1, 1, tn), lambda l, j: (l, 0, j))],
        out_specs=pl.BlockSpec((1, 8, tn), lambda l, j: (l, 0, j)),
        out_shape=jax.ShapeDtypeStruct((depth, 8, n3), F32),
        compiler_params=_cparams(("arbitrary", "arbitrary")),
        name="adaln_mod",
    )(crep, w_ada, b_ada.reshape(depth, 1, n3))


def _cast_kernel(x_ref, o_ref):
    o_ref[...] = x_ref[...].astype(o_ref.dtype)


def _cast_bf16(w3, tr):
    depth, rows, cols = w3.shape
    spec = pl.BlockSpec((1, tr, cols), lambda l, i: (l, i, 0))
    return pl.pallas_call(
        _cast_kernel, grid=(depth, pl.cdiv(rows, tr)), in_specs=[spec], out_specs=spec,
        out_shape=jax.ShapeDtypeStruct(w3.shape, BF16),
        compiler_params=_cparams(("arbitrary", "arbitrary")),
        name="cast_bf16",
    )(w3)


def _proj_tile(cast_w, u_ref, w_ref, p_ref, wout):
    if cast_w:
        w = w_ref[0].astype(BF16)
        wout[0][...] = w
    else:
        w = w_ref[...]
    p_ref[...] = _dot_nt(u_ref[...], w).astype(p_ref.dtype)


def _in_proj_kernel(cast_w, x_ref, shift_ref, scale_ref, w_ref, ws_ref, p_ref, small_ref, u_ref,
                    *wout):
    @pl.when(pl.program_id(1) == 0)
    def _():
        x = x_ref[...]
        mu = jnp.mean(x, axis=-1, keepdims=True)
        xc = x - mu
        var = jnp.mean(xc * xc, axis=-1, keepdims=True)
        u = xc * lax.rsqrt(var + LN_EPS) * (1.0 + scale_ref[0]) + shift_ref[0]
        ub = u.astype(BF16)
        u_ref[...] = ub
        small_ref[...] = _dot_nt(ub, ws_ref[...].astype(BF16))

    _proj_tile(cast_w, u_ref, w_ref, p_ref, wout)


def _proj_kernel(cast_w, u_ref, w_ref, ws_ref, p_ref, small_ref, *wout):
    @pl.when(pl.program_id(1) == 0)
    def _():
        small_ref[...] = _dot_nt(u_ref[...], ws_ref[...].astype(BF16))

    _proj_tile(cast_w, u_ref, w_ref, p_ref, wout)


def _weight_specs(w, starts, layer, tile0, rows, tm, tn, d):
    if layer is None:
        return False, pl.BlockSpec((tn, d), lambda i, j: (tile0 + j, 0)), [], []
    assert rows == tm and tile0 == 0

    def start(j):
        row = starts[0]
        for k in range(1, len(starts)):
            row = jnp.where(j >= k, starts[k], row)
        return pl.multiple_of(row, 32)

    spec = pl.BlockSpec((pl.Element(1), pl.Element(tn), pl.Element(d)),
                        lambda i, j: (layer, start(j), 0))
    return (True, spec, [pl.BlockSpec((tn, d), lambda i, j: (j, 0))],
            [jax.ShapeDtypeStruct((len(starts) * tn, d), BF16)])


def _in_proj_call(x2d, shift, scale, w, starts, w_small, mod_index, tm, layer=None, tile0=0):
    rows, d = x2d.shape
    tn = CW
    n_main = len(starts) * tn
    n_small = w_small.shape[0]
    cast_w, w_spec, wout_specs, wout_shapes = _weight_specs(w, starts, layer, tile0, rows, tm,
                                                            tn, d)
    return pl.pallas_call(
        functools.partial(_in_proj_kernel, cast_w),
        grid=(rows // tm, n_main // tn),
        in_specs=[pl.BlockSpec((tm, d), lambda i, j: (i, 0)),
                  pl.BlockSpec((1, 1, d), lambda i, j: (mod_index(i), 0, 0)),
                  pl.BlockSpec((1, 1, d), lambda i, j: (mod_index(i), 0, 0)),
                  w_spec,
                  pl.BlockSpec((n_small, d), lambda i, j: (0, 0))],
        out_specs=[pl.BlockSpec((tm, tn), lambda i, j: (i, j)),
                   pl.BlockSpec((tm, n_small), lambda i, j: (i, 0)),
                   pl.BlockSpec((tm, d), lambda i, j: (i, 0))] + wout_specs,
        out_shape=[jax.ShapeDtypeStruct((rows, n_main), PROJ_DTYPE),
                   jax.ShapeDtypeStruct((rows, n_small), F32),
                   jax.ShapeDtypeStruct((rows, d), BF16)] + wout_shapes,
        compiler_params=_cparams(("arbitrary", "arbitrary")),
        name="in_proj",
    )(x2d, shift, scale, w, w_small)


def _proj_call(u2d, w, starts, w_small, tm, tile0=0):
    rows, d = u2d.shape
    tn = CW
    n_main = len(starts) * tn
    cast_w, w_spec, wout_specs, wout_shapes = _weight_specs(w, starts, None, tile0, rows, tm,
                                                            tn, d)
    return pl.pallas_call(
        functools.partial(_proj_kernel, cast_w),
        grid=(rows // tm, n_main // tn),
        in_specs=[pl.BlockSpec((tm, d), lambda i, j: (i, 0)),
                  w_spec,
                  pl.BlockSpec((N_SMALL, d), lambda i, j: (0, 0))],
        out_specs=[pl.BlockSpec((tm, tn), lambda i, j: (i, j)),
                   pl.BlockSpec((tm, N_SMALL), lambda i, j: (i, 0))] + wout_specs,
        out_shape=[jax.ShapeDtypeStruct((rows, n_main), PROJ_DTYPE),
                   jax.ShapeDtypeStruct((rows, N_SMALL), F32)] + wout_shapes,
        compiler_params=_cparams(("arbitrary", "arbitrary")),
        name="ssd_proj",
    )(u2d, w, w_small)


def _out_proj_kernel(ya_ref, yb_ref, x_ref, gate_ref, wa_ref, wb_ref, g_ref, b_ref, o_ref):
    acc = _dot(ya_ref[...], wa_ref[0]) + _dot(yb_ref[...], wb_ref[0])
    r = DEEPNORM_ALPHA * x_ref[...] + gate_ref[0] * acc
    mu = jnp.mean(r, axis=-1, keepdims=True)
    rc = r - mu
    var = jnp.mean(rc * rc, axis=-1, keepdims=True)
    o_ref[...] = rc * lax.rsqrt(var + LN_EPS) * g_ref[...] + b_ref[...]


def _out_proj_call(ya, yb, x2d, gate, w_out3, layer, ln_g, ln_b, mod_index, tm):
    rows, d = x2d.shape
    const = lambda i: (0, 0)
    return pl.pallas_call(
        _out_proj_kernel,
        grid=(rows // tm,),
        in_specs=[pl.BlockSpec((tm, W_A), lambda i: (i, 0)),
                  pl.BlockSpec((tm, W_B), lambda i: (i, 0)),
                  pl.BlockSpec((tm, d), lambda i: (i, 0)),
                  pl.BlockSpec((1, 1, d), lambda i: (mod_index(i), 0, 0)),
                  pl.BlockSpec((1, W_A, d), lambda i: (layer, 0, 0)),
                  pl.BlockSpec((1, W_B, d), lambda i: (layer, 1, 0)),
                  pl.BlockSpec((1, d), const),
                  pl.BlockSpec((1, d), const)],
        out_specs=pl.BlockSpec((tm, d), lambda i: (i, 0)),
        out_shape=jax.ShapeDtypeStruct((rows, d), F32),
        compiler_params=_cparams(("arbitrary",)),
        name="out_proj",
    )(ya, yb, x2d, gate, w_out3, w_out3, ln_g.reshape(1, d), ln_b.reshape(1, d))


def _conv_silu(x, prev_row, next_row, w, bias):
    rows = x.shape[0]
    row = lax.broadcasted_iota(jnp.int32, x.shape, 0)
    xm = jnp.where(row == 0, prev_row, pltpu.roll(x, 1, 0))
    xp = jnp.where(row == rows - 1, next_row, pltpu.roll(x, rows - 1, 0))
    y = w[0:1] * xm + w[1:2] * x + w[2:3] * xp + bias
    return _silu(y)


def _conv_block(x_ref, p_ref, n_ref, sc, sub, cs, has_prev, has_next, w, bias):
    r0 = sc * CHUNK
    if sc > 0:
        pr = x_ref[0, r0 - 1:r0, cs].astype(F32)
    else:
        pr = jnp.where(has_prev, p_ref[0, HALO - 1:HALO, cs].astype(F32), 0.0)
    if sc < sub - 1:
        nx = x_ref[0, r0 + CHUNK:r0 + CHUNK + 1, cs].astype(F32)
    else:
        nx = jnp.where(has_next, n_ref[0, 0:1, cs].astype(F32), 0.0)
    return _conv_silu(x_ref[0, r0:r0 + CHUNK, cs].astype(F32), pr, nx, w, bias)


def _tri_masks(reverse):
    t = lax.broadcasted_iota(jnp.int32, (CHUNK, CHUNK), 0)
    s = lax.broadcasted_iota(jnp.int32, (CHUNK, CHUNK), 1)
    mask = (s >= t) if reverse else (s <= t)
    mask_t = (t >= s) if reverse else (t <= s)
    return mask, mask.astype(F32), mask_t.astype(F32)


def _chunk_pos(reverse):
    i = pl.program_id(1)
    nc = pl.num_programs(1)
    j = (nc - 1 - i) if reverse else i
    return i, j, nc


def _chunk_specs(t_len, reverse, sub):
    blk = sub * CHUNK
    nc = t_len // blk
    nhb = t_len // HALO
    hpc = blk // HALO

    def jj(i):
        return (nc - 1 - i) if reverse else i

    def main(col, width):
        return pl.BlockSpec((1, blk, width), lambda b, i: (b, jj(i), (col * CW) // width))

    def prev(col):
        return pl.BlockSpec((1, HALO, CW), lambda b, i: (b, jnp.maximum(jj(i) * hpc - 1, 0), col))

    def nxt(col):
        return pl.BlockSpec((1, HALO, CW),
                            lambda b, i: (b, jnp.minimum((jj(i) + 1) * hpc, nhb - 1), col))

    def const2(shape):
        return pl.BlockSpec(shape, lambda b, i: (0, 0))

    return nc, main, prev, nxt, const2


def _mlstm_kernel(reverse, finalize, *refs):
    if finalize:
        (qk_ref, v_ref, gs_ref, gb_ref, c0_ref, m0_ref, hb_ref, o_ref, z_ref, mhw_ref,
         out_ref, cf_ref, mf_ref) = refs
    else:
        (q_ref, qp_ref, qn_ref, k_ref, kp_ref, kn_ref, v_ref, gs_ref, cw_ref, cb_ref, gb_ref,
         c0_ref, m0_ref, out_ref, qk_ref, cf_ref, mf_ref) = refs

    i, j, nc = _chunk_pos(reverse)

    @pl.when(i == 0)
    def _():
        cf_ref[...] = c0_ref[...]
        mf_ref[...] = m0_ref[...]

    has_prev = j > 0
    has_next = j < nc - 1
    for sc in (reversed(range(SUB_MLSTM)) if reverse else range(SUB_MLSTM)):
        _mlstm_chunk(reverse, finalize, sc, has_prev, has_next, refs)


def _mlstm_chunk(reverse, finalize, sc, has_prev, has_next, refs):
    if finalize:
        (qk_ref, v_ref, gs_ref, gb_ref, c0_ref, m0_ref, hb_ref, o_ref, z_ref, mhw_ref,
         out_ref, cf_ref, mf_ref) = refs
    else:
        (q_ref, qp_ref, qn_ref, k_ref, kp_ref, kn_ref, v_ref, gs_ref, cw_ref, cb_ref, gb_ref,
         c0_ref, m0_ref, out_ref, qk_ref, cf_ref, mf_ref) = refs
    rows = slice(sc * CHUNK, (sc + 1) * CHUNK)
    mask, tri, tri_t = _tri_masks(reverse)
    edge = 0 if reverse else CHUNK - 1

    gb = gs_ref[0, rows, :] + gb_ref[...]
    gbt = gb.T
    io = 2 * NH_A if reverse else 0
    b_all = _cumsum_rows(tri, LOG2E * _log_sigmoid(gb))
    li_c = LOG2E * gb[:, io:io + NH_A]
    b_c = b_all[:, io + NH_A:io + 2 * NH_A]
    b_r = jnp.dot(LOG2E * _log_sigmoid(gbt[io + NH_A:io + 2 * NH_A, :]), tri_t,
                  preferred_element_type=F32, precision=lax.Precision.HIGHEST)
    c_r = LOG2E * gbt[io:io + NH_A, :] - b_r

    m_prev = mf_ref[0][:, 0:NH_A]
    g_c = b_c[edge:edge + 1, :]
    a_c = g_c - b_c + li_c
    m_new = jnp.maximum(g_c + m_prev, jnp.max(a_c, axis=0, keepdims=True))
    w_c = jnp.exp2(a_c - m_new)
    dec = jnp.exp2(g_c + m_prev - m_new)

    ksl = [slice(h * DQK_A, (h + 1) * DQK_A) for h in range(NH_A)]
    vsl = [slice(h * DV_A, (h + 1) * DV_A) for h in range(NH_A)]
    kksl = [slice(QK_A + h * DQK_A, QK_A + (h + 1) * DQK_A) for h in range(NH_A)]
    mf_ref[0, :, 0:NH_A] = m_new

    def run_heads(heads):
        if finalize:
            qb = {h: qk_ref[0, rows, ksl[h]] for h in heads}
            kb = {h: qk_ref[0, rows, kksl[h]] for h in heads}
            qh = {h: qb[h].astype(F32) for h in heads}
            kh = {h: kb[h].astype(F32) for h in heads}
            gate = {}
            for h in heads:
                o = o_ref[0, rows, vsl[h]].astype(F32)
                z = z_ref[0, rows, vsl[h]].astype(F32)
                gate[h] = z / ((1.0 + jnp.exp(-o)) * (1.0 + jnp.exp(-z)))
        else:
            qh = {h: _conv_block(q_ref, qp_ref, qn_ref, sc, SUB_MLSTM, ksl[h], has_prev, has_next,
                              cw_ref[:, ksl[h]], cb_ref[:, ksl[h]]) * (DQK_A ** -0.5) for h in heads}
            kh = {h: _conv_block(k_ref, kp_ref, kn_ref, sc, SUB_MLSTM, ksl[h], has_prev, has_next,
                              cw_ref[:, kksl[h]], cb_ref[:, kksl[h]]) for h in heads}
            qb = {h: qh[h].astype(BF16) for h in heads}
            kb = {h: kh[h].astype(BF16) for h in heads}
            for h in heads:
                qk_ref[0, rows, ksl[h]] = qb[h]
                qk_ref[0, rows, kksl[h]] = kb[h]
        ones_l = jnp.ones((CHUNK, CHUNK), BF16)
        vb = {h: jnp.concatenate([v_ref[0, rows, vsl[h]].astype(BF16), ones_l], axis=1)
              for h in heads}
        c_prev = {h: cf_ref[0, h] for h in heads}
        mp = {h: m_prev[:, h:h + 1] for h in heads}

        s_qk = {h: _dot_nt(qb[h], kb[h]) for h in heads}
        q_c = {h: _dot(qb[h], c_prev[h].astype(BF16)) for h in heads}
        cm = {h: jnp.where(mask, c_r[h:h + 1, :], -jnp.inf) for h in heads}
        mm = {h: jnp.broadcast_to(jnp.maximum(jnp.max(cm[h], axis=1, keepdims=True), mp[h]),
                               (CHUNK, CHUNK)) for h in heads}

        kw = {h: kh[h] * w_c[:, h:h + 1] for h in heads}
        for h in heads:
            cf_ref[0, h] = dec[:, h:h + 1] * c_prev[h] + _dot_tn(kw[h].astype(BF16), vb[h])

        p = {h: s_qk[h] * jnp.exp2(cm[h] - mm[h]) for h in heads}
        p_v = {h: _dot(p[h].astype(BF16), vb[h]) for h in heads}
        w_int = {h: jnp.exp2(mp[h] - mm[h]) for h in heads}
        den = {h: w_int[h] * q_c[h][:, DV_A:] + p_v[h][:, DV_A:] for h in heads}
        b_t = {h: jnp.broadcast_to(b_c[:, h:h + 1], (CHUNK, CHUNK)) for h in heads}
        inv = {h: 1.0 / jnp.maximum(jnp.abs(den[h]), jnp.exp2(-(b_t[h] + mm[h]))) for h in heads}
        hval = {h: jnp.concatenate(
            [(w_int[h] * q_c[h][:, c * CHUNK:(c + 1) * CHUNK] + p_v[h][:, c * CHUNK:(c + 1) * CHUNK])
             * inv[h] for c in range(DV_A // CHUNK)], axis=1) for h in heads}

        if not finalize:
            for h in heads:
                out_ref[0, rows, vsl[h]] = hval[h].astype(out_ref.dtype)
            return

        hs = {h: hval[h] + hb_ref[0, rows, vsl[h]].astype(F32) for h in heads}
        mu = {h: jnp.mean(hs[h], axis=-1, keepdims=True) for h in heads}
        hc = {h: hs[h] - mu[h] for h in heads}
        var = {h: jnp.mean(hc[h] * hc[h], axis=-1, keepdims=True) for h in heads}
        for h in heads:
            hn = hc[h] * lax.rsqrt(var[h] + LN_EPS) * mhw_ref[:, vsl[h]]
            out_ref[0, rows, vsl[h]] = (hn * gate[h]).astype(out_ref.dtype)

    for g0 in range(0, NH_A, HEAD_GROUP):
        run_heads(range(g0, g0 + HEAD_GROUP))


def _mlstm_call(reverse, finalize, state, *, p3, small3, gate_b128, conv_w=None, conv_b=None,
                qk=None, hb=None, mh_w=None):
    bsz, t_len, _ = p3.shape
    nc, main, prev, nxt, const2 = _chunk_specs(t_len, reverse, SUB_MLSTM)
    c0, m0 = state
    state_specs = [pl.BlockSpec((1, NH_A, DQK_A, DV_A + CHUNK), lambda b, i: (b, 0, 0, 0)),
                   pl.BlockSpec((1, 1, 128), lambda b, i: (b, 0, 0))]
    state_shapes = [jax.ShapeDtypeStruct(c0.shape, F32), jax.ShapeDtypeStruct(m0.shape, F32)]
    wide = main(0, W_A)
    if finalize:
        in_specs = [wide, main(COL_V, W_A), main(0, N_SMALL), const2((1, N_SMALL))] + state_specs
        in_specs += [wide, main(COL_O, W_A), main(COL_ZA, W_A), const2((1, W_A))]
        args = [qk, p3, small3, gate_b128, c0, m0, hb, p3, p3, mh_w]
        out_specs = [wide] + state_specs
        out_shape = [jax.ShapeDtypeStruct((bsz, t_len, W_A), BF16)] + state_shapes
    else:
        in_specs = [main(COL_Q, CW), prev(COL_Q), nxt(COL_Q),
                    main(COL_K, CW), prev(COL_K), nxt(COL_K),
                    main(COL_V, W_A), main(0, N_SMALL),
                    const2((3, 2 * QK_A)), const2((1, 2 * QK_A)), const2((1, N_SMALL))] + state_specs
        args = [p3, p3, p3, p3, p3, p3, p3, small3, conv_w, conv_b, gate_b128, c0, m0]
        out_specs = [wide, wide] + state_specs
        out_shape = [jax.ShapeDtypeStruct((bsz, t_len, W_A), HB_DTYPE),
                     jax.ShapeDtypeStruct((bsz, t_len, 2 * QK_A), BF16)] + state_shapes
    res = pl.pallas_call(
        functools.partial(_mlstm_kernel, reverse, finalize),
        grid=(bsz, nc), in_specs=in_specs, out_specs=out_specs, out_shape=out_shape,
        compiler_params=_cparams(("arbitrary", "arbitrary")),
        name="mlstm_" + ("bwd" if reverse else "fwd"),
    )(*args)
    if finalize:
        return res[0], tuple(res[1:])
    return res[0], res[1], tuple(res[2:])


def _ssd_kernel(reverse, finalize, *refs):
    if finalize:
        (xc_ref, bcc_ref, gs_ref, dtb_ref, alog_ref, s0_ref, yb_ref, z0_ref, z1_ref, dsk_ref,
         nw_ref, out_ref, sf_ref, y_s) = refs
    else:
        (x0_ref, x0p_ref, x0n_ref, x1_ref, x1p_ref, x1n_ref, bc_ref, bcp_ref, bcn_ref,
         gs_ref, cw_ref, cb_ref, dtb_ref, alog_ref, s0_ref, out_ref, xc_ref, sf_ref) = refs

    i, j, nc = _chunk_pos(reverse)

    @pl.when(i == 0)
    def _():
        sf_ref[...] = s0_ref[...]

    has_prev = j > 0
    has_next = j < nc - 1
    for sc in (reversed(range(SUB_SSD)) if reverse else range(SUB_SSD)):
        _ssd_chunk(reverse, finalize, sc, has_prev, has_next, refs)


def _ssd_chunk(reverse, finalize, sc, has_prev, has_next, refs):
    if finalize:
        (xc_ref, bcc_ref, gs_ref, dtb_ref, alog_ref, s0_ref, yb_ref, z0_ref, z1_ref, dsk_ref,
         nw_ref, out_ref, sf_ref, y_s) = refs
    else:
        (x0_ref, x0p_ref, x0n_ref, x1_ref, x1p_ref, x1n_ref, bc_ref, bcp_ref, bcn_ref,
         gs_ref, cw_ref, cb_ref, dtb_ref, alog_ref, s0_ref, out_ref, xc_ref, sf_ref) = refs
    rows = slice(sc * CHUNK, (sc + 1) * CHUNK)
    _, tri, _ = _tri_masks(reverse)
    edge = 0 if reverse else CHUNK - 1

    t_i = lax.broadcasted_iota(jnp.int32, (CHUNK, CHUNK), 0)
    l_i = lax.broadcasted_iota(jnp.int32, (CHUNK, CHUNK), 1)
    s_lo = jnp.where(l_i < HALF, l_i, l_i - HALF)
    lo_half = l_i < HALF
    if reverse:
        mask1, mask2 = s_lo >= t_i, s_lo + HALF >= t_i
    else:
        mask1, mask2 = s_lo <= t_i, s_lo + HALF <= t_i
    k_i = lax.broadcasted_iota(jnp.int32, (2 * CHUNK, CHUNK), 0)
    kl_i = lax.broadcasted_iota(jnp.int32, (2 * CHUNK, CHUNK), 1)
    blockdiag = ((k_i // HALF) % 2 == 1) == (kl_i >= HALF)

    dt_all = _softplus(gs_ref[0, rows, :] + dtb_ref[...])
    da_all = dt_all * (-LOG2E * jnp.exp(alog_ref[...]))
    acum_all = _cumsum_rows(tri, da_all)
    do = H_B if reverse else 0
    acum_t = acum_all.T
    a_even = acum_t[do:do + NPAIR, :]
    a_odd = acum_t[do + NPAIR:do + H_B, :]
    lane16 = lax.broadcasted_iota(jnp.int32, (NPAIR, CHUNK), 1) < HALF
    rv1 = jnp.where(lane16, a_even, pltpu.roll(a_odd, HALF, 1))
    rv2 = jnp.where(lane16, pltpu.roll(a_even, HALF, 1), a_odd)

    if finalize:
        bmat = bcc_ref[0, rows, 0:GP_B]
        cmat = bcc_ref[0, rows, GP_B:2 * GP_B]
        ssq = jnp.zeros((CHUNK, CHUNK), F32)
    else:
        bmat = _conv_block(bc_ref, bcp_ref, bcn_ref, sc, SUB_SSD, slice(0, GP_B), has_prev, has_next,
                           cw_ref[:, W_B:W_B + GP_B], cb_ref[:, W_B:W_B + GP_B]).astype(BF16)
        cmat = _conv_block(bc_ref, bcp_ref, bcn_ref, sc, SUB_SSD, slice(GP_B, 2 * GP_B), has_prev, has_next,
                           cw_ref[:, W_B + GP_B:W_B + 2 * GP_B],
                           cb_ref[:, W_B + GP_B:W_B + 2 * GP_B]).astype(BF16)
        xc_ref[0, rows, W_B:W_B + GP_B] = bmat
        xc_ref[0, rows, W_B + GP_B:W_B + 2 * GP_B] = cmat

    for g in range(G_B):
        gs = slice(g * GP_B, (g + 1) * GP_B)
        if finalize:
            xg = xc_ref[0, rows, gs].astype(F32)
        else:
            xr, xpr, xnr = (x0_ref, x0p_ref, x0n_ref) if g < 2 else (x1_ref, x1p_ref, x1n_ref)
            cs = slice((g % 2) * GP_B, (g % 2 + 1) * GP_B)
            xg = _conv_block(xr, xpr, xnr, sc, SUB_SSD, cs, has_prev, has_next, cw_ref[:, gs], cb_ref[:, gs])
            xc_ref[0, rows, gs] = xg.astype(BF16)
        bg = bmat[:, g * N_B:(g + 1) * N_B]
        cg = cmat[:, g * N_B:(g + 1) * N_B]
        cb1 = _dot_nt(cg, jnp.concatenate([bg[0:HALF], bg[0:HALF]], axis=0))
        cb2 = _dot_nt(cg, jnp.concatenate([bg[HALF:], bg[HALF:]], axis=0))
        s_prev = sf_ref[0, g]
        cs_prev = _dot(cg, s_prev.astype(BF16))
        y_parts, xw_parts, tot_parts = [], [], []
        for pp in range(R_B // 2):
            pr = g * (R_B // 2) + pp
            ps = slice(pp * CHUNK, (pp + 1) * CHUNK)
            idx = jnp.where(lo_half, do + pr, do + NPAIR + pr)
            acum_b = jnp.take_along_axis(acum_all, idx, axis=1)
            dt_b = jnp.take_along_axis(dt_all, idx, axis=1)
            xdt = xg[:, ps] * dt_b
            xdtb = xdt.astype(BF16)
            w1 = cb1 * jnp.exp2(jnp.where(mask1, acum_b - rv1[pr:pr + 1, :], -jnp.inf))
            w2 = cb2 * jnp.exp2(jnp.where(mask2, acum_b - rv2[pr:pr + 1, :], -jnp.inf))
            w12 = jnp.concatenate([w1.astype(BF16), w2.astype(BF16)], axis=1)
            x12 = jnp.concatenate([xdtb[0:HALF], xdtb[0:HALF], xdtb[HALF:], xdtb[HALF:]], axis=0)
            x12 = jnp.where(blockdiag, x12, jnp.zeros_like(x12))
            tot = acum_b[edge:edge + 1, :]
            y_parts.append(_dot(w12, x12) + jnp.exp2(acum_b) * cs_prev[:, ps])
            xw_parts.append((xdt * jnp.exp2(tot - acum_b)).astype(BF16))
            tot_parts.append(tot)
        xw = jnp.concatenate(xw_parts, axis=1)
        etot = jnp.exp2(jnp.concatenate(tot_parts, axis=1))
        sf_ref[0, g] = etot * s_prev + _dot_tn(bg, xw)
        y_grp = jnp.concatenate(y_parts, axis=1)
        if finalize:
            z_ref = z0_ref if g < 2 else z1_ref
            zs = slice((g % 2) * GP_B, (g % 2 + 1) * GP_B)
            y = y_grp + yb_ref[0, rows, gs].astype(F32) + dsk_ref[:, gs] * xg
            y = y * _silu(z_ref[0, rows, zs].astype(F32))
            y_s[:, gs] = y
            y2 = y * y
            ssq = ssq + (y2[:, 0:CHUNK] + y2[:, CHUNK:2 * CHUNK]
                         + y2[:, 2 * CHUNK:3 * CHUNK] + y2[:, 3 * CHUNK:])
        else:
            out_ref[0, rows, gs] = y_grp.astype(out_ref.dtype)

    if finalize:
        inv = lax.rsqrt(jnp.sum(ssq, axis=-1, keepdims=True) * (1.0 / W_B) + LN_EPS)
        for half in range(2):
            cs = slice(half * CW, (half + 1) * CW)
            out_ref[0, rows, cs] = (y_s[:, cs] * inv * nw_ref[:, cs]).astype(out_ref.dtype)


def _ssd_call(reverse, finalize, s0, *, small3, dtb128, alog128, p3, conv_w=None, conv_b=None,
              xc=None, yb=None, d_skip=None, norm_w=None):
    bsz, t_len, _ = p3.shape
    nc, main, prev, nxt, const2 = _chunk_specs(t_len, reverse, SUB_SSD)
    state_spec = pl.BlockSpec((1, G_B, N_B, GP_B), lambda b, i: (b, 0, 0, 0))
    state_shape = jax.ShapeDtypeStruct(s0.shape, F32)
    xc_spec = main(0, W_B + 2 * GP_B)
    if finalize:
        in_specs = [main(0, W_B), main(2, CW), main(0, N_SMALL), const2((1, N_SMALL)),
                    const2((1, N_SMALL)), state_spec,
                    main(0, W_B), main(COL_ZB, CW), main(COL_ZB + 1, CW),
                    const2((1, W_B)), const2((1, W_B))]
        args = [xc, xc, small3, dtb128, alog128, s0, yb, p3, p3, d_skip, norm_w]
        out_specs = [main(0, W_B), state_spec]
        out_shape = [jax.ShapeDtypeStruct((bsz, t_len, W_B), BF16), state_shape]
        scratch = [pltpu.VMEM((CHUNK, W_B), F32)]
    else:
        in_specs = [main(COL_X, CW), prev(COL_X), nxt(COL_X),
                    main(COL_X + 1, CW), prev(COL_X + 1), nxt(COL_X + 1),
                    main(COL_BC, CW), prev(COL_BC), nxt(COL_BC),
                    main(0, N_SMALL),
                    const2((3, 3 * CW)), const2((1, 3 * CW)), const2((1, N_SMALL)),
                    const2((1, N_SMALL)), state_spec]
        args = [p3] * 9 + [small3, conv_w, conv_b, dtb128, alog128, s0]
        out_specs = [main(0, W_B), xc_spec, state_spec]
        out_shape = [jax.ShapeDtypeStruct((bsz, t_len, W_B), HB_DTYPE),
                     jax.ShapeDtypeStruct((bsz, t_len, W_B + 2 * GP_B), BF16), state_shape]
        scratch = []
    return pl.pallas_call(
        functools.partial(_ssd_kernel, reverse, finalize),
        grid=(bsz, nc), in_specs=in_specs, out_specs=out_specs, out_shape=out_shape,
        scratch_shapes=scratch,
        compiler_params=_cparams(("arbitrary", "arbitrary")),
        name="ssd_" + ("bwd" if reverse else "fwd"),
    )(*args)


def _to_colmajor(t, bsz):
    ch = t.shape[-1]
    return t.reshape(bsz, -1, GRID_W, ch).transpose(0, 2, 1, 3).reshape(-1, ch)


def _from_colmajor(t, bsz):
    ch = t.shape[-1]
    return t.reshape(bsz, GRID_W, -1, ch).transpose(0, 2, 1, 3).reshape(-1, ch)


def _mixer_scans(pa, sa, pb, sb, bsz, t_len, lp, states):
    st_mf, st_mb, st_sf, st_sb = states
    pa = pa.reshape(bsz, t_len, N_MAIN_A)
    sa = sa.reshape(bsz, t_len, N_SMALL)
    h_b, qk, st_mb = _mlstm_call(True, False, st_mb, p3=pa, small3=sa, gate_b128=lp["gate_b128"],
                                 conv_w=lp["conv_qk_w"], conv_b=lp["conv_qk_b"])
    y_a, st_mf = _mlstm_call(False, True, st_mf, p3=pa, small3=sa, gate_b128=lp["gate_b128"],
                             qk=qk, hb=h_b, mh_w=lp["mh_w"])

    pb = pb.reshape(bsz, t_len, N_MAIN_B)
    sb = sb.reshape(bsz, t_len, N_SMALL)
    y_bb, xc, st_sb = _ssd_call(True, False, st_sb, small3=sb, dtb128=lp["dtb128"],
                                alog128=lp["alog128"], p3=pb, conv_w=lp["conv_xbc_w"],
                                conv_b=lp["conv_xbc_b"])
    y_b, st_sf = _ssd_call(False, True, st_sf, small3=sb, dtb128=lp["dtb128"],
                           alog128=lp["alog128"], p3=pb, xc=xc, yb=y_bb, d_skip=lp["d_skip"],
                           norm_w=lp["ssm_w"])
    return y_a, y_b, (st_mf, st_mb, st_sf, st_sb)


def _mixer(x2d, bsz, t_len, latent, shift, scale, mod_index, tm, tm_b, w, lp, states,
           cast_layer=None):
    n_a = len(A_STARTS)
    if latent:
        pa, sa, u2d = _in_proj_call(x2d, shift, scale, w, A_STARTS, lp["w_small_a"], mod_index, tm)
        pb, sb = _proj_call(_to_colmajor(u2d, bsz), w, B_STARTS, lp["w_small_b"], tm_b, tile0=n_a)
    else:
        w_small = jnp.concatenate([lp["w_small_a"], lp["w_small_b"]], axis=0)
        p, small, _, w = _in_proj_call(x2d, shift, scale, w, A_STARTS + B_STARTS, w_small,
                                       mod_index, tm, cast_layer)
        pa, pb = p[:, :N_MAIN_A], p[:, N_MAIN_A:]
        sa, sb = small[:, :N_SMALL], small[:, N_SMALL:]
    y_a, y_b, states = _mixer_scans(pa, sa, pb, sb, bsz, t_len, lp, states)
    y_b = y_b.reshape(bsz * t_len, W_B)
    if latent:
        y_b = _from_colmajor(y_b, bsz)
    return y_a.reshape(bsz * t_len, W_A), y_b, states, w


def _zero_states(bsz):
    m_state = (jnp.zeros((bsz, NH_A, DQK_A, DV_A + CHUNK), F32), jnp.zeros((bsz, 1, 128), F32))
    s_state = jnp.zeros((bsz, G_B, N_B, GP_B), F32)
    return (m_state, m_state, s_state, s_state)


def _layer_params(l, w_t, conv_qk_w, conv_qk_b, gate_b, mh_norm_w, conv_xbc_w, conv_xbc_b,
                  dt_bias, a_log, d_skip, ssm_norm_w):
    feats = lambda a, n: lax.slice(w_t, (l, a, 0), (l + 1, a + n, w_t.shape[2]))[0]

    def pad_cols(a):
        return jnp.concatenate([a, jnp.zeros((a.shape[0], N_SMALL - a.shape[1]), a.dtype)], axis=1)

    def pad_rows(a):
        return jnp.concatenate([a, jnp.zeros((N_SMALL - a.shape[0], a.shape[1]), a.dtype)], axis=0)

    def even_odd(a):
        a4 = a.reshape(a.shape[0], 2, NPAIR, 2)
        return a4.transpose(0, 1, 3, 2).reshape(a.shape[0], 2 * H_B)

    return {
        "w_small_a": pad_rows(feats(O_GATE, 4 * NH_A)),
        "w_small_b": pad_rows(even_odd(feats(O_DT, 2 * H_B).T).T),
        "conv_qk_w": conv_qk_w[l], "conv_qk_b": conv_qk_b[l].reshape(1, -1),
        "gate_b128": pad_cols(gate_b[l].reshape(1, -1)),
        "mh_w": mh_norm_w[l].reshape(1, W_A),
        "conv_xbc_w": conv_xbc_w[l], "conv_xbc_b": conv_xbc_b[l].reshape(1, -1),
        "dtb128": pad_cols(even_odd(dt_bias[l].reshape(1, -1))),
        "alog128": pad_cols(even_odd(a_log[l].reshape(1, -1))),
        "d_skip": jnp.repeat(d_skip[l], P_B).reshape(1, W_B),
        "ssm_w": ssm_norm_w[l].reshape(1, W_B),
    }


def kernel(x, c, ctx, c_ctx, w_ada, b_ada, w_in, conv_qk_w, conv_qk_b, gate_b, mh_norm_w,
           conv_xbc_w, conv_xbc_b, dt_bias, a_log, d_skip, ssm_norm_w, w_out, ln_g, ln_b):
    bsz, t_len, d = x.shape
    ctx_len = ctx.shape[1]
    depth = w_in.shape[0]
    assert d == D_MODEL and depth == DEPTH and bsz + 1 <= 8
    assert t_len % (CHUNK * GRID_W) == 0 and ctx_len % CHUNK == 0

    cvec = jnp.concatenate([c, c_ctx[None, :]], axis=0)
    mod = _ada_call(jnp.broadcast_to(cvec[:, :, None], (bsz + 1, d, 128)), w_ada, b_ada)

    w_t = jnp.swapaxes(w_in, 1, 2)
    w_out_bf = _cast_bf16(w_out, 2048)

    x2d = x.reshape(bsz * t_len, d)
    xc2d = ctx.reshape(bsz * ctx_len, d)
    tm_in, tm_ssd, tm_out, tm_ctx_in, tm_ctx_out = 1024, 2048, 512, bsz * ctx_len, 256
    lat_in = lambda i: i // (t_len // tm_in)
    lat_out = lambda i: i // (t_len // tm_out)
    ctx_index = lambda i: bsz

    for l in range(depth):
        last = l == depth - 1
        lp = _layer_params(l, w_t, conv_qk_w, conv_qk_b, gate_b, mh_norm_w, conv_xbc_w,
                           conv_xbc_b, dt_bias, a_log, d_skip, ssm_norm_w)
        shift = mod[l, :, 0:d].reshape(8, 1, d)
        scale = mod[l, :, d:2 * d].reshape(8, 1, d)
        gate = mod[l, :, 2 * d:3 * d].reshape(8, 1, d)

        yc_a, yc_b, ctx_states, w_bf = _mixer(xc2d, bsz, ctx_len, False, shift, scale, ctx_index,
                                              tm_ctx_in, tm_ctx_in, w_t, lp, _zero_states(bsz),
                                              cast_layer=l)
        y_a, y_b, _, _ = _mixer(x2d, bsz, t_len, True, shift, scale, lat_in, tm_in, tm_ssd, w_bf,
                                lp, ctx_states)
        x2d = _out_proj_call(y_a, y_b, x2d, gate, w_out_bf, l, ln_g[l], ln_b[l], lat_out, tm_out)
        if not last:
            xc2d = _out_proj_call(yc_a, yc_b, xc2d, gate, w_out_bf, l, ln_g[l], ln_b[l],
                                  ctx_index, tm_ctx_out)
    return x2d.reshape(bsz, t_len, d)
```

```python
import functools

import jax
import jax.numpy as jnp
from jax import lax
from jax.experimental import pallas as pl
from jax.experimental.pallas import tpu as pltpu

F32 = jnp.float32
BF16 = jnp.bfloat16

D_MODEL = 2048
DEPTH = 2
GRID_W = 64
NH_A = 8
DV_A = 256
DQK_A = 128
W_A = NH_A * DV_A
QK_A = NH_A * DQK_A
P_B = 64
H_B = 32
G_B = 4
R_B = H_B // G_B
N_B = 128
W_B = H_B * P_B
GP_B = R_B * P_B
NPAIR = H_B // 2
CHUNK = 128
HALF = CHUNK // 2
SUB_MLSTM = 1
HEAD_GROUP_BWD = 4
HEAD_GROUP_FWD = 8
SUB_SSD = 2
LN_EPS = 1e-5
DEEPNORM_ALPHA = (2 * DEPTH) ** 0.25
LOG2E = 1.4426950408889634

CW = 1024
COL_Q, COL_K, COL_V, COL_O, COL_ZA = 0, 1, 2, 4, 6
N_MAIN_A = 8 * CW
COL_X, COL_BC, COL_ZB = 0, 2, 3
N_MAIN_B = 5 * CW
N_SMALL = 128
O_GATE = N_MAIN_A
O_XBC = O_GATE + 4 * NH_A
O_DT = O_XBC + W_B + 2 * G_B * N_B
O_ZB = O_DT + 2 * H_B
A_STARTS = [k * CW for k in range(N_MAIN_A // CW)]
B_STARTS = ([O_XBC + k * CW for k in range((W_B + 2 * G_B * N_B) // CW)]
            + [O_ZB + k * CW for k in range(W_B // CW)])

HALO = 16
VMEM_LIMIT = 56 * 1024 * 1024

PROJ_DTYPE = BF16
HB_DTYPE = BF16


def _cparams(sem):
    return pltpu.CompilerParams(dimension_semantics=sem, vmem_limit_bytes=VMEM_LIMIT)


def _softplus(x):
    return jnp.maximum(x, 0.0) + jnp.log1p(jnp.exp(-jnp.abs(x)))


def _log_sigmoid(x):
    return jnp.minimum(x, 0.0) - jnp.log1p(jnp.exp(-jnp.abs(x)))


def _silu(x):
    return x * jax.nn.sigmoid(x)


def _dot(a, b):
    return jnp.dot(a, b, preferred_element_type=F32)


def _dot_nt(a, b):
    return lax.dot_general(a, b, (((1,), (1,)), ((), ())), preferred_element_type=F32)


def _dot_tn(a, b):
    return lax.dot_general(a, b, (((0,), (0,)), ((), ())), preferred_element_type=F32)


def _cumsum_rows(tri, x):
    return jnp.dot(tri, x, preferred_element_type=F32, precision=lax.Precision.HIGHEST)


def _ada_kernel(c_ref, w_ref, b_ref, o_ref):
    n_rows, d, _ = c_ref.shape
    tn = w_ref.shape[2]
    o_ref[...] = jnp.zeros(o_ref.shape, F32)
    for r in range(n_rows):
        s = _silu(c_ref[r]).reshape(d // 8, 8, 128)
        for cb in range(tn // 128):
            cs = slice(cb * 128, (cb + 1) * 128)
            acc = jnp.sum(w_ref[0, :, cs].reshape(d // 8, 8, 128) * s, axis=0)
            o_ref[0, r:r + 1, cs] = jnp.sum(acc, axis=0, keepdims=True) + b_ref[0, :, cs]


def _ada_call(crep, w_ada, b_ada):
    depth, d, n3 = w_ada.shape
    n_rows = crep.shape[0]
    tn = 1024
    return pl.pallas_call(
        _ada_kernel,
        grid=(depth, n3 // tn),
        in_specs=[pl.BlockSpec((n_rows, d, 128), lambda l, j: (0, 0, 0)),
                  pl.BlockSpec((1, d, tn), lambda l, j: (l, 0, j)),
                  pl.BlockSpec((1, 1, tn), lambda l, j: (l, 0, j))],
        out_specs=pl.BlockSpec((1, 8, tn), lambda l, j: (l, 0, j)),
        out_shape=jax.ShapeDtypeStruct((depth, 8, n3), F32),
        compiler_params=_cparams(("arbitrary", "arbitrary")),
        name="adaln_mod",
    )(crep, w_ada, b_ada.reshape(depth, 1, n3))


def _cast_kernel(x_ref, o_ref):
    o_ref[...] = x_ref[...].astype(o_ref.dtype)


def _cast_bf16(w3, tr):
    depth, rows, cols = w3.shape
    spec = pl.BlockSpec((1, tr, cols), lambda l, i: (l, i, 0))
    return pl.pallas_call(
        _cast_kernel, grid=(depth, pl.cdiv(rows, tr)), in_specs=[spec], out_specs=spec,
        out_shape=jax.ShapeDtypeStruct(w3.shape, BF16),
        compiler_params=_cparams(("arbitrary", "arbitrary")),
        name="cast_bf16",
    )(w3)


def _proj_tile(cast_w, u_ref, w_ref, p_ref, wout):
    if cast_w:
        w = w_ref[0].astype(BF16)
        wout[0][...] = w
    else:
        w = w_ref[...]
    p_ref[...] = _dot_nt(u_ref[...], w).astype(p_ref.dtype)


def _in_proj_kernel(cast_w, x_ref, shift_ref, scale_ref, w_ref, ws_ref, p_ref, small_ref, u_ref,
                    *wout):
    @pl.when(pl.program_id(1) == 0)
    def _():
        x = x_ref[...]
        mu = jnp.mean(x, axis=-1, keepdims=True)
        xc = x - mu
        var = jnp.mean(xc * xc, axis=-1, keepdims=True)
        u = xc * lax.rsqrt(var + LN_EPS) * (1.0 + scale_ref[0]) + shift_ref[0]
        ub = u.astype(BF16)
        u_ref[...] = ub
        small_ref[...] = _dot_nt(ub, ws_ref[...].astype(BF16))

    _proj_tile(cast_w, u_ref, w_ref, p_ref, wout)


def _proj_kernel(cast_w, u_ref, w_ref, ws_ref, p_ref, small_ref, *wout):
    @pl.when(pl.program_id(1) == 0)
    def _():
        small_ref[...] = _dot_nt(u_ref[...], ws_ref[...].astype(BF16))

    _proj_tile(cast_w, u_ref, w_ref, p_ref, wout)


def _weight_specs(w, starts, layer, tile0, rows, tm, tn, d):
    if layer is None:
        return False, pl.BlockSpec((tn, d), lambda i, j: (tile0 + j, 0)), [], []
    assert rows == tm and tile0 == 0

    def start(j):
        row = starts[0]
        for k in range(1, len(starts)):
            row = jnp.where(j >= k, starts[k], row)
        return pl.multiple_of(row, 32)

    spec = pl.BlockSpec((pl.Element(1), pl.Element(tn), pl.Element(d)),
                        lambda i, j: (layer, start(j), 0))
    return (True, spec, [pl.BlockSpec((tn, d), lambda i, j: (j, 0))],
            [jax.ShapeDtypeStruct((len(starts) * tn, d), BF16)])


def _in_proj_call(x2d, shift, scale, w, starts, w_small, mod_index, tm, layer=None, tile0=0):
    rows, d = x2d.shape
    tn = CW
    n_main = len(starts) * tn
    n_small = w_small.shape[0]
    cast_w, w_spec, wout_specs, wout_shapes = _weight_specs(w, starts, layer, tile0, rows, tm,
                                                            tn, d)
    return pl.pallas_call(
        functools.partial(_in_proj_kernel, cast_w),
        grid=(rows // tm, n_main // tn),
        in_specs=[pl.BlockSpec((tm, d), lambda i, j: (i, 0)),
                  pl.BlockSpec((1, 1, d), lambda i, j: (mod_index(i), 0, 0)),
                  pl.BlockSpec((1, 1, d), lambda i, j: (mod_index(i), 0, 0)),
                  w_spec,
                  pl.BlockSpec((n_small, d), lambda i, j: (0, 0))],
        out_specs=[pl.BlockSpec((tm, tn), lambda i, j: (i, j)),
                   pl.BlockSpec((tm, n_small), lambda i, j: (i, 0)),
                   pl.BlockSpec((tm, d), lambda i, j: (i, 0))] + wout_specs,
        out_shape=[jax.ShapeDtypeStruct((rows, n_main), PROJ_DTYPE),
                   jax.ShapeDtypeStruct((rows, n_small), F32),
                   jax.ShapeDtypeStruct((rows, d), BF16)] + wout_shapes,
        compiler_params=_cparams(("arbitrary", "arbitrary")),
        name="in_proj",
    )(x2d, shift, scale, w, w_small)


def _proj_call(u2d, w, starts, w_small, tm, tile0=0):
    rows, d = u2d.shape
    tn = CW
    n_main = len(starts) * tn
    cast_w, w_spec, wout_specs, wout_shapes = _weight_specs(w, starts, None, tile0, rows, tm,
                                                            tn, d)
    return pl.pallas_call(
        functools.partial(_proj_kernel, cast_w),
        grid=(rows // tm, n_main // tn),
        in_specs=[pl.BlockSpec((tm, d), lambda i, j: (i, 0)),
                  w_spec,
                  pl.BlockSpec((N_SMALL, d), lambda i, j: (0, 0))],
        out_specs=[pl.BlockSpec((tm, tn), lambda i, j: (i, j)),
                   pl.BlockSpec((tm, N_SMALL), lambda i, j: (i, 0))] + wout_specs,
        out_shape=[jax.ShapeDtypeStruct((rows, n_main), PROJ_DTYPE),
                   jax.ShapeDtypeStruct((rows, N_SMALL), F32)] + wout_shapes,
        compiler_params=_cparams(("arbitrary", "arbitrary")),
        name="ssd_proj",
    )(u2d, w, w_small)


def _out_proj_kernel(ya_ref, yb_ref, x_ref, gate_ref, wa_ref, wb_ref, g_ref, b_ref, o_ref):
    acc = _dot(ya_ref[...], wa_ref[0]) + _dot(yb_ref[...], wb_ref[0])
    r = DEEPNORM_ALPHA * x_ref[...] + gate_ref[0] * acc
    mu = jnp.mean(r, axis=-1, keepdims=True)
    rc = r - mu
    var = jnp.mean(rc * rc, axis=-1, keepdims=True)
    o_ref[...] = rc * lax.rsqrt(var + LN_EPS) * g_ref[...] + b_ref[...]


def _out_proj_call(ya, yb, x2d, gate, w_out3, layer, ln_g, ln_b, mod_index, tm):
    rows, d = x2d.shape
    const = lambda i: (0, 0)
    return pl.pallas_call(
        _out_proj_kernel,
        grid=(rows // tm,),
        in_specs=[pl.BlockSpec((tm, W_A), lambda i: (i, 0)),
                  pl.BlockSpec((tm, W_B), lambda i: (i, 0)),
                  pl.BlockSpec((tm, d), lambda i: (i, 0)),
                  pl.BlockSpec((1, 1, d), lambda i: (mod_index(i), 0, 0)),
                  pl.BlockSpec((1, W_A, d), lambda i: (layer, 0, 0)),
                  pl.BlockSpec((1, W_B, d), lambda i: (layer, 1, 0)),
                  pl.BlockSpec((1, d), const),
                  pl.BlockSpec((1, d), const)],
        out_specs=pl.BlockSpec((tm, d), lambda i: (i, 0)),
        out_shape=jax.ShapeDtypeStruct((rows, d), F32),
        compiler_params=_cparams(("arbitrary",)),
        name="out_proj",
    )(ya, yb, x2d, gate, w_out3, w_out3, ln_g.reshape(1, d), ln_b.reshape(1, d))


def _conv_silu(x, prev_row, next_row, w, bias):
    rows = x.shape[0]
    row = lax.broadcasted_iota(jnp.int32, x.shape, 0)
    xm = jnp.where(row == 0, prev_row, pltpu.roll(x, 1, 0))
    xp = jnp.where(row == rows - 1, next_row, pltpu.roll(x, rows - 1, 0))
    y = w[0:1] * xm + w[1:2] * x + w[2:3] * xp + bias
    return _silu(y)


def _conv_block(x_ref, p_ref, n_ref, sc, sub, cs, has_prev, has_next, w, bias):
    r0 = sc * CHUNK
    if sc > 0:
        pr = x_ref[0, r0 - 1:r0, cs].astype(F32)
    else:
        pr = jnp.where(has_prev, p_ref[0, HALO - 1:HALO, cs].astype(F32), 0.0)
    if sc < sub - 1:
        nx = x_ref[0, r0 + CHUNK:r0 + CHUNK + 1, cs].astype(F32)
    else:
        nx = jnp.where(has_next, n_ref[0, 0:1, cs].astype(F32), 0.0)
    return _conv_silu(x_ref[0, r0:r0 + CHUNK, cs].astype(F32), pr, nx, w, bias)


def _tri_masks(reverse):
    t = lax.broadcasted_iota(jnp.int32, (CHUNK, CHUNK), 0)
    s = lax.broadcasted_iota(jnp.int32, (CHUNK, CHUNK), 1)
    mask = (s >= t) if reverse else (s <= t)
    mask_t = (t >= s) if reverse else (t <= s)
    return mask, mask.astype(F32), mask_t.astype(F32)


def _chunk_pos(reverse):
    i = pl.program_id(1)
    nc = pl.num_programs(1)
    j = (nc - 1 - i) if reverse else i
    return i, j, nc


def _chunk_specs(t_len, reverse, sub):
    blk = sub * CHUNK
    nc = t_len // blk
    nhb = t_len // HALO
    hpc = blk // HALO

    def jj(i):
        return (nc - 1 - i) if reverse else i

    def main(col, width):
        return pl.BlockSpec((1, blk, width), lambda b, i: (b, jj(i), (col * CW) // width))

    def prev(col):
        return pl.BlockSpec((1, HALO, CW), lambda b, i: (b, jnp.maximum(jj(i) * hpc - 1, 0), col))

    def nxt(col):
        return pl.BlockSpec((1, HALO, CW),
                            lambda b, i: (b, jnp.minimum((jj(i) + 1) * hpc, nhb - 1), col))

    def const2(shape):
        return pl.BlockSpec(shape, lambda b, i: (0, 0))

    return nc, main, prev, nxt, const2


def _mlstm_kernel(reverse, finalize, *refs):
    if finalize:
        (qk_ref, v_ref, gs_ref, gb_ref, c0_ref, m0_ref, hb_ref, o_ref, z_ref, mhw_ref,
         out_ref, cf_ref, mf_ref) = refs
    else:
        (q_ref, qp_ref, qn_ref, k_ref, kp_ref, kn_ref, v_ref, gs_ref, cw_ref, cb_ref, gb_ref,
         c0_ref, m0_ref, out_ref, qk_ref, cf_ref, mf_ref) = refs

    i, j, nc = _chunk_pos(reverse)

    @pl.when(i == 0)
    def _():
        cf_ref[...] = c0_ref[...]
        mf_ref[...] = m0_ref[...]

    has_prev = j > 0
    has_next = j < nc - 1
    for sc in (reversed(range(SUB_MLSTM)) if reverse else range(SUB_MLSTM)):
        _mlstm_chunk(reverse, finalize, sc, has_prev, has_next, refs)


def _mlstm_chunk(reverse, finalize, sc, has_prev, has_next, refs):
    if finalize:
        (qk_ref, v_ref, gs_ref, gb_ref, c0_ref, m0_ref, hb_ref, o_ref, z_ref, mhw_ref,
         out_ref, cf_ref, mf_ref) = refs
    else:
        (q_ref, qp_ref, qn_ref, k_ref, kp_ref, kn_ref, v_ref, gs_ref, cw_ref, cb_ref, gb_ref,
         c0_ref, m0_ref, out_ref, qk_ref, cf_ref, mf_ref) = refs
    rows = slice(sc * CHUNK, (sc + 1) * CHUNK)
    mask, tri, tri_t = _tri_masks(reverse)
    edge = 0 if reverse else CHUNK - 1

    gb = gs_ref[0, rows, :] + gb_ref[...]
    gbt = gb.T
    io = 2 * NH_A if reverse else 0
    b_all = _cumsum_rows(tri, LOG2E * _log_sigmoid(gb))
    li_c = LOG2E * gb[:, io:io + NH_A]
    b_c = b_all[:, io + NH_A:io + 2 * NH_A]
    b_r = jnp.dot(LOG2E * _log_sigmoid(gbt[io + NH_A:io + 2 * NH_A, :]), tri_t,
                  preferred_element_type=F32, precision=lax.Precision.HIGHEST)
    c_r = LOG2E * gbt[io:io + NH_A, :] - b_r

    m_prev = mf_ref[0][:, 0:NH_A]
    g_c = b_c[edge:edge + 1, :]
    a_c = g_c - b_c + li_c
    m_new = jnp.maximum(g_c + m_prev, jnp.max(a_c, axis=0, keepdims=True))
    w_c = jnp.exp2(a_c - m_new)
    dec = jnp.exp2(g_c + m_prev - m_new)

    ksl = [slice(h * DQK_A, (h + 1) * DQK_A) for h in range(NH_A)]
    vsl = [slice(h * DV_A, (h + 1) * DV_A) for h in range(NH_A)]
    kksl = [slice(QK_A + h * DQK_A, QK_A + (h + 1) * DQK_A) for h in range(NH_A)]
    mf_ref[0, :, 0:NH_A] = m_new

    def run_heads(heads):
        if finalize:
            qb = {h: qk_ref[0, rows, ksl[h]] for h in heads}
            kb = {h: qk_ref[0, rows, kksl[h]] for h in heads}
            qh = {h: qb[h].astype(F32) for h in heads}
            kh = {h: kb[h].astype(F32) for h in heads}
            gate = {}
            for h in heads:
                o = o_ref[0, rows, vsl[h]].astype(F32)
                z = z_ref[0, rows, vsl[h]].astype(F32)
                gate[h] = z / ((1.0 + jnp.exp(-o)) * (1.0 + jnp.exp(-z)))
        else:
            qh = {h: _conv_block(q_ref, qp_ref, qn_ref, sc, SUB_MLSTM, ksl[h], has_prev, has_next,
                              cw_ref[:, ksl[h]], cb_ref[:, ksl[h]]) * (DQK_A ** -0.5) for h in heads}
            kh = {h: _conv_block(k_ref, kp_ref, kn_ref, sc, SUB_MLSTM, ksl[h], has_prev, has_next,
                              cw_ref[:, kksl[h]], cb_ref[:, kksl[h]]) for h in heads}
            qb = {h: qh[h].astype(BF16) for h in heads}
            kb = {h: kh[h].astype(BF16) for h in heads}
            for h in heads:
                qk_ref[0, rows, ksl[h]] = qb[h]
                qk_ref[0, rows, kksl[h]] = kb[h]
        ones_l = jnp.ones((CHUNK, CHUNK), BF16)
        vb = {h: jnp.concatenate([v_ref[0, rows, vsl[h]].astype(BF16), ones_l], axis=1)
              for h in heads}
        c_prev = {h: cf_ref[0, h] for h in heads}
        mp = {h: m_prev[:, h:h + 1] for h in heads}

        s_qk = {h: _dot_nt(qb[h], kb[h]) for h in heads}
        q_c = {h: _dot(qb[h], c_prev[h].astype(BF16)) for h in heads}
        cm = {h: jnp.where(mask, c_r[h:h + 1, :], -jnp.inf) for h in heads}
        mm = {h: jnp.broadcast_to(jnp.maximum(jnp.max(cm[h], axis=1, keepdims=True), mp[h]),
                               (CHUNK, CHUNK)) for h in heads}

        kw = {h: kh[h] * w_c[:, h:h + 1] for h in heads}
        for h in heads:
            cf_ref[0, h] = dec[:, h:h + 1] * c_prev[h] + _dot_tn(kw[h].astype(BF16), vb[h])

        p = {h: s_qk[h] * jnp.exp2(cm[h] - mm[h]) for h in heads}
        p_v = {h: _dot(p[h].astype(BF16), vb[h]) for h in heads}
        w_int = {h: jnp.exp2(mp[h] - mm[h]) for h in heads}
        den = {h: w_int[h] * q_c[h][:, DV_A:] + p_v[h][:, DV_A:] for h in heads}
        b_t = {h: jnp.broadcast_to(b_c[:, h:h + 1], (CHUNK, CHUNK)) for h in heads}
        inv = {h: 1.0 / jnp.maximum(jnp.abs(den[h]), jnp.exp2(-(b_t[h] + mm[h]))) for h in heads}
        hval = {h: jnp.concatenate(
            [(w_int[h] * q_c[h][:, c * CHUNK:(c + 1) * CHUNK] + p_v[h][:, c * CHUNK:(c + 1) * CHUNK])
             * inv[h] for c in range(DV_A // CHUNK)], axis=1) for h in heads}

        if not finalize:
            for h in heads:
                out_ref[0, rows, vsl[h]] = hval[h].astype(out_ref.dtype)
            return

        hs = {h: hval[h] + hb_ref[0, rows, vsl[h]].astype(F32) for h in heads}
        mu = {h: jnp.mean(hs[h], axis=-1, keepdims=True) for h in heads}
        hc = {h: hs[h] - mu[h] for h in heads}
        var = {h: jnp.mean(hc[h] * hc[h], axis=-1, keepdims=True) for h in heads}
        for h in heads:
            hn = hc[h] * lax.rsqrt(var[h] + LN_EPS) * mhw_ref[:, vsl[h]]
            out_ref[0, rows, vsl[h]] = (hn * gate[h]).astype(out_ref.dtype)

    group = HEAD_GROUP_FWD if finalize else HEAD_GROUP_BWD
    for g0 in range(0, NH_A, group):
        run_heads(range(g0, g0 + group))


def _mlstm_call(reverse, finalize, state, *, p3, small3, gate_b128, conv_w=None, conv_b=None,
                qk=None, hb=None, mh_w=None):
    bsz, t_len, _ = p3.shape
    nc, main, prev, nxt, const2 = _chunk_specs(t_len, reverse, SUB_MLSTM)
    c0, m0 = state
    state_specs = [pl.BlockSpec((1, NH_A, DQK_A, DV_A + CHUNK), lambda b, i: (b, 0, 0, 0)),
                   pl.BlockSpec((1, 1, 128), lambda b, i: (b, 0, 0))]
    state_shapes = [jax.ShapeDtypeStruct(c0.shape, F32), jax.ShapeDtypeStruct(m0.shape, F32)]
    wide = main(0, W_A)
    if finalize:
        in_specs = [wide, main(COL_V, W_A), main(0, N_SMALL), const2((1, N_SMALL))] + state_specs
        in_specs += [wide, main(COL_O, W_A), main(COL_ZA, W_A), const2((1, W_A))]
        args = [qk, p3, small3, gate_b128, c0, m0, hb, p3, p3, mh_w]
        out_specs = [wide] + state_specs
        out_shape = [jax.ShapeDtypeStruct((bsz, t_len, W_A), BF16)] + state_shapes
    else:
        in_specs = [main(COL_Q, CW), prev(COL_Q), nxt(COL_Q),
                    main(COL_K, CW), prev(COL_K), nxt(COL_K),
                    main(COL_V, W_A), main(0, N_SMALL),
                    const2((3, 2 * QK_A)), const2((1, 2 * QK_A)), const2((1, N_SMALL))] + state_specs
        args = [p3, p3, p3, p3, p3, p3, p3, small3, conv_w, conv_b, gate_b128, c0, m0]
        out_specs = [wide, wide] + state_specs
        out_shape = [jax.ShapeDtypeStruct((bsz, t_len, W_A), HB_DTYPE),
                     jax.ShapeDtypeStruct((bsz, t_len, 2 * QK_A), BF16)] + state_shapes
    res = pl.pallas_call(
        functools.partial(_mlstm_kernel, reverse, finalize),
        grid=(bsz, nc), in_specs=in_specs, out_specs=out_specs, out_shape=out_shape,
        compiler_params=_cparams(("arbitrary", "arbitrary")),
        name="mlstm_" + ("bwd" if reverse else "fwd"),
    )(*args)
    if finalize:
        return res[0], tuple(res[1:])
    return res[0], res[1], tuple(res[2:])


def _ssd_kernel(reverse, finalize, *refs):
    if finalize:
        (xc_ref, bcc_ref, gs_ref, dtb_ref, alog_ref, s0_ref, yb_ref, z0_ref, z1_ref, dsk_ref,
         nw_ref, out_ref, sf_ref, y_s) = refs
    else:
        (x0_ref, x0p_ref, x0n_ref, x1_ref, x1p_ref, x1n_ref, bc_ref, bcp_ref, bcn_ref,
         gs_ref, cw_ref, cb_ref, dtb_ref, alog_ref, s0_ref, out_ref, xc_ref, sf_ref) = refs

    i, j, nc = _chunk_pos(reverse)

    @pl.when(i == 0)
    def _():
        sf_ref[...] = s0_ref[...]

    has_prev = j > 0
    has_next = j < nc - 1
    for sc in (reversed(range(SUB_SSD)) if reverse else range(SUB_SSD)):
        _ssd_chunk(reverse, finalize, sc, has_prev, has_next, refs)


def _ssd_chunk(reverse, finalize, sc, has_prev, has_next, refs):
    if finalize:
        (xc_ref, bcc_ref, gs_ref, dtb_ref, alog_ref, s0_ref, yb_ref, z0_ref, z1_ref, dsk_ref,
         nw_ref, out_ref, sf_ref, y_s) = refs
    else:
        (x0_ref, x0p_ref, x0n_ref, x1_ref, x1p_ref, x1n_ref, bc_ref, bcp_ref, bcn_ref,
         gs_ref, cw_ref, cb_ref, dtb_ref, alog_ref, s0_ref, out_ref, xc_ref, sf_ref) = refs
    rows = slice(sc * CHUNK, (sc + 1) * CHUNK)
    _, tri, _ = _tri_masks(reverse)
    edge = 0 if reverse else CHUNK - 1

    t_i = lax.broadcasted_iota(jnp.int32, (CHUNK, CHUNK), 0)
    l_i = lax.broadcasted_iota(jnp.int32, (CHUNK, CHUNK), 1)
    s_lo = jnp.where(l_i < HALF, l_i, l_i - HALF)
    lo_half = l_i < HALF
    if reverse:
        mask1, mask2 = s_lo >= t_i, s_lo + HALF >= t_i
    else:
        mask1, mask2 = s_lo <= t_i, s_lo + HALF <= t_i
    k_i = lax.broadcasted_iota(jnp.int32, (2 * CHUNK, CHUNK), 0)
    kl_i = lax.broadcasted_iota(jnp.int32, (2 * CHUNK, CHUNK), 1)
    blockdiag = ((k_i // HALF) % 2 == 1) == (kl_i >= HALF)

    dt_all = _softplus(gs_ref[0, rows, :] + dtb_ref[...])
    da_all = dt_all * (-LOG2E * jnp.exp(alog_ref[...]))
    acum_all = _cumsum_rows(tri, da_all)
    do = H_B if reverse else 0
    acum_t = acum_all.T
    a_even = acum_t[do:do + NPAIR, :]
    a_odd = acum_t[do + NPAIR:do + H_B, :]
    lane16 = lax.broadcasted_iota(jnp.int32, (NPAIR, CHUNK), 1) < HALF
    rv1 = jnp.where(lane16, a_even, pltpu.roll(a_odd, HALF, 1))
    rv2 = jnp.where(lane16, pltpu.roll(a_even, HALF, 1), a_odd)

    if finalize:
        bmat = bcc_ref[0, rows, 0:GP_B]
        cmat = bcc_ref[0, rows, GP_B:2 * GP_B]
        ssq = jnp.zeros((CHUNK, CHUNK), F32)
    else:
        bmat = _conv_block(bc_ref, bcp_ref, bcn_ref, sc, SUB_SSD, slice(0, GP_B), has_prev, has_next,
                           cw_ref[:, W_B:W_B + GP_B], cb_ref[:, W_B:W_B + GP_B]).astype(BF16)
        cmat = _conv_block(bc_ref, bcp_ref, bcn_ref, sc, SUB_SSD, slice(GP_B, 2 * GP_B), has_prev, has_next,
                           cw_ref[:, W_B + GP_B:W_B + 2 * GP_B],
                           cb_ref[:, W_B + GP_B:W_B + 2 * GP_B]).astype(BF16)
        xc_ref[0, rows, W_B:W_B + GP_B] = bmat
        xc_ref[0, rows, W_B + GP_B:W_B + 2 * GP_B] = cmat

    for g in range(G_B):
        gs = slice(g * GP_B, (g + 1) * GP_B)
        if finalize:
            xg = xc_ref[0, rows, gs].astype(F32)
        else:
            xr, xpr, xnr = (x0_ref, x0p_ref, x0n_ref) if g < 2 else (x1_ref, x1p_ref, x1n_ref)
            cs = slice((g % 2) * GP_B, (g % 2 + 1) * GP_B)
            xg = _conv_block(xr, xpr, xnr, sc, SUB_SSD, cs, has_prev, has_next, cw_ref[:, gs], cb_ref[:, gs])
            xc_ref[0, rows, gs] = xg.astype(BF16)
        bg = bmat[:, g * N_B:(g + 1) * N_B]
        cg = cmat[:, g * N_B:(g + 1) * N_B]
        cb1 = _dot_nt(cg, jnp.concatenate([bg[0:HALF], bg[0:HALF]], axis=0))
        cb2 = _dot_nt(cg, jnp.concatenate([bg[HALF:], bg[HALF:]], axis=0))
        s_prev = sf_ref[0, g]
        cs_prev = _dot(cg, s_prev.astype(BF16))
        y_parts, xw_parts, tot_parts = [], [], []
        for pp in range(R_B // 2):
            pr = g * (R_B // 2) + pp
            ps = slice(pp * CHUNK, (pp + 1) * CHUNK)
            idx = jnp.where(lo_half, do + pr, do + NPAIR + pr)
            acum_b = jnp.take_along_axis(acum_all, idx, axis=1)
            dt_b = jnp.take_along_axis(dt_all, idx, axis=1)
            xdt = xg[:, ps] * dt_b
            xdtb = xdt.astype(BF16)
            w1 = cb1 * jnp.exp2(jnp.where(mask1, acum_b - rv1[pr:pr + 1, :], -jnp.inf))
            w2 = cb2 * jnp.exp2(jnp.where(mask2, acum_b - rv2[pr:pr + 1, :], -jnp.inf))
            w12 = jnp.concatenate([w1.astype(BF16), w2.astype(BF16)], axis=1)
            x12 = jnp.concatenate([xdtb[0:HALF], xdtb[0:HALF], xdtb[HALF:], xdtb[HALF:]], axis=0)
            x12 = jnp.where(blockdiag, x12, jnp.zeros_like(x12))
            tot = acum_b[edge:edge + 1, :]
            y_parts.append(_dot(w12, x12) + jnp.exp2(acum_b) * cs_prev[:, ps])
            xw_parts.append((xdt * jnp.exp2(tot - acum_b)).astype(BF16))
            tot_parts.append(tot)
        xw = jnp.concatenate(xw_parts, axis=1)
        etot = jnp.exp2(jnp.concatenate(tot_parts, axis=1))
        sf_ref[0, g] = etot * s_prev + _dot_tn(bg, xw)
        y_grp = jnp.concatenate(y_parts, axis=1)
        if finalize:
            z_ref = z0_ref if g < 2 else z1_ref
            zs = slice((g % 2) * GP_B, (g % 2 + 1) * GP_B)
            y = y_grp + yb_ref[0, rows, gs].astype(F32) + dsk_ref[:, gs] * xg
            y = y * _silu(z_ref[0, rows, zs].astype(F32))
            y_s[:, gs] = y
            y2 = y * y
            ssq = ssq + (y2[:, 0:CHUNK] + y2[:, CHUNK:2 * CHUNK]
                         + y2[:, 2 * CHUNK:3 * CHUNK] + y2[:, 3 * CHUNK:])
        else:
            out_ref[0, rows, gs] = y_grp.astype(out_ref.dtype)

    if finalize:
        inv = lax.rsqrt(jnp.sum(ssq, axis=-1, keepdims=True) * (1.0 / W_B) + LN_EPS)
        for half in range(2):
            cs = slice(half * CW, (half + 1) * CW)
            out_ref[0, rows, cs] = (y_s[:, cs] * inv * nw_ref[:, cs]).astype(out_ref.dtype)


def _ssd_call(reverse, finalize, s0, *, small3, dtb128, alog128, p3, pcol0=0, conv_w=None,
              conv_b=None, xc=None, yb=None, d_skip=None, norm_w=None):
    bsz, t_len, _ = p3.shape
    c_x, c_bc, c_zb = COL_X + pcol0, COL_BC + pcol0, COL_ZB + pcol0
    nc, main, prev, nxt, const2 = _chunk_specs(t_len, reverse, SUB_SSD)
    state_spec = pl.BlockSpec((1, G_B, N_B, GP_B), lambda b, i: (b, 0, 0, 0))
    state_shape = jax.ShapeDtypeStruct(s0.shape, F32)
    xc_spec = main(0, W_B + 2 * GP_B)
    if finalize:
        in_specs = [main(0, W_B), main(2, CW), main(0, N_SMALL), const2((1, N_SMALL)),
                    const2((1, N_SMALL)), state_spec,
                    main(0, W_B), main(c_zb, CW), main(c_zb + 1, CW),
                    const2((1, W_B)), const2((1, W_B))]
        args = [xc, xc, small3, dtb128, alog128, s0, yb, p3, p3, d_skip, norm_w]
        out_specs = [main(0, W_B), state_spec]
        out_shape = [jax.ShapeDtypeStruct((bsz, t_len, W_B), BF16), state_shape]
        scratch = [pltpu.VMEM((CHUNK, W_B), F32)]
    else:
        in_specs = [main(c_x, CW), prev(c_x), nxt(c_x),
                    main(c_x + 1, CW), prev(c_x + 1), nxt(c_x + 1),
                    main(c_bc, CW), prev(c_bc), nxt(c_bc),
                    main(0, N_SMALL),
                    const2((3, 3 * CW)), const2((1, 3 * CW)), const2((1, N_SMALL)),
                    const2((1, N_SMALL)), state_spec]
        args = [p3] * 9 + [small3, conv_w, conv_b, dtb128, alog128, s0]
        out_specs = [main(0, W_B), xc_spec, state_spec]
        out_shape = [jax.ShapeDtypeStruct((bsz, t_len, W_B), HB_DTYPE),
                     jax.ShapeDtypeStruct((bsz, t_len, W_B + 2 * GP_B), BF16), state_shape]
        scratch = []
    return pl.pallas_call(
        functools.partial(_ssd_kernel, reverse, finalize),
        grid=(bsz, nc), in_specs=in_specs, out_specs=out_specs, out_shape=out_shape,
        scratch_shapes=scratch,
        compiler_params=_cparams(("arbitrary", "arbitrary")),
        name="ssd_" + ("bwd" if reverse else "fwd"),
    )(*args)


def _to_colmajor(t, bsz):
    ch = t.shape[-1]
    return t.reshape(bsz, -1, GRID_W, ch).transpose(0, 2, 1, 3).reshape(-1, ch)


def _from_colmajor(t, bsz):
    ch = t.shape[-1]
    return t.reshape(bsz, GRID_W, -1, ch).transpose(0, 2, 1, 3).reshape(-1, ch)


def _mixer_scans(pa, sa, pb, sb, bsz, t_len, lp, states, pcol0=0):
    st_mf, st_mb, st_sf, st_sb = states
    pa = pa.reshape(bsz, t_len, pa.shape[-1])
    sa = sa.reshape(bsz, t_len, N_SMALL)
    h_b, qk, st_mb = _mlstm_call(True, False, st_mb, p3=pa, small3=sa, gate_b128=lp["gate_b128"],
                                 conv_w=lp["conv_qk_w"], conv_b=lp["conv_qk_b"])
    y_a, st_mf = _mlstm_call(False, True, st_mf, p3=pa, small3=sa, gate_b128=lp["gate_b128"],
                             qk=qk, hb=h_b, mh_w=lp["mh_w"])

    pb = pb.reshape(bsz, t_len, pb.shape[-1])
    sb = sb.reshape(bsz, t_len, N_SMALL)
    y_bb, xc, st_sb = _ssd_call(True, False, st_sb, small3=sb, dtb128=lp["dtb128"],
                                alog128=lp["alog128"], p3=pb, pcol0=pcol0,
                                conv_w=lp["conv_xbc_w"], conv_b=lp["conv_xbc_b"])
    y_b, st_sf = _ssd_call(False, True, st_sf, small3=sb, dtb128=lp["dtb128"],
                           alog128=lp["alog128"], p3=pb, pcol0=pcol0, xc=xc, yb=y_bb,
                           d_skip=lp["d_skip"], norm_w=lp["ssm_w"])
    return y_a, y_b, (st_mf, st_mb, st_sf, st_sb)


def _mixer(x2d, bsz, t_len, latent, shift, scale, mod_index, tm, tm_b, w, lp, states,
           cast_layer=None):
    n_a = len(A_STARTS)
    if latent:
        pa, sa, u2d = _in_proj_call(x2d, shift, scale, w, A_STARTS, lp["w_small_a"], mod_index, tm)
        pb, sb = _proj_call(_to_colmajor(u2d, bsz), w, B_STARTS, lp["w_small_b"], tm_b, tile0=n_a)
        pcol0 = 0
    else:
        w_small = jnp.concatenate([lp["w_small_a"], lp["w_small_b"]], axis=0)
        p, small, _, w = _in_proj_call(x2d, shift, scale, w, A_STARTS + B_STARTS, w_small,
                                       mod_index, tm, cast_layer)
        pa, pb, pcol0 = p, p, n_a
        sa, sb = small[:, :N_SMALL], small[:, N_SMALL:]
    y_a, y_b, states = _mixer_scans(pa, sa, pb, sb, bsz, t_len, lp, states, pcol0)
    y_b = y_b.reshape(bsz * t_len, W_B)
    if latent:
        y_b = _from_colmajor(y_b, bsz)
    return y_a.reshape(bsz * t_len, W_A), y_b, states, w


def _zero_states(bsz):
    m_state = (jnp.zeros((bsz, NH_A, DQK_A, DV_A + CHUNK), F32), jnp.zeros((bsz, 1, 128), F32))
    s_state = jnp.zeros((bsz, G_B, N_B, GP_B), F32)
    return (m_state, m_state, s_state, s_state)


def _layer_params(l, w_t, conv_qk_w, conv_qk_b, gate_b, mh_norm_w, conv_xbc_w, conv_xbc_b,
                  dt_bias, a_log, d_skip, ssm_norm_w):
    feats = lambda a, n: lax.slice(w_t, (l, a, 0), (l + 1, a + n, w_t.shape[2]))[0]

    def pad_cols(a):
        return jnp.concatenate([a, jnp.zeros((a.shape[0], N_SMALL - a.shape[1]), a.dtype)], axis=1)

    def pad_rows(a):
        return jnp.concatenate([a, jnp.zeros((N_SMALL - a.shape[0], a.shape[1]), a.dtype)], axis=0)

    def even_odd(a):
        a4 = a.reshape(a.shape[0], 2, NPAIR, 2)
        return a4.transpose(0, 1, 3, 2).reshape(a.shape[0], 2 * H_B)

    return {
        "w_small_a": pad_rows(feats(O_GATE, 4 * NH_A)),
        "w_small_b": pad_rows(even_odd(feats(O_DT, 2 * H_B).T).T),
        "conv_qk_w": conv_qk_w[l], "conv_qk_b": conv_qk_b[l].reshape(1, -1),
        "gate_b128": pad_cols(gate_b[l].reshape(1, -1)),
        "mh_w": mh_norm_w[l].reshape(1, W_A),
        "conv_xbc_w": conv_xbc_w[l], "conv_xbc_b": conv_xbc_b[l].reshape(1, -1),
        "dtb128": pad_cols(even_odd(dt_bias[l].reshape(1, -1))),
        "alog128": pad_cols(even_odd(a_log[l].reshape(1, -1))),
        "d_skip": jnp.repeat(d_skip[l], P_B).reshape(1, W_B),
        "ssm_w": ssm_norm_w[l].reshape(1, W_B),
    }


def kernel(x, c, ctx, c_ctx, w_ada, b_ada, w_in, conv_qk_w, conv_qk_b, gate_b, mh_norm_w,
           conv_xbc_w, conv_xbc_b, dt_bias, a_log, d_skip, ssm_norm_w, w_out, ln_g, ln_b):
    bsz, t_len, d = x.shape
    ctx_len = ctx.shape[1]
    depth = w_in.shape[0]
    assert d == D_MODEL and depth == DEPTH and bsz + 1 <= 8
    assert t_len % (CHUNK * GRID_W) == 0 and ctx_len % CHUNK == 0

    cvec = jnp.concatenate([c, c_ctx[None, :]], axis=0)
    mod = _ada_call(jnp.broadcast_to(cvec[:, :, None], (bsz + 1, d, 128)), w_ada, b_ada)

    w_t = jnp.swapaxes(w_in, 1, 2)
    w_out_bf = _cast_bf16(w_out, 2048)

    x2d = x.reshape(bsz * t_len, d)
    xc2d = ctx.reshape(bsz * ctx_len, d)
    tm_in, tm_ssd, tm_out, tm_ctx_in, tm_ctx_out = 1024, 2048, 512, bsz * ctx_len, 256
    lat_in = lambda i: i // (t_len // tm_in)
    lat_out = lambda i: i // (t_len // tm_out)
    ctx_index = lambda i: bsz

    for l in range(depth):
        last = l == depth - 1
        lp = _layer_params(l, w_t, conv_qk_w, conv_qk_b, gate_b, mh_norm_w, conv_xbc_w,
                           conv_xbc_b, dt_bias, a_log, d_skip, ssm_norm_w)
        shift = mod[l, :, 0:d].reshape(8, 1, d)
        scale = mod[l, :, d:2 * d].reshape(8, 1, d)
        gate = mod[l, :, 2 * d:3 * d].reshape(8, 1, d)

        yc_a, yc_b, ctx_states, w_bf = _mixer(xc2d, bsz, ctx_len, False, shift, scale, ctx_index,
                                              tm_ctx_in, tm_ctx_in, w_t, lp, _zero_states(bsz),
                                              cast_layer=l)
        y_a, y_b, _, _ = _mixer(x2d, bsz, t_len, True, shift, scale, lat_in, tm_in, tm_ssd, w_bf,
                                lp, ctx_states)
        x2d = _out_proj_call(y_a, y_b, x2d, gate, w_out_bf, l, ln_g[l], ln_b[l], lat_out, tm_out)
        if not last:
            xc2d = _out_proj_call(yc_a, yc_b, xc2d, gate, w_out_bf, l, ln_g[l], ln_b[l],
                                  ctx_index, tm_ctx_out)
    return x2d.reshape(bsz, t_len, d)
```

```python
import functools
import math

import jax
import jax.numpy as jnp
from jax import lax
from jax.experimental import pallas as pl
from jax.experimental.pallas import tpu as pltpu

F32 = jnp.float32
BF16 = jnp.bfloat16

D_MODEL = 2048
DEPTH = 2
GRID_W = 64
NH_A = 8
DV_A = 256
DQK_A = 128
W_A = NH_A * DV_A
QK_A = NH_A * DQK_A
P_B = 64
H_B = 32
G_B = 4
R_B = H_B // G_B
N_B = 128
W_B = H_B * P_B
GP_B = R_B * P_B
NPAIR = H_B // 2
CHUNK = 128
HALF = CHUNK // 2
SUB_MLSTM = 1
HEAD_GROUP_BWD = 4
HEAD_GROUP_FWD = 8
SUB_SSD = 2
LN_EPS = 1e-5
DEEPNORM_ALPHA = (2 * DEPTH) ** 0.25
LOG2E = 1.4426950408889634

CW = 1024
COL_Q, COL_K, COL_V, COL_O, COL_ZA = 0, 1, 2, 4, 6
N_MAIN_A = 8 * CW
COL_X, COL_BC, COL_ZB = 0, 2, 3
N_SMALL = 128
O_GATE = N_MAIN_A
O_XBC = O_GATE + 4 * NH_A
O_DT = O_XBC + W_B + 2 * G_B * N_B
O_ZB = O_DT + 2 * H_B
A_STARTS = [k * CW for k in range(N_MAIN_A // CW)]
B_STARTS = ([O_XBC + k * CW for k in range((W_B + 2 * G_B * N_B) // CW)]
            + [O_ZB + k * CW for k in range(W_B // CW)])

HALO = 16
VMEM_LIMIT = 56 * 1024 * 1024

PROJ_DTYPE = BF16
HB_DTYPE = BF16


def _cparams(sem):
    return pltpu.CompilerParams(dimension_semantics=sem, vmem_limit_bytes=VMEM_LIMIT)


def _softplus(x):
    return jnp.maximum(x, 0.0) + jnp.log1p(jnp.exp(-jnp.abs(x)))


def _log_sigmoid(x):
    return jnp.minimum(x, 0.0) - jnp.log1p(jnp.exp(-jnp.abs(x)))


def _silu(x):
    return x * jax.nn.sigmoid(x)


def _dot(a, b):
    return jnp.dot(a, b, preferred_element_type=F32)


def _dot_nt(a, b):
    return lax.dot_general(a, b, (((1,), (1,)), ((), ())), preferred_element_type=F32)


def _dot_tn(a, b):
    return lax.dot_general(a, b, (((0,), (0,)), ((), ())), preferred_element_type=F32)


def _cumsum_rows(tri, x):
    return jnp.dot(tri, x, preferred_element_type=F32, precision=lax.Precision.HIGHEST)


def _ada_kernel(c_ref, w_ref, b_ref, o_ref):
    n_rows, d, _ = c_ref.shape
    tn = w_ref.shape[2]
    o_ref[...] = jnp.zeros(o_ref.shape, F32)
    for r in range(n_rows):
        s = _silu(c_ref[r]).reshape(d // 8, 8, 128)
        for cb in range(tn // 128):
            cs = slice(cb * 128, (cb + 1) * 128)
            acc = jnp.sum(w_ref[0, :, cs].reshape(d // 8, 8, 128) * s, axis=0)
            o_ref[0, r:r + 1, cs] = jnp.sum(acc, axis=0, keepdims=True) + b_ref[0, :, cs]


def _ada_call(crep, w_ada, b_ada):
    depth, d, n3 = w_ada.shape
    n_rows = crep.shape[0]
    tn = 1024
    return pl.pallas_call(
        _ada_kernel,
        grid=(depth, n3 // tn),
        in_specs=[pl.BlockSpec((n_rows, d, 128), lambda l, j: (0, 0, 0)),
                  pl.BlockSpec((1, d, tn), lambda l, j: (l, 0, j)),
                  pl.BlockSpec((1, 1, tn), lambda l, j: (l, 0, j))],
        out_specs=pl.BlockSpec((1, 8, tn), lambda l, j: (l, 0, j)),
        out_shape=jax.ShapeDtypeStruct((depth, 8, n3), F32),
        compiler_params=_cparams(("arbitrary", "arbitrary")),
        name="adaln_mod",
    )(crep, w_ada, b_ada.reshape(depth, 1, n3))


def _cast_kernel(x_ref, o_ref):
    o_ref[...] = x_ref[...].astype(o_ref.dtype)


def _cast_bf16(w3, tr):
    depth, rows, cols = w3.shape
    spec = pl.BlockSpec((1, tr, cols), lambda l, i: (l, i, 0))
    return pl.pallas_call(
        _cast_kernel, grid=(depth, pl.cdiv(rows, tr)), in_specs=[spec], out_specs=spec,
        out_shape=jax.ShapeDtypeStruct(w3.shape, BF16),
        compiler_params=_cparams(("arbitrary", "arbitrary")),
        name="cast_bf16",
    )(w3)


def _proj_tile(cast_w, u_ref, w_ref, p_ref, wout):
    if cast_w:
        w = w_ref[0].astype(BF16)
        wout[0][...] = w
    else:
        w = w_ref[...]
    p_ref[...] = _dot_nt(u_ref[...], w).astype(p_ref.dtype)


def _in_proj_kernel(cast_w, x_ref, shift_ref, scale_ref, w_ref, ws_ref, p_ref, small_ref, u_ref,
                    *wout):
    @pl.when(pl.program_id(1) == 0)
    def _():
        x = x_ref[...]
        mu = jnp.mean(x, axis=-1, keepdims=True)
        xc = x - mu
        var = jnp.mean(xc * xc, axis=-1, keepdims=True)
        u = xc * lax.rsqrt(var + LN_EPS) * (1.0 + scale_ref[0]) + shift_ref[0]
        ub = u.astype(BF16)
        u_ref[...] = ub
        small_ref[...] = _dot_nt(ub, ws_ref[...].astype(BF16))

    _proj_tile(cast_w, u_ref, w_ref, p_ref, wout)


def _proj_kernel(u_ref, w_ref, ws_ref, p_ref, small_ref):
    @pl.when(pl.program_id(1) == 0)
    def _():
        small_ref[...] = _dot_nt(u_ref[...], ws_ref[...].astype(BF16))

    _proj_tile(False, u_ref, w_ref, p_ref, ())


def _weight_specs(w, starts, layer, tile0, rows, tm, tn, d):
    if layer is None:
        return False, pl.BlockSpec((tn, d), lambda i, j: (tile0 + j, 0)), [], []
    assert rows == tm and tile0 == 0

    align = functools.reduce(math.gcd, starts, tn)

    def start(j):
        row = starts[0]
        for k in range(1, len(starts)):
            row = jnp.where(j >= k, starts[k], row)
        return pl.multiple_of(row, align)

    spec = pl.BlockSpec((pl.Element(1), pl.Element(tn), pl.Element(d)),
                        lambda i, j: (layer, start(j), 0))
    return (True, spec, [pl.BlockSpec((tn, d), lambda i, j: (j, 0))],
            [jax.ShapeDtypeStruct((len(starts) * tn, d), BF16)])


def _in_proj_call(x2d, shift, scale, w, starts, w_small, mod_index, tm, layer=None, tile0=0):
    rows, d = x2d.shape
    tn = CW
    n_main = len(starts) * tn
    n_small = w_small.shape[0]
    cast_w, w_spec, wout_specs, wout_shapes = _weight_specs(w, starts, layer, tile0, rows, tm,
                                                            tn, d)
    return pl.pallas_call(
        functools.partial(_in_proj_kernel, cast_w),
        grid=(rows // tm, n_main // tn),
        in_specs=[pl.BlockSpec((tm, d), lambda i, j: (i, 0)),
                  pl.BlockSpec((1, 1, d), lambda i, j: (mod_index(i), 0, 0)),
                  pl.BlockSpec((1, 1, d), lambda i, j: (mod_index(i), 0, 0)),
                  w_spec,
                  pl.BlockSpec((n_small, d), lambda i, j: (0, 0))],
        out_specs=[pl.BlockSpec((tm, tn), lambda i, j: (i, j)),
                   pl.BlockSpec((tm, n_small), lambda i, j: (i, 0)),
                   pl.BlockSpec((tm, d), lambda i, j: (i, 0))] + wout_specs,
        out_shape=[jax.ShapeDtypeStruct((rows, n_main), PROJ_DTYPE),
                   jax.ShapeDtypeStruct((rows, n_small), F32),
                   jax.ShapeDtypeStruct((rows, d), BF16)] + wout_shapes,
        compiler_params=_cparams(("arbitrary", "arbitrary")),
        name="in_proj",
    )(x2d, shift, scale, w, w_small)


def _proj_call(u2d, w, starts, w_small, tm, tile0=0):
    rows, d = u2d.shape
    tn = CW
    n_main = len(starts) * tn
    _, w_spec, _, _ = _weight_specs(w, starts, None, tile0, rows, tm, tn, d)
    return pl.pallas_call(
        _proj_kernel,
        grid=(rows // tm, n_main // tn),
        in_specs=[pl.BlockSpec((tm, d), lambda i, j: (i, 0)),
                  w_spec,
                  pl.BlockSpec((N_SMALL, d), lambda i, j: (0, 0))],
        out_specs=[pl.BlockSpec((tm, tn), lambda i, j: (i, j)),
                   pl.BlockSpec((tm, N_SMALL), lambda i, j: (i, 0))],
        out_shape=[jax.ShapeDtypeStruct((rows, n_main), PROJ_DTYPE),
                   jax.ShapeDtypeStruct((rows, N_SMALL), F32)],
        compiler_params=_cparams(("arbitrary", "arbitrary")),
        name="ssd_proj",
    )(u2d, w, w_small)


def _out_proj_kernel(ya_ref, yb_ref, x_ref, gate_ref, wa_ref, wb_ref, g_ref, b_ref, o_ref):
    acc = _dot(ya_ref[...], wa_ref[0]) + _dot(yb_ref[...], wb_ref[0])
    r = DEEPNORM_ALPHA * x_ref[...] + gate_ref[0] * acc
    mu = jnp.mean(r, axis=-1, keepdims=True)
    rc = r - mu
    var = jnp.mean(rc * rc, axis=-1, keepdims=True)
    o_ref[...] = rc * lax.rsqrt(var + LN_EPS) * g_ref[...] + b_ref[...]


def _out_proj_call(ya, yb, x2d, gate, w_out3, layer, ln_g, ln_b, mod_index, tm):
    rows, d = x2d.shape
    const = lambda i: (0, 0)
    return pl.pallas_call(
        _out_proj_kernel,
        grid=(rows // tm,),
        in_specs=[pl.BlockSpec((tm, W_A), lambda i: (i, 0)),
                  pl.BlockSpec((tm, W_B), lambda i: (i, 0)),
                  pl.BlockSpec((tm, d), lambda i: (i, 0)),
                  pl.BlockSpec((1, 1, d), lambda i: (mod_index(i), 0, 0)),
                  pl.BlockSpec((1, W_A, d), lambda i: (layer, 0, 0)),
                  pl.BlockSpec((1, W_B, d), lambda i: (layer, 1, 0)),
                  pl.BlockSpec((1, d), const),
                  pl.BlockSpec((1, d), const)],
        out_specs=pl.BlockSpec((tm, d), lambda i: (i, 0)),
        out_shape=jax.ShapeDtypeStruct((rows, d), F32),
        compiler_params=_cparams(("arbitrary",)),
        name="out_proj",
    )(ya, yb, x2d, gate, w_out3, w_out3, ln_g.reshape(1, d), ln_b.reshape(1, d))


def _conv_silu(x, prev_row, next_row, w, bias):
    rows = x.shape[0]
    row = lax.broadcasted_iota(jnp.int32, x.shape, 0)
    xm = jnp.where(row == 0, prev_row, pltpu.roll(x, 1, 0))
    xp = jnp.where(row == rows - 1, next_row, pltpu.roll(x, rows - 1, 0))
    y = w[0:1] * xm + w[1:2] * x + w[2:3] * xp + bias
    return _silu(y)


def _conv_block(x_ref, p_ref, n_ref, sc, sub, cs, has_prev, has_next, w, bias):
    r0 = sc * CHUNK
    if sc > 0:
        pr = x_ref[0, r0 - 1:r0, cs].astype(F32)
    else:
        pr = jnp.where(has_prev, p_ref[0, HALO - 1:HALO, cs].astype(F32), 0.0)
    if sc < sub - 1:
        nx = x_ref[0, r0 + CHUNK:r0 + CHUNK + 1, cs].astype(F32)
    else:
        nx = jnp.where(has_next, n_ref[0, 0:1, cs].astype(F32), 0.0)
    return _conv_silu(x_ref[0, r0:r0 + CHUNK, cs].astype(F32), pr, nx, w, bias)


def _tri_masks(reverse):
    t = lax.broadcasted_iota(jnp.int32, (CHUNK, CHUNK), 0)
    s = lax.broadcasted_iota(jnp.int32, (CHUNK, CHUNK), 1)
    mask = (s >= t) if reverse else (s <= t)
    mask_t = (t >= s) if reverse else (t <= s)
    return mask, mask.astype(F32), mask_t.astype(F32)


def _chunk_pos(reverse):
    i = pl.program_id(1)
    nc = pl.num_programs(1)
    j = (nc - 1 - i) if reverse else i
    return i, j, nc


def _chunk_specs(t_len, reverse, sub):
    blk = sub * CHUNK
    nc = t_len // blk
    nhb = t_len // HALO
    hpc = blk // HALO

    def jj(i):
        return (nc - 1 - i) if reverse else i

    def main(col, width):
        return pl.BlockSpec((1, blk, width), lambda b, i: (b, jj(i), (col * CW) // width))

    def prev(col):
        return pl.BlockSpec((1, HALO, CW), lambda b, i: (b, jnp.maximum(jj(i) * hpc - 1, 0), col))

    def nxt(col):
        return pl.BlockSpec((1, HALO, CW),
                            lambda b, i: (b, jnp.minimum((jj(i) + 1) * hpc, nhb - 1), col))

    def const2(shape):
        return pl.BlockSpec(shape, lambda b, i: (0, 0))

    return nc, main, prev, nxt, const2


def _mlstm_kernel(reverse, finalize, *refs):
    if finalize:
        (qk_ref, v_ref, gs_ref, gb_ref, c0_ref, m0_ref, hb_ref, o_ref, z_ref, mhw_ref,
         out_ref, cf_ref, mf_ref) = refs
    else:
        (q_ref, qp_ref, qn_ref, k_ref, kp_ref, kn_ref, v_ref, gs_ref, cw_ref, cb_ref, gb_ref,
         c0_ref, m0_ref, out_ref, qk_ref, cf_ref, mf_ref) = refs

    i, j, nc = _chunk_pos(reverse)

    @pl.when(i == 0)
    def _():
        cf_ref[...] = c0_ref[...]
        mf_ref[...] = m0_ref[...]

    has_prev = j > 0
    has_next = j < nc - 1
    for sc in (reversed(range(SUB_MLSTM)) if reverse else range(SUB_MLSTM)):
        _mlstm_chunk(reverse, finalize, sc, has_prev, has_next, refs)


def _mlstm_chunk(reverse, finalize, sc, has_prev, has_next, refs):
    if finalize:
        (qk_ref, v_ref, gs_ref, gb_ref, c0_ref, m0_ref, hb_ref, o_ref, z_ref, mhw_ref,
         out_ref, cf_ref, mf_ref) = refs
    else:
        (q_ref, qp_ref, qn_ref, k_ref, kp_ref, kn_ref, v_ref, gs_ref, cw_ref, cb_ref, gb_ref,
         c0_ref, m0_ref, out_ref, qk_ref, cf_ref, mf_ref) = refs
    rows = slice(sc * CHUNK, (sc + 1) * CHUNK)
    mask, tri, tri_t = _tri_masks(reverse)
    edge = 0 if reverse else CHUNK - 1

    gb = gs_ref[0, rows, :] + gb_ref[...]
    gbt = gb.T
    io = 2 * NH_A if reverse else 0
    b_all = _cumsum_rows(tri, LOG2E * _log_sigmoid(gb))
    li_c = LOG2E * gb[:, io:io + NH_A]
    b_c = b_all[:, io + NH_A:io + 2 * NH_A]
    b_r = jnp.dot(LOG2E * _log_sigmoid(gbt[io + NH_A:io + 2 * NH_A, :]), tri_t,
                  preferred_element_type=F32, precision=lax.Precision.HIGHEST)
    c_r = LOG2E * gbt[io:io + NH_A, :] - b_r

    m_prev = mf_ref[0][:, 0:NH_A]
    g_c = b_c[edge:edge + 1, :]
    a_c = g_c - b_c + li_c
    m_new = jnp.maximum(g_c + m_prev, jnp.max(a_c, axis=0, keepdims=True))
    w_c = jnp.exp2(a_c - m_new)
    dec = jnp.exp2(g_c + m_prev - m_new)

    ksl = [slice(h * DQK_A, (h + 1) * DQK_A) for h in range(NH_A)]
    vsl = [slice(h * DV_A, (h + 1) * DV_A) for h in range(NH_A)]
    kksl = [slice(QK_A + h * DQK_A, QK_A + (h + 1) * DQK_A) for h in range(NH_A)]
    mf_ref[0, :, 0:NH_A] = m_new

    def run_heads(heads):
        if finalize:
            qb = {h: qk_ref[0, rows, ksl[h]] for h in heads}
            kb = {h: qk_ref[0, rows, kksl[h]] for h in heads}
            qh = {h: qb[h].astype(F32) for h in heads}
            kh = {h: kb[h].astype(F32) for h in heads}
            gate = {}
            for h in heads:
                o = o_ref[0, rows, vsl[h]].astype(F32)
                z = z_ref[0, rows, vsl[h]].astype(F32)
                gate[h] = z / ((1.0 + jnp.exp(-o)) * (1.0 + jnp.exp(-z)))
        else:
            qh = {h: _conv_block(q_ref, qp_ref, qn_ref, sc, SUB_MLSTM, ksl[h], has_prev, has_next,
                              cw_ref[:, ksl[h]], cb_ref[:, ksl[h]]) * (DQK_A ** -0.5) for h in heads}
            kh = {h: _conv_block(k_ref, kp_ref, kn_ref, sc, SUB_MLSTM, ksl[h], has_prev, has_next,
                              cw_ref[:, kksl[h]], cb_ref[:, kksl[h]]) for h in heads}
            qb = {h: qh[h].astype(BF16) for h in heads}
            kb = {h: kh[h].astype(BF16) for h in heads}
            for h in heads:
                qk_ref[0, rows, ksl[h]] = qb[h]
                qk_ref[0, rows, kksl[h]] = kb[h]
        ones_l = jnp.ones((CHUNK, CHUNK), BF16)
        vb = {h: jnp.concatenate([v_ref[0, rows, vsl[h]].astype(BF16), ones_l], axis=1)
              for h in heads}
        c_prev = {h: cf_ref[0, h] for h in heads}
        mp = {h: m_prev[:, h:h + 1] for h in heads}

        s_qk = {h: _dot_nt(qb[h], kb[h]) for h in heads}
        q_c = {h: _dot(qb[h], c_prev[h].astype(BF16)) for h in heads}
        cm = {h: jnp.where(mask, c_r[h:h + 1, :], -jnp.inf) for h in heads}
        mm = {h: jnp.broadcast_to(jnp.maximum(jnp.max(cm[h], axis=1, keepdims=True), mp[h]),
                               (CHUNK, CHUNK)) for h in heads}

        kw = {h: kh[h] * w_c[:, h:h + 1] for h in heads}
        for h in heads:
            cf_ref[0, h] = dec[:, h:h + 1] * c_prev[h] + _dot_tn(kw[h].astype(BF16), vb[h])

        p = {h: s_qk[h] * jnp.exp2(cm[h] - mm[h]) for h in heads}
        p_v = {h: _dot(p[h].astype(BF16), vb[h]) for h in heads}
        w_int = {h: jnp.exp2(mp[h] - mm[h]) for h in heads}
        den = {h: w_int[h] * q_c[h][:, DV_A:] + p_v[h][:, DV_A:] for h in heads}
        b_t = {h: jnp.broadcast_to(b_c[:, h:h + 1], (CHUNK, CHUNK)) for h in heads}
        inv = {h: 1.0 / jnp.maximum(jnp.abs(den[h]), jnp.exp2(-(b_t[h] + mm[h]))) for h in heads}
        hval = {h: jnp.concatenate(
            [(w_int[h] * q_c[h][:, c * CHUNK:(c + 1) * CHUNK] + p_v[h][:, c * CHUNK:(c + 1) * CHUNK])
             * inv[h] for c in range(DV_A // CHUNK)], axis=1) for h in heads}

        if not finalize:
            for h in heads:
                out_ref[0, rows, vsl[h]] = hval[h].astype(out_ref.dtype)
            return

        hs = {h: hval[h] + hb_ref[0, rows, vsl[h]].astype(F32) for h in heads}
        mu = {h: jnp.mean(hs[h], axis=-1, keepdims=True) for h in heads}
        hc = {h: hs[h] - mu[h] for h in heads}
        var = {h: jnp.mean(hc[h] * hc[h], axis=-1, keepdims=True) for h in heads}
        for h in heads:
            hn = hc[h] * lax.rsqrt(var[h] + LN_EPS) * mhw_ref[:, vsl[h]]
            out_ref[0, rows, vsl[h]] = (hn * gate[h]).astype(out_ref.dtype)

    group = HEAD_GROUP_FWD if finalize else HEAD_GROUP_BWD
    for g0 in range(0, NH_A, group):
        run_heads(range(g0, g0 + group))


def _mlstm_call(reverse, finalize, state, *, p3, small3, gate_b128, conv_w=None, conv_b=None,
                qk=None, hb=None, mh_w=None):
    bsz, t_len, _ = p3.shape
    nc, main, prev, nxt, const2 = _chunk_specs(t_len, reverse, SUB_MLSTM)
    c0, m0 = state
    state_specs = [pl.BlockSpec((1, NH_A, DQK_A, DV_A + CHUNK), lambda b, i: (b, 0, 0, 0)),
                   pl.BlockSpec((1, 1, 128), lambda b, i: (b, 0, 0))]
    state_shapes = [jax.ShapeDtypeStruct(c0.shape, F32), jax.ShapeDtypeStruct(m0.shape, F32)]
    wide = main(0, W_A)
    if finalize:
        in_specs = [wide, main(COL_V, W_A), main(0, N_SMALL), const2((1, N_SMALL))] + state_specs
        in_specs += [wide, main(COL_O, W_A), main(COL_ZA, W_A), const2((1, W_A))]
        args = [qk, p3, small3, gate_b128, c0, m0, hb, p3, p3, mh_w]
        out_specs = [wide] + state_specs
        out_shape = [jax.ShapeDtypeStruct((bsz, t_len, W_A), BF16)] + state_shapes
    else:
        in_specs = [main(COL_Q, CW), prev(COL_Q), nxt(COL_Q),
                    main(COL_K, CW), prev(COL_K), nxt(COL_K),
                    main(COL_V, W_A), main(0, N_SMALL),
                    const2((3, 2 * QK_A)), const2((1, 2 * QK_A)), const2((1, N_SMALL))] + state_specs
        args = [p3, p3, p3, p3, p3, p3, p3, small3, conv_w, conv_b, gate_b128, c0, m0]
        out_specs = [wide, wide] + state_specs
        out_shape = [jax.ShapeDtypeStruct((bsz, t_len, W_A), HB_DTYPE),
                     jax.ShapeDtypeStruct((bsz, t_len, 2 * QK_A), BF16)] + state_shapes
    res = pl.pallas_call(
        functools.partial(_mlstm_kernel, reverse, finalize),
        grid=(bsz, nc), in_specs=in_specs, out_specs=out_specs, out_shape=out_shape,
        compiler_params=_cparams(("arbitrary", "arbitrary")),
        name="mlstm_" + ("bwd" if reverse else "fwd"),
    )(*args)
    if finalize:
        return res[0], tuple(res[1:])
    return res[0], res[1], tuple(res[2:])


def _ssd_kernel(reverse, finalize, *refs):
    if finalize:
        (xc_ref, bcc_ref, gs_ref, dtb_ref, alog_ref, s0_ref, yb_ref, z0_ref, z1_ref, dsk_ref,
         nw_ref, out_ref, sf_ref, y_s) = refs
    else:
        (x0_ref, x0p_ref, x0n_ref, x1_ref, x1p_ref, x1n_ref, bc_ref, bcp_ref, bcn_ref,
         gs_ref, cw_ref, cb_ref, dtb_ref, alog_ref, s0_ref, out_ref, xc_ref, sf_ref) = refs

    i, j, nc = _chunk_pos(reverse)

    @pl.when(i == 0)
    def _():
        sf_ref[...] = s0_ref[...]

    has_prev = j > 0
    has_next = j < nc - 1
    for sc in (reversed(range(SUB_SSD)) if reverse else range(SUB_SSD)):
        _ssd_chunk(reverse, finalize, sc, has_prev, has_next, refs)


def _ssd_chunk(reverse, finalize, sc, has_prev, has_next, refs):
    if finalize:
        (xc_ref, bcc_ref, gs_ref, dtb_ref, alog_ref, s0_ref, yb_ref, z0_ref, z1_ref, dsk_ref,
         nw_ref, out_ref, sf_ref, y_s) = refs
    else:
        (x0_ref, x0p_ref, x0n_ref, x1_ref, x1p_ref, x1n_ref, bc_ref, bcp_ref, bcn_ref,
         gs_ref, cw_ref, cb_ref, dtb_ref, alog_ref, s0_ref, out_ref, xc_ref, sf_ref) = refs
    rows = slice(sc * CHUNK, (sc + 1) * CHUNK)
    _, tri, _ = _tri_masks(reverse)
    edge = 0 if reverse else CHUNK - 1

    t_i = lax.broadcasted_iota(jnp.int32, (CHUNK, CHUNK), 0)
    l_i = lax.broadcasted_iota(jnp.int32, (CHUNK, CHUNK), 1)
    s_lo = jnp.where(l_i < HALF, l_i, l_i - HALF)
    lo_half = l_i < HALF
    if reverse:
        mask1, mask2 = s_lo >= t_i, s_lo + HALF >= t_i
    else:
        mask1, mask2 = s_lo <= t_i, s_lo + HALF <= t_i
    k_i = lax.broadcasted_iota(jnp.int32, (2 * CHUNK, CHUNK), 0)
    kl_i = lax.broadcasted_iota(jnp.int32, (2 * CHUNK, CHUNK), 1)
    blockdiag = ((k_i // HALF) % 2 == 1) == (kl_i >= HALF)

    dt_all = _softplus(gs_ref[0, rows, :] + dtb_ref[...])
    da_all = dt_all * (-LOG2E * jnp.exp(alog_ref[...]))
    acum_all = _cumsum_rows(tri, da_all)
    do = H_B if reverse else 0
    acum_t = acum_all.T
    a_even = acum_t[do:do + NPAIR, :]
    a_odd = acum_t[do + NPAIR:do + H_B, :]
    lane16 = lax.broadcasted_iota(jnp.int32, (NPAIR, CHUNK), 1) < HALF
    rv1 = jnp.where(lane16, a_even, pltpu.roll(a_odd, HALF, 1))
    rv2 = jnp.where(lane16, pltpu.roll(a_even, HALF, 1), a_odd)

    if finalize:
        bmat = bcc_ref[0, rows, 0:GP_B]
        cmat = bcc_ref[0, rows, GP_B:2 * GP_B]
        ssq = jnp.zeros((CHUNK, CHUNK), F32)
    else:
        bmat = _conv_block(bc_ref, bcp_ref, bcn_ref, sc, SUB_SSD, slice(0, GP_B), has_prev, has_next,
                           cw_ref[:, W_B:W_B + GP_B], cb_ref[:, W_B:W_B + GP_B]).astype(BF16)
        cmat = _conv_block(bc_ref, bcp_ref, bcn_ref, sc, SUB_SSD, slice(GP_B, 2 * GP_B), has_prev, has_next,
                           cw_ref[:, W_B + GP_B:W_B + 2 * GP_B],
                           cb_ref[:, W_B + GP_B:W_B + 2 * GP_B]).astype(BF16)
        xc_ref[0, rows, W_B:W_B + GP_B] = bmat
        xc_ref[0, rows, W_B + GP_B:W_B + 2 * GP_B] = cmat

    for g in range(G_B):
        gs = slice(g * GP_B, (g + 1) * GP_B)
        if finalize:
            xg = xc_ref[0, rows, gs].astype(F32)
        else:
            xr, xpr, xnr = (x0_ref, x0p_ref, x0n_ref) if g < 2 else (x1_ref, x1p_ref, x1n_ref)
            cs = slice((g % 2) * GP_B, (g % 2 + 1) * GP_B)
            xg = _conv_block(xr, xpr, xnr, sc, SUB_SSD, cs, has_prev, has_next, cw_ref[:, gs], cb_ref[:, gs])
            xc_ref[0, rows, gs] = xg.astype(BF16)
        bg = bmat[:, g * N_B:(g + 1) * N_B]
        cg = cmat[:, g * N_B:(g + 1) * N_B]
        cb1 = _dot_nt(cg, jnp.concatenate([bg[0:HALF], bg[0:HALF]], axis=0))
        cb2 = _dot_nt(cg, jnp.concatenate([bg[HALF:], bg[HALF:]], axis=0))
        s_prev = sf_ref[0, g]
        cs_prev = _dot(cg, s_prev.astype(BF16))
        y_parts, xw_parts, tot_parts = [], [], []
        for pp in range(R_B // 2):
            pr = g * (R_B // 2) + pp
            ps = slice(pp * CHUNK, (pp + 1) * CHUNK)
            idx = jnp.where(lo_half, do + pr, do + NPAIR + pr)
            acum_b = jnp.take_along_axis(acum_all, idx, axis=1)
            dt_b = jnp.take_along_axis(dt_all, idx, axis=1)
            xdt = xg[:, ps] * dt_b
            xdtb = xdt.astype(BF16)
            w1 = cb1 * jnp.exp2(jnp.where(mask1, acum_b - rv1[pr:pr + 1, :], -jnp.inf))
            w2 = cb2 * jnp.exp2(jnp.where(mask2, acum_b - rv2[pr:pr + 1, :], -jnp.inf))
            w12 = jnp.concatenate([w1.astype(BF16), w2.astype(BF16)], axis=1)
            x12 = jnp.concatenate([xdtb[0:HALF], xdtb[0:HALF], xdtb[HALF:], xdtb[HALF:]], axis=0)
            x12 = jnp.where(blockdiag, x12, jnp.zeros_like(x12))
            tot = acum_b[edge:edge + 1, :]
            y_parts.append(_dot(w12, x12) + jnp.exp2(acum_b) * cs_prev[:, ps])
            xw_parts.append((xdt * jnp.exp2(tot - acum_b)).astype(BF16))
            tot_parts.append(tot)
        xw = jnp.concatenate(xw_parts, axis=1)
        etot = jnp.exp2(jnp.concatenate(tot_parts, axis=1))
        sf_ref[0, g] = etot * s_prev + _dot_tn(bg, xw)
        y_grp = jnp.concatenate(y_parts, axis=1)
        if finalize:
            z_ref = z0_ref if g < 2 else z1_ref
            zs = slice((g % 2) * GP_B, (g % 2 + 1) * GP_B)
            y = y_grp + yb_ref[0, rows, gs].astype(F32) + dsk_ref[:, gs] * xg
            y = y * _silu(z_ref[0, rows, zs].astype(F32))
            y_s[:, gs] = y
            y2 = y * y
            ssq = ssq + (y2[:, 0:CHUNK] + y2[:, CHUNK:2 * CHUNK]
                         + y2[:, 2 * CHUNK:3 * CHUNK] + y2[:, 3 * CHUNK:])
        else:
            out_ref[0, rows, gs] = y_grp.astype(out_ref.dtype)

    if finalize:
        inv = lax.rsqrt(jnp.sum(ssq, axis=-1, keepdims=True) * (1.0 / W_B) + LN_EPS)
        for half in range(2):
            cs = slice(half * CW, (half + 1) * CW)
            out_ref[0, rows, cs] = (y_s[:, cs] * inv * nw_ref[:, cs]).astype(out_ref.dtype)


def _ssd_call(reverse, finalize, s0, *, small3, dtb128, alog128, p3, pcol0=0, conv_w=None,
              conv_b=None, xc=None, yb=None, d_skip=None, norm_w=None):
    bsz, t_len, _ = p3.shape
    c_x, c_bc, c_zb = COL_X + pcol0, COL_BC + pcol0, COL_ZB + pcol0
    nc, main, prev, nxt, const2 = _chunk_specs(t_len, reverse, SUB_SSD)
    state_spec = pl.BlockSpec((1, G_B, N_B, GP_B), lambda b, i: (b, 0, 0, 0))
    state_shape = jax.ShapeDtypeStruct(s0.shape, F32)
    xc_spec = main(0, W_B + 2 * GP_B)
    if finalize:
        in_specs = [main(0, W_B), main(2, CW), main(0, N_SMALL), const2((1, N_SMALL)),
                    const2((1, N_SMALL)), state_spec,
                    main(0, W_B), main(c_zb, CW), main(c_zb + 1, CW),
                    const2((1, W_B)), const2((1, W_B))]
        args = [xc, xc, small3, dtb128, alog128, s0, yb, p3, p3, d_skip, norm_w]
        out_specs = [main(0, W_B), state_spec]
        out_shape = [jax.ShapeDtypeStruct((bsz, t_len, W_B), BF16), state_shape]
        scratch = [pltpu.VMEM((CHUNK, W_B), F32)]
    else:
        in_specs = [main(c_x, CW), prev(c_x), nxt(c_x),
                    main(c_x + 1, CW), prev(c_x + 1), nxt(c_x + 1),
                    main(c_bc, CW), prev(c_bc), nxt(c_bc),
                    main(0, N_SMALL),
                    const2((3, 3 * CW)), const2((1, 3 * CW)), const2((1, N_SMALL)),
                    const2((1, N_SMALL)), state_spec]
        args = [p3] * 9 + [small3, conv_w, conv_b, dtb128, alog128, s0]
        out_specs = [main(0, W_B), xc_spec, state_spec]
        out_shape = [jax.ShapeDtypeStruct((bsz, t_len, W_B), HB_DTYPE),
                     jax.ShapeDtypeStruct((bsz, t_len, W_B + 2 * GP_B), BF16), state_shape]
        scratch = []
    return pl.pallas_call(
        functools.partial(_ssd_kernel, reverse, finalize),
        grid=(bsz, nc), in_specs=in_specs, out_specs=out_specs, out_shape=out_shape,
        scratch_shapes=scratch,
        compiler_params=_cparams(("arbitrary", "arbitrary")),
        name="ssd_" + ("bwd" if reverse else "fwd"),
    )(*args)


def _to_colmajor(t, bsz):
    ch = t.shape[-1]
    return t.reshape(bsz, -1, GRID_W, ch).transpose(0, 2, 1, 3).reshape(-1, ch)


def _from_colmajor(t, bsz):
    ch = t.shape[-1]
    return t.reshape(bsz, GRID_W, -1, ch).transpose(0, 2, 1, 3).reshape(-1, ch)


def _mixer_scans(pa, sa, pb, sb, bsz, t_len, lp, states, pcol0=0):
    st_mf, st_mb, st_sf, st_sb = states
    pa = pa.reshape(bsz, t_len, pa.shape[-1])
    sa = sa.reshape(bsz, t_len, N_SMALL)
    h_b, qk, st_mb = _mlstm_call(True, False, st_mb, p3=pa, small3=sa, gate_b128=lp["gate_b128"],
                                 conv_w=lp["conv_qk_w"], conv_b=lp["conv_qk_b"])
    y_a, st_mf = _mlstm_call(False, True, st_mf, p3=pa, small3=sa, gate_b128=lp["gate_b128"],
                             qk=qk, hb=h_b, mh_w=lp["mh_w"])

    pb = pb.reshape(bsz, t_len, pb.shape[-1])
    sb = sb.reshape(bsz, t_len, N_SMALL)
    y_bb, xc, st_sb = _ssd_call(True, False, st_sb, small3=sb, dtb128=lp["dtb128"],
                                alog128=lp["alog128"], p3=pb, pcol0=pcol0,
                                conv_w=lp["conv_xbc_w"], conv_b=lp["conv_xbc_b"])
    y_b, st_sf = _ssd_call(False, True, st_sf, small3=sb, dtb128=lp["dtb128"],
                           alog128=lp["alog128"], p3=pb, pcol0=pcol0, xc=xc, yb=y_bb,
                           d_skip=lp["d_skip"], norm_w=lp["ssm_w"])
    return y_a, y_b, (st_mf, st_mb, st_sf, st_sb)


def _mixer(x2d, bsz, t_len, latent, shift, scale, mod_index, tm, tm_b, w, lp, states,
           cast_layer=None):
    n_a = len(A_STARTS)
    if latent:
        pa, sa, u2d = _in_proj_call(x2d, shift, scale, w, A_STARTS, lp["w_small_a"], mod_index, tm)
        pb, sb = _proj_call(_to_colmajor(u2d, bsz), w, B_STARTS, lp["w_small_b"], tm_b, tile0=n_a)
        pcol0 = 0
    else:
        w_small = jnp.concatenate([lp["w_small_a"], lp["w_small_b"]], axis=0)
        p, small, _, w = _in_proj_call(x2d, shift, scale, w, A_STARTS + B_STARTS, w_small,
                                       mod_index, tm, cast_layer)
        pa, pb, pcol0 = p, p, n_a
        sa, sb = small[:, :N_SMALL], small[:, N_SMALL:]
    y_a, y_b, states = _mixer_scans(pa, sa, pb, sb, bsz, t_len, lp, states, pcol0)
    y_b = y_b.reshape(bsz * t_len, W_B)
    if latent:
        y_b = _from_colmajor(y_b, bsz)
    return y_a.reshape(bsz * t_len, W_A), y_b, states, w


def _zero_states(bsz):
    m_state = (jnp.zeros((bsz, NH_A, DQK_A, DV_A + CHUNK), F32), jnp.zeros((bsz, 1, 128), F32))
    s_state = jnp.zeros((bsz, G_B, N_B, GP_B), F32)
    return (m_state, m_state, s_state, s_state)


def _layer_params(l, w_t, conv_qk_w, conv_qk_b, gate_b, mh_norm_w, conv_xbc_w, conv_xbc_b,
                  dt_bias, a_log, d_skip, ssm_norm_w):
    feats = lambda a, n: lax.slice(w_t, (l, a, 0), (l + 1, a + n, w_t.shape[2]))[0]

    def pad_cols(a):
        return jnp.concatenate([a, jnp.zeros((a.shape[0], N_SMALL - a.shape[1]), a.dtype)], axis=1)

    def pad_rows(a):
        return jnp.concatenate([a, jnp.zeros((N_SMALL - a.shape[0], a.shape[1]), a.dtype)], axis=0)

    def even_odd(a):
        a4 = a.reshape(a.shape[0], 2, NPAIR, 2)
        return a4.transpose(0, 1, 3, 2).reshape(a.shape[0], 2 * H_B)

    return {
        "w_small_a": pad_rows(feats(O_GATE, 4 * NH_A)),
        "w_small_b": pad_rows(even_odd(feats(O_DT, 2 * H_B).T).T),
        "conv_qk_w": conv_qk_w[l], "conv_qk_b": conv_qk_b[l].reshape(1, -1),
        "gate_b128": pad_cols(gate_b[l].reshape(1, -1)),
        "mh_w": mh_norm_w[l].reshape(1, W_A),
        "conv_xbc_w": conv_xbc_w[l], "conv_xbc_b": conv_xbc_b[l].reshape(1, -1),
        "dtb128": pad_cols(even_odd(dt_bias[l].reshape(1, -1))),
        "alog128": pad_cols(even_odd(a_log[l].reshape(1, -1))),
        "d_skip": jnp.repeat(d_skip[l], P_B).reshape(1, W_B),
        "ssm_w": ssm_norm_w[l].reshape(1, W_B),
    }


def kernel(x, c, ctx, c_ctx, w_ada, b_ada, w_in, conv_qk_w, conv_qk_b, gate_b, mh_norm_w,
           conv_xbc_w, conv_xbc_b, dt_bias, a_log, d_skip, ssm_norm_w, w_out, ln_g, ln_b):
    bsz, t_len, d = x.shape
    ctx_len = ctx.shape[1]
    depth = w_in.shape[0]
    assert d == D_MODEL and depth == DEPTH and bsz + 1 <= 8
    assert t_len % (CHUNK * GRID_W) == 0 and ctx_len % CHUNK == 0

    cvec = jnp.concatenate([c, c_ctx[None, :]], axis=0)
    mod = _ada_call(jnp.broadcast_to(cvec[:, :, None], (bsz + 1, d, 128)), w_ada, b_ada)

    w_t = jnp.swapaxes(w_in, 1, 2)
    w_out_bf = _cast_bf16(w_out, 2048)

    x2d = x.reshape(bsz * t_len, d)
    xc2d = ctx.reshape(bsz * ctx_len, d)
    tm_in, tm_ssd, tm_out, tm_ctx_in, tm_ctx_out = 1024, 2048, 512, bsz * ctx_len, 256
    lat_in = lambda i: i // (t_len // tm_in)
    lat_out = lambda i: i // (t_len // tm_out)
    ctx_index = lambda i: bsz

    for l in range(depth):
        last = l == depth - 1
        lp = _layer_params(l, w_t, conv_qk_w, conv_qk_b, gate_b, mh_norm_w, conv_xbc_w,
                           conv_xbc_b, dt_bias, a_log, d_skip, ssm_norm_w)
        shift = mod[l, :, 0:d].reshape(8, 1, d)
        scale = mod[l, :, d:2 * d].reshape(8, 1, d)
        gate = mod[l, :, 2 * d:3 * d].reshape(8, 1, d)

        yc_a, yc_b, ctx_states, w_bf = _mixer(xc2d, bsz, ctx_len, False, shift, scale, ctx_index,
                                              tm_ctx_in, tm_ctx_in, w_t, lp, _zero_states(bsz),
                                              cast_layer=l)
        y_a, y_b, _, _ = _mixer(x2d, bsz, t_len, True, shift, scale, lat_in, tm_in, tm_ssd, w_bf,
                                lp, ctx_states)
        x2d = _out_proj_call(y_a, y_b, x2d, gate, w_out_bf, l, ln_g[l], ln_b[l], lat_out, tm_out)
        if not last:
            xc2d = _out_proj_call(yc_a, yc_b, xc2d, gate, w_out_bf, l, ln_g[l], ln_b[l],
                                  ctx_index, tm_ctx_out)
    return x2d.reshape(bsz, t_len, d)
```

```python
import functools
import math

import jax
import jax.numpy as jnp
from jax import lax
from jax.experimental import pallas as pl
from jax.experimental.pallas import tpu as pltpu

F32 = jnp.float32
BF16 = jnp.bfloat16

D_MODEL = 2048
DEPTH = 2
GRID_W = 64
NH_A = 8
DV_A = 256
DQK_A = 128
W_A = NH_A * DV_A
QK_A = NH_A * DQK_A
P_B = 64
H_B = 32
G_B = 4
R_B = H_B // G_B
N_B = 128
W_B = H_B * P_B
GP_B = R_B * P_B
NPAIR = H_B // 2
CHUNK = 128
HALF = CHUNK // 2
SUB_MLSTM = 1
HEAD_GROUP_BWD = 4
HEAD_GROUP_FWD = 8
SUB_SSD = 2
LN_EPS = 1e-5
DEEPNORM_ALPHA = (2 * DEPTH) ** 0.25
LOG2E = 1.4426950408889634

CW = 1024
COL_Q, COL_K, COL_V, COL_O, COL_ZA = 0, 1, 2, 4, 6
N_MAIN_A = 8 * CW
COL_X, COL_BC, COL_ZB = 0, 2, 3
N_SMALL = 128
O_GATE = N_MAIN_A
O_XBC = O_GATE + 4 * NH_A
O_DT = O_XBC + W_B + 2 * G_B * N_B
O_ZB = O_DT + 2 * H_B
A_STARTS = [k * CW for k in range(N_MAIN_A // CW)]
B_STARTS = ([O_XBC + k * CW for k in range((W_B + 2 * G_B * N_B) // CW)]
            + [O_ZB + k * CW for k in range(W_B // CW)])

HALO = 16
VMEM_LIMIT = 62 * 1024 * 1024

PROJ_DTYPE = BF16
HB_DTYPE = BF16


def _cparams(sem):
    return pltpu.CompilerParams(dimension_semantics=sem, vmem_limit_bytes=VMEM_LIMIT)


def _softplus(x):
    return jnp.maximum(x, 0.0) + jnp.log1p(jnp.exp(-jnp.abs(x)))


def _log_sigmoid(x):
    return jnp.minimum(x, 0.0) - jnp.log1p(jnp.exp(-jnp.abs(x)))


def _silu(x):
    return x * jax.nn.sigmoid(x)


def _dot(a, b):
    return jnp.dot(a, b, preferred_element_type=F32)


def _dot_nt(a, b):
    return lax.dot_general(a, b, (((1,), (1,)), ((), ())), preferred_element_type=F32)


def _dot_tn(a, b):
    return lax.dot_general(a, b, (((0,), (0,)), ((), ())), preferred_element_type=F32)


def _cumsum_rows(tri, x):
    return jnp.dot(tri, x, preferred_element_type=F32, precision=lax.Precision.HIGHEST)


def _ada_kernel(c_ref, w_ref, b_ref, o_ref):
    n_rows, d, _ = c_ref.shape
    tn = w_ref.shape[2]
    o_ref[...] = jnp.zeros(o_ref.shape, F32)
    for r in range(n_rows):
        s = _silu(c_ref[r]).reshape(d // 8, 8, 128)
        for cb in range(tn // 128):
            cs = slice(cb * 128, (cb + 1) * 128)
            acc = jnp.sum(w_ref[0, :, cs].reshape(d // 8, 8, 128) * s, axis=0)
            o_ref[0, r:r + 1, cs] = jnp.sum(acc, axis=0, keepdims=True) + b_ref[0, :, cs]


def _ada_call(crep, w_ada, b_ada):
    depth, d, n3 = w_ada.shape
    n_rows = crep.shape[0]
    tn = 1024
    return pl.pallas_call(
        _ada_kernel,
        grid=(depth, n3 // tn),
        in_specs=[pl.BlockSpec((n_rows, d, 128), lambda l, j: (0, 0, 0)),
                  pl.BlockSpec((1, d, tn), lambda l, j: (l, 0, j)),
                  pl.BlockSpec((1, 1, tn), lambda l, j: (l, 0, j))],
        out_specs=pl.BlockSpec((1, 8, tn), lambda l, j: (l, 0, j)),
        out_shape=jax.ShapeDtypeStruct((depth, 8, n3), F32),
        compiler_params=_cparams(("arbitrary", "arbitrary")),
        name="adaln_mod",
    )(crep, w_ada, b_ada.reshape(depth, 1, n3))


def _cast_kernel(x_ref, o_ref):
    o_ref[...] = x_ref[...].astype(o_ref.dtype)


def _cast_bf16(w3, tr):
    depth, rows, cols = w3.shape
    spec = pl.BlockSpec((1, tr, cols), lambda l, i: (l, i, 0))
    return pl.pallas_call(
        _cast_kernel, grid=(depth, pl.cdiv(rows, tr)), in_specs=[spec], out_specs=spec,
        out_shape=jax.ShapeDtypeStruct(w3.shape, BF16),
        compiler_params=_cparams(("arbitrary", "arbitrary")),
        name="cast_bf16",
    )(w3)


def _proj_tile(cast_w, u_ref, w_ref, p_ref, wout):
    if cast_w:
        w = w_ref[0].astype(BF16)
        wout[0][...] = w
    else:
        w = w_ref[...]
    p_ref[...] = _dot_nt(u_ref[...], w).astype(p_ref.dtype)


def _in_proj_kernel(cast_w, x_ref, shift_ref, scale_ref, w_ref, ws_ref, p_ref, small_ref, u_ref,
                    *wout):
    @pl.when(pl.program_id(1) == 0)
    def _():
        x = x_ref[...]
        mu = jnp.mean(x, axis=-1, keepdims=True)
        xc = x - mu
        var = jnp.mean(xc * xc, axis=-1, keepdims=True)
        u = xc * lax.rsqrt(var + LN_EPS) * (1.0 + scale_ref[0]) + shift_ref[0]
        ub = u.astype(BF16)
        u_ref[...] = ub
        small_ref[...] = _dot_nt(ub, ws_ref[...].astype(BF16))

    _proj_tile(cast_w, u_ref, w_ref, p_ref, wout)


def _proj_kernel(u_ref, w_ref, ws_ref, p_ref, small_ref):
    @pl.when(pl.program_id(1) == 0)
    def _():
        small_ref[...] = _dot_nt(u_ref[...], ws_ref[...].astype(BF16))

    _proj_tile(False, u_ref, w_ref, p_ref, ())


def _weight_specs(w, starts, layer, tile0, rows, tm, tn, d):
    if layer is None:
        return False, pl.BlockSpec((tn, d), lambda i, j: (tile0 + j, 0)), [], []
    assert rows == tm and tile0 == 0

    align = functools.reduce(math.gcd, starts, tn)

    def start(j):
        row = starts[0]
        for k in range(1, len(starts)):
            row = jnp.where(j >= k, starts[k], row)
        return pl.multiple_of(row, align)

    spec = pl.BlockSpec((pl.Element(1), pl.Element(tn), pl.Element(d)),
                        lambda i, j: (layer, start(j), 0))
    return (True, spec, [pl.BlockSpec((tn, d), lambda i, j: (j, 0))],
            [jax.ShapeDtypeStruct((len(starts) * tn, d), BF16)])


def _in_proj_call(x2d, shift, scale, w, starts, w_small, mod_index, tm, layer=None, tile0=0,
                  tn=CW):
    rows, d = x2d.shape
    n_main = len(starts) * CW
    assert layer is None or tn == CW
    n_small = w_small.shape[0]
    cast_w, w_spec, wout_specs, wout_shapes = _weight_specs(w, starts, layer, tile0, rows, tm,
                                                            tn, d)
    return pl.pallas_call(
        functools.partial(_in_proj_kernel, cast_w),
        grid=(rows // tm, n_main // tn),
        in_specs=[pl.BlockSpec((tm, d), lambda i, j: (i, 0)),
                  pl.BlockSpec((1, 1, d), lambda i, j: (mod_index(i), 0, 0)),
                  pl.BlockSpec((1, 1, d), lambda i, j: (mod_index(i), 0, 0)),
                  w_spec,
                  pl.BlockSpec((n_small, d), lambda i, j: (0, 0))],
        out_specs=[pl.BlockSpec((tm, tn), lambda i, j: (i, j)),
                   pl.BlockSpec((tm, n_small), lambda i, j: (i, 0)),
                   pl.BlockSpec((tm, d), lambda i, j: (i, 0))] + wout_specs,
        out_shape=[jax.ShapeDtypeStruct((rows, n_main), PROJ_DTYPE),
                   jax.ShapeDtypeStruct((rows, n_small), F32),
                   jax.ShapeDtypeStruct((rows, d), BF16)] + wout_shapes,
        compiler_params=_cparams(("arbitrary", "arbitrary")),
        name="in_proj",
    )(x2d, shift, scale, w, w_small)


def _proj_call(u2d, w, starts, w_small, tm, tile0=0):
    rows, d = u2d.shape
    tn = CW
    n_main = len(starts) * tn
    _, w_spec, _, _ = _weight_specs(w, starts, None, tile0, rows, tm, tn, d)
    return pl.pallas_call(
        _proj_kernel,
        grid=(rows // tm, n_main // tn),
        in_specs=[pl.BlockSpec((tm, d), lambda i, j: (i, 0)),
                  w_spec,
                  pl.BlockSpec((N_SMALL, d), lambda i, j: (0, 0))],
        out_specs=[pl.BlockSpec((tm, tn), lambda i, j: (i, j)),
                   pl.BlockSpec((tm, N_SMALL), lambda i, j: (i, 0))],
        out_shape=[jax.ShapeDtypeStruct((rows, n_main), PROJ_DTYPE),
                   jax.ShapeDtypeStruct((rows, N_SMALL), F32)],
        compiler_params=_cparams(("arbitrary", "arbitrary")),
        name="ssd_proj",
    )(u2d, w, w_small)


def _out_proj_kernel(ya_ref, yb_ref, x_ref, gate_ref, wa_ref, wb_ref, g_ref, b_ref, o_ref):
    acc = _dot(ya_ref[...], wa_ref[0]) + _dot(yb_ref[...], wb_ref[0])
    r = DEEPNORM_ALPHA * x_ref[...] + gate_ref[0] * acc
    mu = jnp.mean(r, axis=-1, keepdims=True)
    rc = r - mu
    var = jnp.mean(rc * rc, axis=-1, keepdims=True)
    o_ref[...] = rc * lax.rsqrt(var + LN_EPS) * g_ref[...] + b_ref[...]


def _out_proj_call(ya, yb, x2d, gate, w_out3, layer, ln_g, ln_b, mod_index, tm):
    rows, d = x2d.shape
    const = lambda i: (0, 0)
    return pl.pallas_call(
        _out_proj_kernel,
        grid=(rows // tm,),
        in_specs=[pl.BlockSpec((tm, W_A), lambda i: (i, 0)),
                  pl.BlockSpec((tm, W_B), lambda i: (i, 0)),
                  pl.BlockSpec((tm, d), lambda i: (i, 0)),
                  pl.BlockSpec((1, 1, d), lambda i: (mod_index(i), 0, 0)),
                  pl.BlockSpec((1, W_A, d), lambda i: (layer, 0, 0)),
                  pl.BlockSpec((1, W_B, d), lambda i: (layer, 1, 0)),
                  pl.BlockSpec((1, d), const),
                  pl.BlockSpec((1, d), const)],
        out_specs=pl.BlockSpec((tm, d), lambda i: (i, 0)),
        out_shape=jax.ShapeDtypeStruct((rows, d), F32),
        compiler_params=_cparams(("arbitrary",)),
        name="out_proj",
    )(ya, yb, x2d, gate, w_out3, w_out3, ln_g.reshape(1, d), ln_b.reshape(1, d))


def _conv_silu(x, prev_row, next_row, w, bias):
    rows = x.shape[0]
    row = lax.broadcasted_iota(jnp.int32, x.shape, 0)
    xm = jnp.where(row == 0, prev_row, pltpu.roll(x, 1, 0))
    xp = jnp.where(row == rows - 1, next_row, pltpu.roll(x, rows - 1, 0))
    y = w[0:1] * xm + w[1:2] * x + w[2:3] * xp + bias
    return _silu(y)


def _conv_block(x_ref, p_ref, n_ref, sc, sub, cs, has_prev, has_next, w, bias):
    r0 = sc * CHUNK
    if sc > 0:
        pr = x_ref[0, r0 - 1:r0, cs].astype(F32)
    else:
        pr = jnp.where(has_prev, p_ref[0, HALO - 1:HALO, cs].astype(F32), 0.0)
    if sc < sub - 1:
        nx = x_ref[0, r0 + CHUNK:r0 + CHUNK + 1, cs].astype(F32)
    else:
        nx = jnp.where(has_next, n_ref[0, 0:1, cs].astype(F32), 0.0)
    return _conv_silu(x_ref[0, r0:r0 + CHUNK, cs].astype(F32), pr, nx, w, bias)


def _tri_masks(reverse):
    t = lax.broadcasted_iota(jnp.int32, (CHUNK, CHUNK), 0)
    s = lax.broadcasted_iota(jnp.int32, (CHUNK, CHUNK), 1)
    mask = (s >= t) if reverse else (s <= t)
    mask_t = (t >= s) if reverse else (t <= s)
    return mask, mask.astype(F32), mask_t.astype(F32)


def _chunk_pos(reverse):
    i = pl.program_id(1)
    nc = pl.num_programs(1)
    j = (nc - 1 - i) if reverse else i
    return i, j, nc


def _chunk_specs(t_len, reverse, sub):
    blk = sub * CHUNK
    nc = t_len // blk
    nhb = t_len // HALO
    hpc = blk // HALO

    def jj(i):
        return (nc - 1 - i) if reverse else i

    def main(col, width):
        return pl.BlockSpec((1, blk, width), lambda b, i: (b, jj(i), (col * CW) // width))

    def prev(col):
        return pl.BlockSpec((1, HALO, CW), lambda b, i: (b, jnp.maximum(jj(i) * hpc - 1, 0), col))

    def nxt(col):
        return pl.BlockSpec((1, HALO, CW),
                            lambda b, i: (b, jnp.minimum((jj(i) + 1) * hpc, nhb - 1), col))

    def const2(shape):
        return pl.BlockSpec(shape, lambda b, i: (0, 0))

    return nc, main, prev, nxt, const2


def _mlstm_kernel(reverse, finalize, *refs):
    if finalize:
        (qk_ref, v_ref, gs_ref, gb_ref, c0_ref, m0_ref, hb_ref, o_ref, z_ref, mhw_ref,
         out_ref, cf_ref, mf_ref) = refs
    else:
        (q_ref, qp_ref, qn_ref, k_ref, kp_ref, kn_ref, v_ref, gs_ref, cw_ref, cb_ref, gb_ref,
         c0_ref, m0_ref, out_ref, qk_ref, cf_ref, mf_ref) = refs

    i, j, nc = _chunk_pos(reverse)

    @pl.when(i == 0)
    def _():
        cf_ref[...] = c0_ref[...]
        mf_ref[...] = m0_ref[...]

    has_prev = j > 0
    has_next = j < nc - 1
    for sc in (reversed(range(SUB_MLSTM)) if reverse else range(SUB_MLSTM)):
        _mlstm_chunk(reverse, finalize, sc, has_prev, has_next, refs)


def _mlstm_chunk(reverse, finalize, sc, has_prev, has_next, refs):
    if finalize:
        (qk_ref, v_ref, gs_ref, gb_ref, c0_ref, m0_ref, hb_ref, o_ref, z_ref, mhw_ref,
         out_ref, cf_ref, mf_ref) = refs
    else:
        (q_ref, qp_ref, qn_ref, k_ref, kp_ref, kn_ref, v_ref, gs_ref, cw_ref, cb_ref, gb_ref,
         c0_ref, m0_ref, out_ref, qk_ref, cf_ref, mf_ref) = refs
    rows = slice(sc * CHUNK, (sc + 1) * CHUNK)
    mask, tri, tri_t = _tri_masks(reverse)
    edge = 0 if reverse else CHUNK - 1

    gb = gs_ref[0, rows, :] + gb_ref[...]
    gbt = gb.T
    io = 2 * NH_A if reverse else 0
    b_all = _cumsum_rows(tri, LOG2E * _log_sigmoid(gb))
    li_c = LOG2E * gb[:, io:io + NH_A]
    b_c = b_all[:, io + NH_A:io + 2 * NH_A]
    b_r = jnp.dot(LOG2E * _log_sigmoid(gbt[io + NH_A:io + 2 * NH_A, :]), tri_t,
                  preferred_element_type=F32, precision=lax.Precision.HIGHEST)
    c_r = LOG2E * gbt[io:io + NH_A, :] - b_r

    m_prev = mf_ref[0][:, 0:NH_A]
    g_c = b_c[edge:edge + 1, :]
    a_c = g_c - b_c + li_c
    m_new = jnp.maximum(g_c + m_prev, jnp.max(a_c, axis=0, keepdims=True))
    w_c = jnp.exp2(a_c - m_new)
    dec = jnp.exp2(g_c + m_prev - m_new)

    ksl = [slice(h * DQK_A, (h + 1) * DQK_A) for h in range(NH_A)]
    vsl = [slice(h * DV_A, (h + 1) * DV_A) for h in range(NH_A)]
    kksl = [slice(QK_A + h * DQK_A, QK_A + (h + 1) * DQK_A) for h in range(NH_A)]
    mf_ref[0, :, 0:NH_A] = m_new

    def run_heads(heads):
        if finalize:
            qb = {h: qk_ref[0, rows, ksl[h]] for h in heads}
            kb = {h: qk_ref[0, rows, kksl[h]] for h in heads}
            qh = {h: qb[h].astype(F32) for h in heads}
            kh = {h: kb[h].astype(F32) for h in heads}
            gate = {}
            for h in heads:
                o = o_ref[0, rows, vsl[h]].astype(F32)
                z = z_ref[0, rows, vsl[h]].astype(F32)
                gate[h] = z / ((1.0 + jnp.exp(-o)) * (1.0 + jnp.exp(-z)))
        else:
            qh = {h: _conv_block(q_ref, qp_ref, qn_ref, sc, SUB_MLSTM, ksl[h], has_prev, has_next,
                              cw_ref[:, ksl[h]], cb_ref[:, ksl[h]]) * (DQK_A ** -0.5) for h in heads}
            kh = {h: _conv_block(k_ref, kp_ref, kn_ref, sc, SUB_MLSTM, ksl[h], has_prev, has_next,
                              cw_ref[:, kksl[h]], cb_ref[:, kksl[h]]) for h in heads}
            qb = {h: qh[h].astype(BF16) for h in heads}
            kb = {h: kh[h].astype(BF16) for h in heads}
            for h in heads:
                qk_ref[0, rows, ksl[h]] = qb[h]
                qk_ref[0, rows, kksl[h]] = kb[h]
        ones_l = jnp.ones((CHUNK, CHUNK), BF16)
        vb = {h: jnp.concatenate([v_ref[0, rows, vsl[h]].astype(BF16), ones_l], axis=1)
              for h in heads}
        c_prev = {h: cf_ref[0, h] for h in heads}
        mp = {h: m_prev[:, h:h + 1] for h in heads}

        s_qk = {h: _dot_nt(qb[h], kb[h]) for h in heads}
        q_c = {h: _dot(qb[h], c_prev[h].astype(BF16)) for h in heads}
        cm = {h: jnp.where(mask, c_r[h:h + 1, :], -jnp.inf) for h in heads}
        mm = {h: jnp.broadcast_to(jnp.maximum(jnp.max(cm[h], axis=1, keepdims=True), mp[h]),
                               (CHUNK, CHUNK)) for h in heads}

        kw = {h: kh[h] * w_c[:, h:h + 1] for h in heads}
        for h in heads:
            cf_ref[0, h] = dec[:, h:h + 1] * c_prev[h] + _dot_tn(kw[h].astype(BF16), vb[h])

        p = {h: s_qk[h] * jnp.exp2(cm[h] - mm[h]) for h in heads}
        p_v = {h: _dot(p[h].astype(BF16), vb[h]) for h in heads}
        w_int = {h: jnp.exp2(mp[h] - mm[h]) for h in heads}
        den = {h: w_int[h] * q_c[h][:, DV_A:] + p_v[h][:, DV_A:] for h in heads}
        b_t = {h: jnp.broadcast_to(b_c[:, h:h + 1], (CHUNK, CHUNK)) for h in heads}
        inv = {h: 1.0 / jnp.maximum(jnp.abs(den[h]), jnp.exp2(-(b_t[h] + mm[h]))) for h in heads}
        hval = {h: jnp.concatenate(
            [(w_int[h] * q_c[h][:, c * CHUNK:(c + 1) * CHUNK] + p_v[h][:, c * CHUNK:(c + 1) * CHUNK])
             * inv[h] for c in range(DV_A // CHUNK)], axis=1) for h in heads}

        if not finalize:
            for h in heads:
                out_ref[0, rows, vsl[h]] = hval[h].astype(out_ref.dtype)
            return

        hs = {h: hval[h] + hb_ref[0, rows, vsl[h]].astype(F32) for h in heads}
        mu = {h: jnp.mean(hs[h], axis=-1, keepdims=True) for h in heads}
        hc = {h: hs[h] - mu[h] for h in heads}
        var = {h: jnp.mean(hc[h] * hc[h], axis=-1, keepdims=True) for h in heads}
        for h in heads:
            hn = hc[h] * lax.rsqrt(var[h] + LN_EPS) * mhw_ref[:, vsl[h]]
            out_ref[0, rows, vsl[h]] = (hn * gate[h]).astype(out_ref.dtype)

    group = HEAD_GROUP_FWD if finalize else HEAD_GROUP_BWD
    for g0 in range(0, NH_A, group):
        run_heads(range(g0, g0 + group))


def _mlstm_call(reverse, finalize, state, *, p3, small3, gate_b128, conv_w=None, conv_b=None,
                qk=None, hb=None, mh_w=None):
    bsz, t_len, _ = p3.shape
    nc, main, prev, nxt, const2 = _chunk_specs(t_len, reverse, SUB_MLSTM)
    c0, m0 = state
    state_specs = [pl.BlockSpec((1, NH_A, DQK_A, DV_A + CHUNK), lambda b, i: (b, 0, 0, 0)),
                   pl.BlockSpec((1, 1, 128), lambda b, i: (b, 0, 0))]
    state_shapes = [jax.ShapeDtypeStruct(c0.shape, F32), jax.ShapeDtypeStruct(m0.shape, F32)]
    wide = main(0, W_A)
    if finalize:
        in_specs = [wide, main(COL_V, W_A), main(0, N_SMALL), const2((1, N_SMALL))] + state_specs
        in_specs += [wide, main(COL_O, W_A), main(COL_ZA, W_A), const2((1, W_A))]
        args = [qk, p3, small3, gate_b128, c0, m0, hb, p3, p3, mh_w]
        out_specs = [wide] + state_specs
        out_shape = [jax.ShapeDtypeStruct((bsz, t_len, W_A), BF16)] + state_shapes
    else:
        in_specs = [main(COL_Q, CW), prev(COL_Q), nxt(COL_Q),
                    main(COL_K, CW), prev(COL_K), nxt(COL_K),
                    main(COL_V, W_A), main(0, N_SMALL),
                    const2((3, 2 * QK_A)), const2((1, 2 * QK_A)), const2((1, N_SMALL))] + state_specs
        args = [p3, p3, p3, p3, p3, p3, p3, small3, conv_w, conv_b, gate_b128, c0, m0]
        out_specs = [wide, wide] + state_specs
        out_shape = [jax.ShapeDtypeStruct((bsz, t_len, W_A), HB_DTYPE),
                     jax.ShapeDtypeStruct((bsz, t_len, 2 * QK_A), BF16)] + state_shapes
    res = pl.pallas_call(
        functools.partial(_mlstm_kernel, reverse, finalize),
        grid=(bsz, nc), in_specs=in_specs, out_specs=out_specs, out_shape=out_shape,
        compiler_params=_cparams(("arbitrary", "arbitrary")),
        name="mlstm_" + ("bwd" if reverse else "fwd"),
    )(*args)
    if finalize:
        return res[0], tuple(res[1:])
    return res[0], res[1], tuple(res[2:])


def _ssd_kernel(reverse, finalize, *refs):
    if finalize:
        (xc_ref, bcc_ref, gs_ref, dtb_ref, alog_ref, s0_ref, yb_ref, z0_ref, z1_ref, dsk_ref,
         nw_ref, out_ref, sf_ref, y_s) = refs
    else:
        (x0_ref, x0p_ref, x0n_ref, x1_ref, x1p_ref, x1n_ref, bc_ref, bcp_ref, bcn_ref,
         gs_ref, cw_ref, cb_ref, dtb_ref, alog_ref, s0_ref, out_ref, xc_ref, sf_ref) = refs

    i, j, nc = _chunk_pos(reverse)

    @pl.when(i == 0)
    def _():
        sf_ref[...] = s0_ref[...]

    has_prev = j > 0
    has_next = j < nc - 1
    for sc in (reversed(range(SUB_SSD)) if reverse else range(SUB_SSD)):
        _ssd_chunk(reverse, finalize, sc, has_prev, has_next, refs)


def _ssd_chunk(reverse, finalize, sc, has_prev, has_next, refs):
    if finalize:
        (xc_ref, bcc_ref, gs_ref, dtb_ref, alog_ref, s0_ref, yb_ref, z0_ref, z1_ref, dsk_ref,
         nw_ref, out_ref, sf_ref, y_s) = refs
    else:
        (x0_ref, x0p_ref, x0n_ref, x1_ref, x1p_ref, x1n_ref, bc_ref, bcp_ref, bcn_ref,
         gs_ref, cw_ref, cb_ref, dtb_ref, alog_ref, s0_ref, out_ref, xc_ref, sf_ref) = refs
    rows = slice(sc * CHUNK, (sc + 1) * CHUNK)
    _, tri, _ = _tri_masks(reverse)
    edge = 0 if reverse else CHUNK - 1

    t_i = lax.broadcasted_iota(jnp.int32, (CHUNK, CHUNK), 0)
    l_i = lax.broadcasted_iota(jnp.int32, (CHUNK, CHUNK), 1)
    s_lo = jnp.where(l_i < HALF, l_i, l_i - HALF)
    lo_half = l_i < HALF
    if reverse:
        mask1, mask2 = s_lo >= t_i, s_lo + HALF >= t_i
    else:
        mask1, mask2 = s_lo <= t_i, s_lo + HALF <= t_i
    k_i = lax.broadcasted_iota(jnp.int32, (2 * CHUNK, CHUNK), 0)
    kl_i = lax.broadcasted_iota(jnp.int32, (2 * CHUNK, CHUNK), 1)
    blockdiag = ((k_i // HALF) % 2 == 1) == (kl_i >= HALF)

    dt_all = _softplus(gs_ref[0, rows, :] + dtb_ref[...])
    da_all = dt_all * (-LOG2E * jnp.exp(alog_ref[...]))
    acum_all = _cumsum_rows(tri, da_all)
    do = H_B if reverse else 0
    acum_t = acum_all.T
    a_even = acum_t[do:do + NPAIR, :]
    a_odd = acum_t[do + NPAIR:do + H_B, :]
    lane16 = lax.broadcasted_iota(jnp.int32, (NPAIR, CHUNK), 1) < HALF
    rv1 = jnp.where(lane16, a_even, pltpu.roll(a_odd, HALF, 1))
    rv2 = jnp.where(lane16, pltpu.roll(a_even, HALF, 1), a_odd)

    if finalize:
        bmat = bcc_ref[0, rows, 0:GP_B]
        cmat = bcc_ref[0, rows, GP_B:2 * GP_B]
        ssq = jnp.zeros((CHUNK, CHUNK), F32)
    else:
        bmat = _conv_block(bc_ref, bcp_ref, bcn_ref, sc, SUB_SSD, slice(0, GP_B), has_prev, has_next,
                           cw_ref[:, W_B:W_B + GP_B], cb_ref[:, W_B:W_B + GP_B]).astype(BF16)
        cmat = _conv_block(bc_ref, bcp_ref, bcn_ref, sc, SUB_SSD, slice(GP_B, 2 * GP_B), has_prev, has_next,
                           cw_ref[:, W_B + GP_B:W_B + 2 * GP_B],
                           cb_ref[:, W_B + GP_B:W_B + 2 * GP_B]).astype(BF16)
        xc_ref[0, rows, W_B:W_B + GP_B] = bmat
        xc_ref[0, rows, W_B + GP_B:W_B + 2 * GP_B] = cmat

    for g in range(G_B):
        gs = slice(g * GP_B, (g + 1) * GP_B)
        if finalize:
            xg = xc_ref[0, rows, gs].astype(F32)
        else:
            xr, xpr, xnr = (x0_ref, x0p_ref, x0n_ref) if g < 2 else (x1_ref, x1p_ref, x1n_ref)
            cs = slice((g % 2) * GP_B, (g % 2 + 1) * GP_B)
            xg = _conv_block(xr, xpr, xnr, sc, SUB_SSD, cs, has_prev, has_next, cw_ref[:, gs], cb_ref[:, gs])
            xc_ref[0, rows, gs] = xg.astype(BF16)
        bg = bmat[:, g * N_B:(g + 1) * N_B]
        cg = cmat[:, g * N_B:(g + 1) * N_B]
        cb1 = _dot_nt(cg, jnp.concatenate([bg[0:HALF], bg[0:HALF]], axis=0))
        cb2 = _dot_nt(cg, jnp.concatenate([bg[HALF:], bg[HALF:]], axis=0))
        s_prev = sf_ref[0, g]
        cs_prev = _dot(cg, s_prev.astype(BF16))
        y_parts, xw_parts, tot_parts = [], [], []
        for pp in range(R_B // 2):
            pr = g * (R_B // 2) + pp
            ps = slice(pp * CHUNK, (pp + 1) * CHUNK)
            idx = jnp.where(lo_half, do + pr, do + NPAIR + pr)
            acum_b = jnp.take_along_axis(acum_all, idx, axis=1)
            dt_b = jnp.take_along_axis(dt_all, idx, axis=1)
            xdt = xg[:, ps] * dt_b
            xdtb = xdt.astype(BF16)
            w1 = cb1 * jnp.exp2(jnp.where(mask1, acum_b - rv1[pr:pr + 1, :], -jnp.inf))
            w2 = cb2 * jnp.exp2(jnp.where(mask2, acum_b - rv2[pr:pr + 1, :], -jnp.inf))
            w12 = jnp.concatenate([w1.astype(BF16), w2.astype(BF16)], axis=1)
            x12 = jnp.concatenate([xdtb[0:HALF], xdtb[0:HALF], xdtb[HALF:], xdtb[HALF:]], axis=0)
            x12 = jnp.where(blockdiag, x12, jnp.zeros_like(x12))
            tot = acum_b[edge:edge + 1, :]
            y_parts.append(_dot(w12, x12) + jnp.exp2(acum_b) * cs_prev[:, ps])
            xw_parts.append((xdt * jnp.exp2(tot - acum_b)).astype(BF16))
            tot_parts.append(tot)
        xw = jnp.concatenate(xw_parts, axis=1)
        etot = jnp.exp2(jnp.concatenate(tot_parts, axis=1))
        sf_ref[0, g] = etot * s_prev + _dot_tn(bg, xw)
        y_grp = jnp.concatenate(y_parts, axis=1)
        if finalize:
            z_ref = z0_ref if g < 2 else z1_ref
            zs = slice((g % 2) * GP_B, (g % 2 + 1) * GP_B)
            y = y_grp + yb_ref[0, rows, gs].astype(F32) + dsk_ref[:, gs] * xg
            y = y * _silu(z_ref[0, rows, zs].astype(F32))
            y_s[:, gs] = y
            y2 = y * y
            ssq = ssq + (y2[:, 0:CHUNK] + y2[:, CHUNK:2 * CHUNK]
                         + y2[:, 2 * CHUNK:3 * CHUNK] + y2[:, 3 * CHUNK:])
        else:
            out_ref[0, rows, gs] = y_grp.astype(out_ref.dtype)

    if finalize:
        inv = lax.rsqrt(jnp.sum(ssq, axis=-1, keepdims=True) * (1.0 / W_B) + LN_EPS)
        for half in range(2):
            cs = slice(half * CW, (half + 1) * CW)
            out_ref[0, rows, cs] = (y_s[:, cs] * inv * nw_ref[:, cs]).astype(out_ref.dtype)


def _ssd_call(reverse, finalize, s0, *, small3, dtb128, alog128, p3, pcol0=0, conv_w=None,
              conv_b=None, xc=None, yb=None, d_skip=None, norm_w=None):
    bsz, t_len, _ = p3.shape
    c_x, c_bc, c_zb = COL_X + pcol0, COL_BC + pcol0, COL_ZB + pcol0
    nc, main, prev, nxt, const2 = _chunk_specs(t_len, reverse, SUB_SSD)
    state_spec = pl.BlockSpec((1, G_B, N_B, GP_B), lambda b, i: (b, 0, 0, 0))
    state_shape = jax.ShapeDtypeStruct(s0.shape, F32)
    xc_spec = main(0, W_B + 2 * GP_B)
    if finalize:
        in_specs = [main(0, W_B), main(2, CW), main(0, N_SMALL), const2((1, N_SMALL)),
                    const2((1, N_SMALL)), state_spec,
                    main(0, W_B), main(c_zb, CW), main(c_zb + 1, CW),
                    const2((1, W_B)), const2((1, W_B))]
        args = [xc, xc, small3, dtb128, alog128, s0, yb, p3, p3, d_skip, norm_w]
        out_specs = [main(0, W_B), state_spec]
        out_shape = [jax.ShapeDtypeStruct((bsz, t_len, W_B), BF16), state_shape]
        scratch = [pltpu.VMEM((CHUNK, W_B), F32)]
    else:
        in_specs = [main(c_x, CW), prev(c_x), nxt(c_x),
                    main(c_x + 1, CW), prev(c_x + 1), nxt(c_x + 1),
                    main(c_bc, CW), prev(c_bc), nxt(c_bc),
                    main(0, N_SMALL),
                    const2((3, 3 * CW)), const2((1, 3 * CW)), const2((1, N_SMALL)),
                    const2((1, N_SMALL)), state_spec]
        args = [p3] * 9 + [small3, conv_w, conv_b, dtb128, alog128, s0]
        out_specs = [main(0, W_B), xc_spec, state_spec]
        out_shape = [jax.ShapeDtypeStruct((bsz, t_len, W_B), HB_DTYPE),
                     jax.ShapeDtypeStruct((bsz, t_len, W_B + 2 * GP_B), BF16), state_shape]
        scratch = []
    return pl.pallas_call(
        functools.partial(_ssd_kernel, reverse, finalize),
        grid=(bsz, nc), in_specs=in_specs, out_specs=out_specs, out_shape=out_shape,
        scratch_shapes=scratch,
        compiler_params=_cparams(("arbitrary", "arbitrary")),
        name="ssd_" + ("bwd" if reverse else "fwd"),
    )(*args)


def _to_colmajor(t, bsz):
    ch = t.shape[-1]
    return t.reshape(bsz, -1, GRID_W, ch).transpose(0, 2, 1, 3).reshape(-1, ch)


def _from_colmajor(t, bsz):
    ch = t.shape[-1]
    return t.reshape(bsz, GRID_W, -1, ch).transpose(0, 2, 1, 3).reshape(-1, ch)


def _mixer_scans(pa, sa, pb, sb, bsz, t_len, lp, states, pcol0=0):
    st_mf, st_mb, st_sf, st_sb = states
    pa = pa.reshape(bsz, t_len, pa.shape[-1])
    sa = sa.reshape(bsz, t_len, N_SMALL)
    h_b, qk, st_mb = _mlstm_call(True, False, st_mb, p3=pa, small3=sa, gate_b128=lp["gate_b128"],
                                 conv_w=lp["conv_qk_w"], conv_b=lp["conv_qk_b"])
    y_a, st_mf = _mlstm_call(False, True, st_mf, p3=pa, small3=sa, gate_b128=lp["gate_b128"],
                             qk=qk, hb=h_b, mh_w=lp["mh_w"])

    pb = pb.reshape(bsz, t_len, pb.shape[-1])
    sb = sb.reshape(bsz, t_len, N_SMALL)
    y_bb, xc, st_sb = _ssd_call(True, False, st_sb, small3=sb, dtb128=lp["dtb128"],
                                alog128=lp["alog128"], p3=pb, pcol0=pcol0,
                                conv_w=lp["conv_xbc_w"], conv_b=lp["conv_xbc_b"])
    y_b, st_sf = _ssd_call(False, True, st_sf, small3=sb, dtb128=lp["dtb128"],
                           alog128=lp["alog128"], p3=pb, pcol0=pcol0, xc=xc, yb=y_bb,
                           d_skip=lp["d_skip"], norm_w=lp["ssm_w"])
    return y_a, y_b, (st_mf, st_mb, st_sf, st_sb)


def _mixer(x2d, bsz, t_len, latent, shift, scale, mod_index, tm, tm_b, w, lp, states,
           cast_layer=None):
    n_a = len(A_STARTS)
    if latent:
        pa, sa, u2d = _in_proj_call(x2d, shift, scale, w, A_STARTS, lp["w_small_a"], mod_index, tm,
                                    tn=2 * CW)
        pb, sb = _proj_call(_to_colmajor(u2d, bsz), w, B_STARTS, lp["w_small_b"], tm_b, tile0=n_a)
        pcol0 = 0
    else:
        w_small = jnp.concatenate([lp["w_small_a"], lp["w_small_b"]], axis=0)
        p, small, _, w = _in_proj_call(x2d, shift, scale, w, A_STARTS + B_STARTS, w_small,
                                       mod_index, tm, cast_layer)
        pa, pb, pcol0 = p, p, n_a
        sa, sb = small[:, :N_SMALL], small[:, N_SMALL:]
    y_a, y_b, states = _mixer_scans(pa, sa, pb, sb, bsz, t_len, lp, states, pcol0)
    y_b = y_b.reshape(bsz * t_len, W_B)
    if latent:
        y_b = _from_colmajor(y_b, bsz)
    return y_a.reshape(bsz * t_len, W_A), y_b, states, w


def _zero_states(bsz):
    m_state = (jnp.zeros((bsz, NH_A, DQK_A, DV_A + CHUNK), F32), jnp.zeros((bsz, 1, 128), F32))
    s_state = jnp.zeros((bsz, G_B, N_B, GP_B), F32)
    return (m_state, m_state, s_state, s_state)


def _layer_params(l, w_t, conv_qk_w, conv_qk_b, gate_b, mh_norm_w, conv_xbc_w, conv_xbc_b,
                  dt_bias, a_log, d_skip, ssm_norm_w):
    feats = lambda a, n: lax.slice(w_t, (l, a, 0), (l + 1, a + n, w_t.shape[2]))[0]

    def pad_cols(a):
        return jnp.concatenate([a, jnp.zeros((a.shape[0], N_SMALL - a.shape[1]), a.dtype)], axis=1)

    def pad_rows(a):
        return jnp.concatenate([a, jnp.zeros((N_SMALL - a.shape[0], a.shape[1]), a.dtype)], axis=0)

    def even_odd(a):
        a4 = a.reshape(a.shape[0], 2, NPAIR, 2)
        return a4.transpose(0, 1, 3, 2).reshape(a.shape[0], 2 * H_B)

    return {
        "w_small_a": pad_rows(feats(O_GATE, 4 * NH_A)),
        "w_small_b": pad_rows(even_odd(feats(O_DT, 2 * H_B).T).T),
        "conv_qk_w": conv_qk_w[l], "conv_qk_b": conv_qk_b[l].reshape(1, -1),
        "gate_b128": pad_cols(gate_b[l].reshape(1, -1)),
        "mh_w": mh_norm_w[l].reshape(1, W_A),
        "conv_xbc_w": conv_xbc_w[l], "conv_xbc_b": conv_xbc_b[l].reshape(1, -1),
        "dtb128": pad_cols(even_odd(dt_bias[l].reshape(1, -1))),
        "alog128": pad_cols(even_odd(a_log[l].reshape(1, -1))),
        "d_skip": jnp.repeat(d_skip[l], P_B).reshape(1, W_B),
        "ssm_w": ssm_norm_w[l].reshape(1, W_B),
    }


def kernel(x, c, ctx, c_ctx, w_ada, b_ada, w_in, conv_qk_w, conv_qk_b, gate_b, mh_norm_w,
           conv_xbc_w, conv_xbc_b, dt_bias, a_log, d_skip, ssm_norm_w, w_out, ln_g, ln_b):
    bsz, t_len, d = x.shape
    ctx_len = ctx.shape[1]
    depth = w_in.shape[0]
    assert d == D_MODEL and depth == DEPTH and bsz + 1 <= 8
    assert t_len % (CHUNK * GRID_W) == 0 and ctx_len % CHUNK == 0

    cvec = jnp.concatenate([c, c_ctx[None, :]], axis=0)
    mod = _ada_call(jnp.broadcast_to(cvec[:, :, None], (bsz + 1, d, 128)), w_ada, b_ada)

    w_t = jnp.swapaxes(w_in, 1, 2)
    w_out_bf = _cast_bf16(w_out, 2048)

    x2d = x.reshape(bsz * t_len, d)
    xc2d = ctx.reshape(bsz * ctx_len, d)
    tm_in, tm_ssd, tm_out, tm_ctx_in, tm_ctx_out = 1024, 2048, 512, bsz * ctx_len, 256
    lat_in = lambda i: i // (t_len // tm_in)
    lat_out = lambda i: i // (t_len // tm_out)
    ctx_index = lambda i: bsz

    for l in range(depth):
        last = l == depth - 1
        lp = _layer_params(l, w_t, conv_qk_w, conv_qk_b, gate_b, mh_norm_w, conv_xbc_w,
                           conv_xbc_b, dt_bias, a_log, d_skip, ssm_norm_w)
        shift = mod[l, :, 0:d].reshape(8, 1, d)
        scale = mod[l, :, d:2 * d].reshape(8, 1, d)
        gate = mod[l, :, 2 * d:3 * d].reshape(8, 1, d)

        yc_a, yc_b, ctx_states, w_bf = _mixer(xc2d, bsz, ctx_len, False, shift, scale, ctx_index,
                                              tm_ctx_in, tm_ctx_in, w_t, lp, _zero_states(bsz),
                                              cast_layer=l)
        y_a, y_b, _, _ = _mixer(x2d, bsz, t_len, True, shift, scale, lat_in, tm_in, tm_ssd, w_bf,
                                lp, ctx_states)
        x2d = _out_proj_call(y_a, y_b, x2d, gate, w_out_bf, l, ln_g[l], ln_b[l], lat_out, tm_out)
        if not last:
            xc2d = _out_proj_call(yc_a, yc_b, xc2d, gate, w_out_bf, l, ln_g[l], ln_b[l],
                                  ctx_index, tm_ctx_out)
    return x2d.reshape(bsz, t_len, d)
```

```python
import functools
import math

import jax
import jax.numpy as jnp
from jax import lax
from jax.experimental import pallas as pl
from jax.experimental.pallas import tpu as pltpu

F32 = jnp.float32
BF16 = jnp.bfloat16

D_MODEL = 2048
DEPTH = 2
GRID_W = 64
NH_A = 8
DV_A = 256
DQK_A = 128
W_A = NH_A * DV_A
QK_A = NH_A * DQK_A
P_B = 64
H_B = 32
G_B = 4
R_B = H_B // G_B
N_B = 128
W_B = H_B * P_B
GP_B = R_B * P_B
NPAIR = H_B // 2
CHUNK = 128
HALF = CHUNK // 2
SUB_MLSTM = 1
HEAD_GROUP_BWD = 4
HEAD_GROUP_FWD = 8
SUB_SSD = 4
LN_EPS = 1e-5
DEEPNORM_ALPHA = (2 * DEPTH) ** 0.25
LOG2E = 1.4426950408889634

CW = 1024
COL_Q, COL_K, COL_V, COL_O, COL_ZA = 0, 1, 2, 4, 6
N_MAIN_A = 8 * CW
COL_X, COL_BC, COL_ZB = 0, 2, 3
N_SMALL = 128
O_GATE = N_MAIN_A
O_XBC = O_GATE + 4 * NH_A
O_DT = O_XBC + W_B + 2 * G_B * N_B
O_ZB = O_DT + 2 * H_B
A_STARTS = [k * CW for k in range(N_MAIN_A // CW)]
B_STARTS = ([O_XBC + k * CW for k in range((W_B + 2 * G_B * N_B) // CW)]
            + [O_ZB + k * CW for k in range(W_B // CW)])

HALO = 16
VMEM_LIMIT = 62 * 1024 * 1024

PROJ_DTYPE = BF16
HB_DTYPE = BF16


def _cparams(sem):
    return pltpu.CompilerParams(dimension_semantics=sem, vmem_limit_bytes=VMEM_LIMIT)


def _softplus(x):
    return jnp.maximum(x, 0.0) + jnp.log1p(jnp.exp(-jnp.abs(x)))


def _log_sigmoid(x):
    return jnp.minimum(x, 0.0) - jnp.log1p(jnp.exp(-jnp.abs(x)))


def _silu(x):
    return x * jax.nn.sigmoid(x)


def _dot(a, b):
    return jnp.dot(a, b, preferred_element_type=F32)


def _dot_nt(a, b):
    return lax.dot_general(a, b, (((1,), (1,)), ((), ())), preferred_element_type=F32)


def _dot_tn(a, b):
    return lax.dot_general(a, b, (((0,), (0,)), ((), ())), preferred_element_type=F32)


def _cumsum_rows(tri, x):
    return jnp.dot(tri, x, preferred_element_type=F32, precision=lax.Precision.HIGHEST)


def _ada_kernel(c_ref, w_ref, b_ref, o_ref):
    n_rows, d, _ = c_ref.shape
    tn = w_ref.shape[2]
    o_ref[...] = jnp.zeros(o_ref.shape, F32)
    for r in range(n_rows):
        s = _silu(c_ref[r]).reshape(d // 8, 8, 128)
        for cb in range(tn // 128):
            cs = slice(cb * 128, (cb + 1) * 128)
            acc = jnp.sum(w_ref[0, :, cs].reshape(d // 8, 8, 128) * s, axis=0)
            o_ref[0, r:r + 1, cs] = jnp.sum(acc, axis=0, keepdims=True) + b_ref[0, :, cs]


def _ada_call(crep, w_ada, b_ada):
    depth, d, n3 = w_ada.shape
    n_rows = crep.shape[0]
    tn = 1024
    return pl.pallas_call(
        _ada_kernel,
        grid=(depth, n3 // tn),
        in_specs=[pl.BlockSpec((n_rows, d, 128), lambda l, j: (0, 0, 0)),
                  pl.BlockSpec((1, d, tn), lambda l, j: (l, 0, j)),
                  pl.BlockSpec((1, 1, tn), lambda l, j: (l, 0, j))],
        out_specs=pl.BlockSpec((1, 8, tn), lambda l, j: (l, 0, j)),
        out_shape=jax.ShapeDtypeStruct((depth, 8, n3), F32),
        compiler_params=_cparams(("arbitrary", "arbitrary")),
        name="adaln_mod",
    )(crep, w_ada, b_ada.reshape(depth, 1, n3))


def _cast_kernel(x_ref, o_ref):
    o_ref[...] = x_ref[...].astype(o_ref.dtype)


def _cast_bf16(w3, tr):
    depth, rows, cols = w3.shape
    spec = pl.BlockSpec((1, tr, cols), lambda l, i: (l, i, 0))
    return pl.pallas_call(
        _cast_kernel, grid=(depth, pl.cdiv(rows, tr)), in_specs=[spec], out_specs=spec,
        out_shape=jax.ShapeDtypeStruct(w3.shape, BF16),
        compiler_params=_cparams(("arbitrary", "arbitrary")),
        name="cast_bf16",
    )(w3)


def _proj_tile(cast_w, u_ref, w_ref, p_ref, wout):
    if cast_w:
        w = w_ref[0].astype(BF16)
        wout[0][...] = w
    else:
        w = w_ref[...]
    p_ref[...] = _dot_nt(u_ref[...], w).astype(p_ref.dtype)


def _in_proj_kernel(cast_w, x_ref, shift_ref, scale_ref, w_ref, ws_ref, p_ref, small_ref, u_ref,
                    *wout):
    @pl.when(pl.program_id(1) == 0)
    def _():
        x = x_ref[...]
        mu = jnp.mean(x, axis=-1, keepdims=True)
        xc = x - mu
        var = jnp.mean(xc * xc, axis=-1, keepdims=True)
        u = xc * lax.rsqrt(var + LN_EPS) * (1.0 + scale_ref[0]) + shift_ref[0]
        ub = u.astype(BF16)
        u_ref[...] = ub
        small_ref[...] = _dot_nt(ub, ws_ref[...].astype(BF16))

    _proj_tile(cast_w, u_ref, w_ref, p_ref, wout)


def _proj_kernel(u_ref, w_ref, ws_ref, p_ref, small_ref):
    @pl.when(pl.program_id(1) == 0)
    def _():
        small_ref[...] = _dot_nt(u_ref[...], ws_ref[...].astype(BF16))

    _proj_tile(False, u_ref, w_ref, p_ref, ())


def _weight_specs(w, starts, layer, tile0, rows, tm, tn, d):
    if layer is None:
        return False, pl.BlockSpec((tn, d), lambda i, j: (tile0 + j, 0)), [], []
    assert rows == tm and tile0 == 0

    align = functools.reduce(math.gcd, starts, tn)

    def start(j):
        row = starts[0]
        for k in range(1, len(starts)):
            row = jnp.where(j >= k, starts[k], row)
        return pl.multiple_of(row, align)

    spec = pl.BlockSpec((pl.Element(1), pl.Element(tn), pl.Element(d)),
                        lambda i, j: (layer, start(j), 0))
    return (True, spec, [pl.BlockSpec((tn, d), lambda i, j: (j, 0))],
            [jax.ShapeDtypeStruct((len(starts) * tn, d), BF16)])


def _in_proj_call(x2d, shift, scale, w, starts, w_small, mod_index, tm, layer=None, tile0=0,
                  tn=CW):
    rows, d = x2d.shape
    n_main = len(starts) * CW
    assert layer is None or tn == CW
    n_small = w_small.shape[0]
    cast_w, w_spec, wout_specs, wout_shapes = _weight_specs(w, starts, layer, tile0, rows, tm,
                                                            tn, d)
    return pl.pallas_call(
        functools.partial(_in_proj_kernel, cast_w),
        grid=(rows // tm, n_main // tn),
        in_specs=[pl.BlockSpec((tm, d), lambda i, j: (i, 0)),
                  pl.BlockSpec((1, 1, d), lambda i, j: (mod_index(i), 0, 0)),
                  pl.BlockSpec((1, 1, d), lambda i, j: (mod_index(i), 0, 0)),
                  w_spec,
                  pl.BlockSpec((n_small, d), lambda i, j: (0, 0))],
        out_specs=[pl.BlockSpec((tm, tn), lambda i, j: (i, j)),
                   pl.BlockSpec((tm, n_small), lambda i, j: (i, 0)),
                   pl.BlockSpec((tm, d), lambda i, j: (i, 0))] + wout_specs,
        out_shape=[jax.ShapeDtypeStruct((rows, n_main), PROJ_DTYPE),
                   jax.ShapeDtypeStruct((rows, n_small), F32),
                   jax.ShapeDtypeStruct((rows, d), BF16)] + wout_shapes,
        compiler_params=_cparams(("arbitrary", "arbitrary")),
        name="in_proj",
    )(x2d, shift, scale, w, w_small)


def _proj_call(u2d, w, starts, w_small, tm, tile0=0):
    rows, d = u2d.shape
    tn = CW
    n_main = len(starts) * tn
    _, w_spec, _, _ = _weight_specs(w, starts, None, tile0, rows, tm, tn, d)
    return pl.pallas_call(
        _proj_kernel,
        grid=(rows // tm, n_main // tn),
        in_specs=[pl.BlockSpec((tm, d), lambda i, j: (i, 0)),
                  w_spec,
                  pl.BlockSpec((N_SMALL, d), lambda i, j: (0, 0))],
        out_specs=[pl.BlockSpec((tm, tn), lambda i, j: (i, j)),
                   pl.BlockSpec((tm, N_SMALL), lambda i, j: (i, 0))],
        out_shape=[jax.ShapeDtypeStruct((rows, n_main), PROJ_DTYPE),
                   jax.ShapeDtypeStruct((rows, N_SMALL), F32)],
        compiler_params=_cparams(("arbitrary", "arbitrary")),
        name="ssd_proj",
    )(u2d, w, w_small)


def _out_proj_kernel(ya_ref, yb_ref, x_ref, gate_ref, wa_ref, wb_ref, g_ref, b_ref, o_ref):
    acc = _dot(ya_ref[...], wa_ref[0]) + _dot(yb_ref[...], wb_ref[0])
    r = DEEPNORM_ALPHA * x_ref[...] + gate_ref[0] * acc
    mu = jnp.mean(r, axis=-1, keepdims=True)
    rc = r - mu
    var = jnp.mean(rc * rc, axis=-1, keepdims=True)
    o_ref[...] = rc * lax.rsqrt(var + LN_EPS) * g_ref[...] + b_ref[...]


def _out_proj_call(ya, yb, x2d, gate, w_out3, layer, ln_g, ln_b, mod_index, tm):
    rows, d = x2d.shape
    const = lambda i: (0, 0)
    return pl.pallas_call(
        _out_proj_kernel,
        grid=(rows // tm,),
        in_specs=[pl.BlockSpec((tm, W_A), lambda i: (i, 0)),
                  pl.BlockSpec((tm, W_B), lambda i: (i, 0)),
                  pl.BlockSpec((tm, d), lambda i: (i, 0)),
                  pl.BlockSpec((1, 1, d), lambda i: (mod_index(i), 0, 0)),
                  pl.BlockSpec((1, W_A, d), lambda i: (layer, 0, 0)),
                  pl.BlockSpec((1, W_B, d), lambda i: (layer, 1, 0)),
                  pl.BlockSpec((1, d), const),
                  pl.BlockSpec((1, d), const)],
        out_specs=pl.BlockSpec((tm, d), lambda i: (i, 0)),
        out_shape=jax.ShapeDtypeStruct((rows, d), F32),
        compiler_params=_cparams(("arbitrary",)),
        name="out_proj",
    )(ya, yb, x2d, gate, w_out3, w_out3, ln_g.reshape(1, d), ln_b.reshape(1, d))


def _conv_silu(x, prev_row, next_row, w, bias):
    rows = x.shape[0]
    row = lax.broadcasted_iota(jnp.int32, x.shape, 0)
    xm = jnp.where(row == 0, prev_row, pltpu.roll(x, 1, 0))
    xp = jnp.where(row == rows - 1, next_row, pltpu.roll(x, rows - 1, 0))
    y = w[0:1] * xm + w[1:2] * x + w[2:3] * xp + bias
    return _silu(y)


def _conv_block(x_ref, p_ref, n_ref, sc, sub, cs, has_prev, has_next, w, bias):
    r0 = sc * CHUNK
    if sc > 0:
        pr = x_ref[0, r0 - 1:r0, cs].astype(F32)
    else:
        pr = jnp.where(has_prev, p_ref[0, HALO - 1:HALO, cs].astype(F32), 0.0)
    if sc < sub - 1:
        nx = x_ref[0, r0 + CHUNK:r0 + CHUNK + 1, cs].astype(F32)
    else:
        nx = jnp.where(has_next, n_ref[0, 0:1, cs].astype(F32), 0.0)
    return _conv_silu(x_ref[0, r0:r0 + CHUNK, cs].astype(F32), pr, nx, w, bias)


def _tri_masks(reverse):
    t = lax.broadcasted_iota(jnp.int32, (CHUNK, CHUNK), 0)
    s = lax.broadcasted_iota(jnp.int32, (CHUNK, CHUNK), 1)
    mask = (s >= t) if reverse else (s <= t)
    mask_t = (t >= s) if reverse else (t <= s)
    return mask, mask.astype(F32), mask_t.astype(F32)


def _chunk_pos(reverse):
    i = pl.program_id(1)
    nc = pl.num_programs(1)
    j = (nc - 1 - i) if reverse else i
    return i, j, nc


def _chunk_specs(t_len, reverse, sub):
    blk = sub * CHUNK
    nc = t_len // blk
    nhb = t_len // HALO
    hpc = blk // HALO

    def jj(i):
        return (nc - 1 - i) if reverse else i

    def main(col, width):
        return pl.BlockSpec((1, blk, width), lambda b, i: (b, jj(i), (col * CW) // width))

    def prev(col):
        return pl.BlockSpec((1, HALO, CW), lambda b, i: (b, jnp.maximum(jj(i) * hpc - 1, 0), col))

    def nxt(col):
        return pl.BlockSpec((1, HALO, CW),
                            lambda b, i: (b, jnp.minimum((jj(i) + 1) * hpc, nhb - 1), col))

    def const2(shape):
        return pl.BlockSpec(shape, lambda b, i: (0, 0))

    return nc, main, prev, nxt, const2


def _mlstm_kernel(reverse, finalize, *refs):
    if finalize:
        (qk_ref, v_ref, gs_ref, gb_ref, c0_ref, m0_ref, hb_ref, o_ref, z_ref, mhw_ref,
         out_ref, cf_ref, mf_ref) = refs
    else:
        (q_ref, qp_ref, qn_ref, k_ref, kp_ref, kn_ref, v_ref, gs_ref, cw_ref, cb_ref, gb_ref,
         c0_ref, m0_ref, out_ref, qk_ref, cf_ref, mf_ref) = refs

    i, j, nc = _chunk_pos(reverse)

    @pl.when(i == 0)
    def _():
        cf_ref[...] = c0_ref[...]
        mf_ref[...] = m0_ref[...]

    has_prev = j > 0
    has_next = j < nc - 1
    for sc in (reversed(range(SUB_MLSTM)) if reverse else range(SUB_MLSTM)):
        _mlstm_chunk(reverse, finalize, sc, has_prev, has_next, refs)


def _mlstm_chunk(reverse, finalize, sc, has_prev, has_next, refs):
    if finalize:
        (qk_ref, v_ref, gs_ref, gb_ref, c0_ref, m0_ref, hb_ref, o_ref, z_ref, mhw_ref,
         out_ref, cf_ref, mf_ref) = refs
    else:
        (q_ref, qp_ref, qn_ref, k_ref, kp_ref, kn_ref, v_ref, gs_ref, cw_ref, cb_ref, gb_ref,
         c0_ref, m0_ref, out_ref, qk_ref, cf_ref, mf_ref) = refs
    rows = slice(sc * CHUNK, (sc + 1) * CHUNK)
    mask, tri, tri_t = _tri_masks(reverse)
    edge = 0 if reverse else CHUNK - 1

    gb = gs_ref[0, rows, :] + gb_ref[...]
    gbt = gb.T
    io = 2 * NH_A if reverse else 0
    b_all = _cumsum_rows(tri, LOG2E * _log_sigmoid(gb))
    li_c = LOG2E * gb[:, io:io + NH_A]
    b_c = b_all[:, io + NH_A:io + 2 * NH_A]
    b_r = jnp.dot(LOG2E * _log_sigmoid(gbt[io + NH_A:io + 2 * NH_A, :]), tri_t,
                  preferred_element_type=F32, precision=lax.Precision.HIGHEST)
    c_r = LOG2E * gbt[io:io + NH_A, :] - b_r

    m_prev = mf_ref[0][:, 0:NH_A]
    g_c = b_c[edge:edge + 1, :]
    a_c = g_c - b_c + li_c
    m_new = jnp.maximum(g_c + m_prev, jnp.max(a_c, axis=0, keepdims=True))
    w_c = jnp.exp2(a_c - m_new)
    dec = jnp.exp2(g_c + m_prev - m_new)

    ksl = [slice(h * DQK_A, (h + 1) * DQK_A) for h in range(NH_A)]
    vsl = [slice(h * DV_A, (h + 1) * DV_A) for h in range(NH_A)]
    kksl = [slice(QK_A + h * DQK_A, QK_A + (h + 1) * DQK_A) for h in range(NH_A)]
    mf_ref[0, :, 0:NH_A] = m_new

    def run_heads(heads):
        if finalize:
            qb = {h: qk_ref[0, rows, ksl[h]] for h in heads}
            kb = {h: qk_ref[0, rows, kksl[h]] for h in heads}
            qh = {h: qb[h].astype(F32) for h in heads}
            kh = {h: kb[h].astype(F32) for h in heads}
            gate = {}
            for h in heads:
                o = o_ref[0, rows, vsl[h]].astype(F32)
                z = z_ref[0, rows, vsl[h]].astype(F32)
                gate[h] = z / ((1.0 + jnp.exp(-o)) * (1.0 + jnp.exp(-z)))
        else:
            qh = {h: _conv_block(q_ref, qp_ref, qn_ref, sc, SUB_MLSTM, ksl[h], has_prev, has_next,
                              cw_ref[:, ksl[h]], cb_ref[:, ksl[h]]) * (DQK_A ** -0.5) for h in heads}
            kh = {h: _conv_block(k_ref, kp_ref, kn_ref, sc, SUB_MLSTM, ksl[h], has_prev, has_next,
                              cw_ref[:, kksl[h]], cb_ref[:, kksl[h]]) for h in heads}
            qb = {h: qh[h].astype(BF16) for h in heads}
            kb = {h: kh[h].astype(BF16) for h in heads}
            for h in heads:
                qk_ref[0, rows, ksl[h]] = qb[h]
                qk_ref[0, rows, kksl[h]] = kb[h]
        ones_l = jnp.ones((CHUNK, CHUNK), BF16)
        vb = {h: jnp.concatenate([v_ref[0, rows, vsl[h]].astype(BF16), ones_l], axis=1)
              for h in heads}
        c_prev = {h: cf_ref[0, h] for h in heads}
        mp = {h: m_prev[:, h:h + 1] for h in heads}

        s_qk = {h: _dot_nt(qb[h], kb[h]) for h in heads}
        q_c = {h: _dot(qb[h], c_prev[h].astype(BF16)) for h in heads}
        cm = {h: jnp.where(mask, c_r[h:h + 1, :], -jnp.inf) for h in heads}
        mm = {h: jnp.broadcast_to(jnp.maximum(jnp.max(cm[h], axis=1, keepdims=True), mp[h]),
                               (CHUNK, CHUNK)) for h in heads}

        kw = {h: kh[h] * w_c[:, h:h + 1] for h in heads}
        for h in heads:
            cf_ref[0, h] = dec[:, h:h + 1] * c_prev[h] + _dot_tn(kw[h].astype(BF16), vb[h])

        p = {h: s_qk[h] * jnp.exp2(cm[h] - mm[h]) for h in heads}
        p_v = {h: _dot(p[h].astype(BF16), vb[h]) for h in heads}
        w_int = {h: jnp.exp2(mp[h] - mm[h]) for h in heads}
        den = {h: w_int[h] * q_c[h][:, DV_A:] + p_v[h][:, DV_A:] for h in heads}
        b_t = {h: jnp.broadcast_to(b_c[:, h:h + 1], (CHUNK, CHUNK)) for h in heads}
        inv = {h: 1.0 / jnp.maximum(jnp.abs(den[h]), jnp.exp2(-(b_t[h] + mm[h]))) for h in heads}
        hval = {h: jnp.concatenate(
            [(w_int[h] * q_c[h][:, c * CHUNK:(c + 1) * CHUNK] + p_v[h][:, c * CHUNK:(c + 1) * CHUNK])
             * inv[h] for c in range(DV_A // CHUNK)], axis=1) for h in heads}

        if not finalize:
            for h in heads:
                out_ref[0, rows, vsl[h]] = hval[h].astype(out_ref.dtype)
            return

        hs = {h: hval[h] + hb_ref[0, rows, vsl[h]].astype(F32) for h in heads}
        mu = {h: jnp.mean(hs[h], axis=-1, keepdims=True) for h in heads}
        hc = {h: hs[h] - mu[h] for h in heads}
        var = {h: jnp.mean(hc[h] * hc[h], axis=-1, keepdims=True) for h in heads}
        for h in heads:
            hn = hc[h] * lax.rsqrt(var[h] + LN_EPS) * mhw_ref[:, vsl[h]]
            out_ref[0, rows, vsl[h]] = (hn * gate[h]).astype(out_ref.dtype)

    group = HEAD_GROUP_FWD if finalize else HEAD_GROUP_BWD
    for g0 in range(0, NH_A, group):
        run_heads(range(g0, g0 + group))


def _mlstm_call(reverse, finalize, state, *, p3, small3, gate_b128, conv_w=None, conv_b=None,
                qk=None, hb=None, mh_w=None):
    bsz, t_len, _ = p3.shape
    nc, main, prev, nxt, const2 = _chunk_specs(t_len, reverse, SUB_MLSTM)
    c0, m0 = state
    state_specs = [pl.BlockSpec((1, NH_A, DQK_A, DV_A + CHUNK), lambda b, i: (b, 0, 0, 0)),
                   pl.BlockSpec((1, 1, 128), lambda b, i: (b, 0, 0))]
    state_shapes = [jax.ShapeDtypeStruct(c0.shape, F32), jax.ShapeDtypeStruct(m0.shape, F32)]
    wide = main(0, W_A)
    if finalize:
        in_specs = [wide, main(COL_V, W_A), main(0, N_SMALL), const2((1, N_SMALL))] + state_specs
        in_specs += [wide, main(COL_O, W_A), main(COL_ZA, W_A), const2((1, W_A))]
        args = [qk, p3, small3, gate_b128, c0, m0, hb, p3, p3, mh_w]
        out_specs = [wide] + state_specs
        out_shape = [jax.ShapeDtypeStruct((bsz, t_len, W_A), BF16)] + state_shapes
    else:
        in_specs = [main(COL_Q, CW), prev(COL_Q), nxt(COL_Q),
                    main(COL_K, CW), prev(COL_K), nxt(COL_K),
                    main(COL_V, W_A), main(0, N_SMALL),
                    const2((3, 2 * QK_A)), const2((1, 2 * QK_A)), const2((1, N_SMALL))] + state_specs
        args = [p3, p3, p3, p3, p3, p3, p3, small3, conv_w, conv_b, gate_b128, c0, m0]
        out_specs = [wide, wide] + state_specs
        out_shape = [jax.ShapeDtypeStruct((bsz, t_len, W_A), HB_DTYPE),
                     jax.ShapeDtypeStruct((bsz, t_len, 2 * QK_A), BF16)] + state_shapes
    res = pl.pallas_call(
        functools.partial(_mlstm_kernel, reverse, finalize),
        grid=(bsz, nc), in_specs=in_specs, out_specs=out_specs, out_shape=out_shape,
        compiler_params=_cparams(("arbitrary", "arbitrary")),
        name="mlstm_" + ("bwd" if reverse else "fwd"),
    )(*args)
    if finalize:
        return res[0], tuple(res[1:])
    return res[0], res[1], tuple(res[2:])


def _ssd_kernel(reverse, finalize, sub, *refs):
    if finalize:
        (xc_ref, bcc_ref, gs_ref, dtb_ref, alog_ref, s0_ref, yb_ref, z0_ref, z1_ref, dsk_ref,
         nw_ref, out_ref, sf_ref, y_s) = refs
    else:
        (x0_ref, x0p_ref, x0n_ref, x1_ref, x1p_ref, x1n_ref, bc_ref, bcp_ref, bcn_ref,
         gs_ref, cw_ref, cb_ref, dtb_ref, alog_ref, s0_ref, out_ref, xc_ref, sf_ref) = refs

    i, j, nc = _chunk_pos(reverse)

    @pl.when(i == 0)
    def _():
        sf_ref[...] = s0_ref[...]

    has_prev = j > 0
    has_next = j < nc - 1
    for sc in (reversed(range(sub)) if reverse else range(sub)):
        _ssd_chunk(reverse, finalize, sc, sub, has_prev, has_next, refs)


def _ssd_chunk(reverse, finalize, sc, sub, has_prev, has_next, refs):
    if finalize:
        (xc_ref, bcc_ref, gs_ref, dtb_ref, alog_ref, s0_ref, yb_ref, z0_ref, z1_ref, dsk_ref,
         nw_ref, out_ref, sf_ref, y_s) = refs
    else:
        (x0_ref, x0p_ref, x0n_ref, x1_ref, x1p_ref, x1n_ref, bc_ref, bcp_ref, bcn_ref,
         gs_ref, cw_ref, cb_ref, dtb_ref, alog_ref, s0_ref, out_ref, xc_ref, sf_ref) = refs
    rows = slice(sc * CHUNK, (sc + 1) * CHUNK)
    _, tri, _ = _tri_masks(reverse)
    edge = 0 if reverse else CHUNK - 1

    t_i = lax.broadcasted_iota(jnp.int32, (CHUNK, CHUNK), 0)
    l_i = lax.broadcasted_iota(jnp.int32, (CHUNK, CHUNK), 1)
    s_lo = jnp.where(l_i < HALF, l_i, l_i - HALF)
    lo_half = l_i < HALF
    if reverse:
        mask1, mask2 = s_lo >= t_i, s_lo + HALF >= t_i
    else:
        mask1, mask2 = s_lo <= t_i, s_lo + HALF <= t_i
    k_i = lax.broadcasted_iota(jnp.int32, (2 * CHUNK, CHUNK), 0)
    kl_i = lax.broadcasted_iota(jnp.int32, (2 * CHUNK, CHUNK), 1)
    blockdiag = ((k_i // HALF) % 2 == 1) == (kl_i >= HALF)

    dt_all = _softplus(gs_ref[0, rows, :] + dtb_ref[...])
    da_all = dt_all * (-LOG2E * jnp.exp(alog_ref[...]))
    acum_all = _cumsum_rows(tri, da_all)
    do = H_B if reverse else 0
    acum_t = acum_all.T
    a_even = acum_t[do:do + NPAIR, :]
    a_odd = acum_t[do + NPAIR:do + H_B, :]
    lane16 = lax.broadcasted_iota(jnp.int32, (NPAIR, CHUNK), 1) < HALF
    rv1 = jnp.where(lane16, a_even, pltpu.roll(a_odd, HALF, 1))
    rv2 = jnp.where(lane16, pltpu.roll(a_even, HALF, 1), a_odd)

    if finalize:
        bmat = bcc_ref[0, rows, 0:GP_B]
        cmat = bcc_ref[0, rows, GP_B:2 * GP_B]
        ssq = jnp.zeros((CHUNK, CHUNK), F32)
    else:
        bmat = _conv_block(bc_ref, bcp_ref, bcn_ref, sc, sub, slice(0, GP_B), has_prev, has_next,
                           cw_ref[:, W_B:W_B + GP_B], cb_ref[:, W_B:W_B + GP_B]).astype(BF16)
        cmat = _conv_block(bc_ref, bcp_ref, bcn_ref, sc, sub, slice(GP_B, 2 * GP_B), has_prev, has_next,
                           cw_ref[:, W_B + GP_B:W_B + 2 * GP_B],
                           cb_ref[:, W_B + GP_B:W_B + 2 * GP_B]).astype(BF16)
        xc_ref[0, rows, W_B:W_B + GP_B] = bmat
        xc_ref[0, rows, W_B + GP_B:W_B + 2 * GP_B] = cmat

    for g in range(G_B):
        gs = slice(g * GP_B, (g + 1) * GP_B)
        if finalize:
            xg = xc_ref[0, rows, gs].astype(F32)
        else:
            xr, xpr, xnr = (x0_ref, x0p_ref, x0n_ref) if g < 2 else (x1_ref, x1p_ref, x1n_ref)
            cs = slice((g % 2) * GP_B, (g % 2 + 1) * GP_B)
            xg = _conv_block(xr, xpr, xnr, sc, sub, cs, has_prev, has_next, cw_ref[:, gs], cb_ref[:, gs])
            xc_ref[0, rows, gs] = xg.astype(BF16)
        bg = bmat[:, g * N_B:(g + 1) * N_B]
        cg = cmat[:, g * N_B:(g + 1) * N_B]
        cb1 = _dot_nt(cg, jnp.concatenate([bg[0:HALF], bg[0:HALF]], axis=0))
        cb2 = _dot_nt(cg, jnp.concatenate([bg[HALF:], bg[HALF:]], axis=0))
        s_prev = sf_ref[0, g]
        cs_prev = _dot(cg, s_prev.astype(BF16))
        y_parts, xw_parts, tot_parts = [], [], []
        for pp in range(R_B // 2):
            pr = g * (R_B // 2) + pp
            ps = slice(pp * CHUNK, (pp + 1) * CHUNK)
            idx = jnp.where(lo_half, do + pr, do + NPAIR + pr)
            acum_b = jnp.take_along_axis(acum_all, idx, axis=1)
            dt_b = jnp.take_along_axis(dt_all, idx, axis=1)
            xdt = xg[:, ps] * dt_b
            xdtb = xdt.astype(BF16)
            w1 = cb1 * jnp.exp2(jnp.where(mask1, acum_b - rv1[pr:pr + 1, :], -jnp.inf))
            w2 = cb2 * jnp.exp2(jnp.where(mask2, acum_b - rv2[pr:pr + 1, :], -jnp.inf))
            w12 = jnp.concatenate([w1.astype(BF16), w2.astype(BF16)], axis=1)
            x12 = jnp.concatenate([xdtb[0:HALF], xdtb[0:HALF], xdtb[HALF:], xdtb[HALF:]], axis=0)
            x12 = jnp.where(blockdiag, x12, jnp.zeros_like(x12))
            tot = acum_b[edge:edge + 1, :]
            y_parts.append(_dot(w12, x12) + jnp.exp2(acum_b) * cs_prev[:, ps])
            xw_parts.append((xdt * jnp.exp2(tot - acum_b)).astype(BF16))
            tot_parts.append(tot)
        xw = jnp.concatenate(xw_parts, axis=1)
        etot = jnp.exp2(jnp.concatenate(tot_parts, axis=1))
        sf_ref[0, g] = etot * s_prev + _dot_tn(bg, xw)
        y_grp = jnp.concatenate(y_parts, axis=1)
        if finalize:
            z_ref = z0_ref if g < 2 else z1_ref
            zs = slice((g % 2) * GP_B, (g % 2 + 1) * GP_B)
            y = y_grp + yb_ref[0, rows, gs].astype(F32) + dsk_ref[:, gs] * xg
            y = y * _silu(z_ref[0, rows, zs].astype(F32))
            y_s[:, gs] = y
            y2 = y * y
            ssq = ssq + (y2[:, 0:CHUNK] + y2[:, CHUNK:2 * CHUNK]
                         + y2[:, 2 * CHUNK:3 * CHUNK] + y2[:, 3 * CHUNK:])
        else:
            out_ref[0, rows, gs] = y_grp.astype(out_ref.dtype)

    if finalize:
        inv = lax.rsqrt(jnp.sum(ssq, axis=-1, keepdims=True) * (1.0 / W_B) + LN_EPS)
        for half in range(2):
            cs = slice(half * CW, (half + 1) * CW)
            out_ref[0, rows, cs] = (y_s[:, cs] * inv * nw_ref[:, cs]).astype(out_ref.dtype)


def _ssd_call(reverse, finalize, s0, *, small3, dtb128, alog128, p3, pcol0=0, conv_w=None,
              conv_b=None, xc=None, yb=None, d_skip=None, norm_w=None):
    bsz, t_len, _ = p3.shape
    c_x, c_bc, c_zb = COL_X + pcol0, COL_BC + pcol0, COL_ZB + pcol0
    sub = math.gcd(SUB_SSD, t_len // CHUNK)
    nc, main, prev, nxt, const2 = _chunk_specs(t_len, reverse, sub)
    state_spec = pl.BlockSpec((1, G_B, N_B, GP_B), lambda b, i: (b, 0, 0, 0))
    state_shape = jax.ShapeDtypeStruct(s0.shape, F32)
    xc_spec = main(0, W_B + 2 * GP_B)
    if finalize:
        in_specs = [main(0, W_B), main(2, CW), main(0, N_SMALL), const2((1, N_SMALL)),
                    const2((1, N_SMALL)), state_spec,
                    main(0, W_B), main(c_zb, CW), main(c_zb + 1, CW),
                    const2((1, W_B)), const2((1, W_B))]
        args = [xc, xc, small3, dtb128, alog128, s0, yb, p3, p3, d_skip, norm_w]
        out_specs = [main(0, W_B), state_spec]
        out_shape = [jax.ShapeDtypeStruct((bsz, t_len, W_B), BF16), state_shape]
        scratch = [pltpu.VMEM((CHUNK, W_B), F32)]
    else:
        in_specs = [main(c_x, CW), prev(c_x), nxt(c_x),
                    main(c_x + 1, CW), prev(c_x + 1), nxt(c_x + 1),
                    main(c_bc, CW), prev(c_bc), nxt(c_bc),
                    main(0, N_SMALL),
                    const2((3, 3 * CW)), const2((1, 3 * CW)), const2((1, N_SMALL)),
                    const2((1, N_SMALL)), state_spec]
        args = [p3] * 9 + [small3, conv_w, conv_b, dtb128, alog128, s0]
        out_specs = [main(0, W_B), xc_spec, state_spec]
        out_shape = [jax.ShapeDtypeStruct((bsz, t_len, W_B), HB_DTYPE),
                     jax.ShapeDtypeStruct((bsz, t_len, W_B + 2 * GP_B), BF16), state_shape]
        scratch = []
    return pl.pallas_call(
        functools.partial(_ssd_kernel, reverse, finalize, sub),
        grid=(bsz, nc), in_specs=in_specs, out_specs=out_specs, out_shape=out_shape,
        scratch_shapes=scratch,
        compiler_params=_cparams(("arbitrary", "arbitrary")),
        name="ssd_" + ("bwd" if reverse else "fwd"),
    )(*args)


def _to_colmajor(t, bsz):
    ch = t.shape[-1]
    return t.reshape(bsz, -1, GRID_W, ch).transpose(0, 2, 1, 3).reshape(-1, ch)


def _from_colmajor(t, bsz):
    ch = t.shape[-1]
    return t.reshape(bsz, GRID_W, -1, ch).transpose(0, 2, 1, 3).reshape(-1, ch)


def _mixer_scans(pa, sa, pb, sb, bsz, t_len, lp, states, pcol0=0):
    st_mf, st_mb, st_sf, st_sb = states
    pa = pa.reshape(bsz, t_len, pa.shape[-1])
    sa = sa.reshape(bsz, t_len, N_SMALL)
    h_b, qk, st_mb = _mlstm_call(True, False, st_mb, p3=pa, small3=sa, gate_b128=lp["gate_b128"],
                                 conv_w=lp["conv_qk_w"], conv_b=lp["conv_qk_b"])
    y_a, st_mf = _mlstm_call(False, True, st_mf, p3=pa, small3=sa, gate_b128=lp["gate_b128"],
                             qk=qk, hb=h_b, mh_w=lp["mh_w"])

    pb = pb.reshape(bsz, t_len, pb.shape[-1])
    sb = sb.reshape(bsz, t_len, N_SMALL)
    y_bb, xc, st_sb = _ssd_call(True, False, st_sb, small3=sb, dtb128=lp["dtb128"],
                                alog128=lp["alog128"], p3=pb, pcol0=pcol0,
                                conv_w=lp["conv_xbc_w"], conv_b=lp["conv_xbc_b"])
    y_b, st_sf = _ssd_call(False, True, st_sf, small3=sb, dtb128=lp["dtb128"],
                           alog128=lp["alog128"], p3=pb, pcol0=pcol0, xc=xc, yb=y_bb,
                           d_skip=lp["d_skip"], norm_w=lp["ssm_w"])
    return y_a, y_b, (st_mf, st_mb, st_sf, st_sb)


def _mixer(x2d, bsz, t_len, latent, shift, scale, mod_index, tm, tm_b, w, lp, states,
           cast_layer=None):
    n_a = len(A_STARTS)
    if latent:
        pa, sa, u2d = _in_proj_call(x2d, shift, scale, w, A_STARTS, lp["w_small_a"], mod_index, tm,
                                    tn=2 * CW)
        pb, sb = _proj_call(_to_colmajor(u2d, bsz), w, B_STARTS, lp["w_small_b"], tm_b, tile0=n_a)
        pcol0 = 0
    else:
        w_small = jnp.concatenate([lp["w_small_a"], lp["w_small_b"]], axis=0)
        p, small, _, w = _in_proj_call(x2d, shift, scale, w, A_STARTS + B_STARTS, w_small,
                                       mod_index, tm, cast_layer)
        pa, pb, pcol0 = p, p, n_a
        sa, sb = small[:, :N_SMALL], small[:, N_SMALL:]
    y_a, y_b, states = _mixer_scans(pa, sa, pb, sb, bsz, t_len, lp, states, pcol0)
    y_b = y_b.reshape(bsz * t_len, W_B)
    if latent:
        y_b = _from_colmajor(y_b, bsz)
    return y_a.reshape(bsz * t_len, W_A), y_b, states, w


def _zero_states(bsz):
    m_state = (jnp.zeros((bsz, NH_A, DQK_A, DV_A + CHUNK), F32), jnp.zeros((bsz, 1, 128), F32))
    s_state = jnp.zeros((bsz, G_B, N_B, GP_B), F32)
    return (m_state, m_state, s_state, s_state)


def _layer_params(l, w_t, conv_qk_w, conv_qk_b, gate_b, mh_norm_w, conv_xbc_w, conv_xbc_b,
                  dt_bias, a_log, d_skip, ssm_norm_w):
    feats = lambda a, n: lax.slice(w_t, (l, a, 0), (l + 1, a + n, w_t.shape[2]))[0]

    def pad_cols(a):
        return jnp.concatenate([a, jnp.zeros((a.shape[0], N_SMALL - a.shape[1]), a.dtype)], axis=1)

    def pad_rows(a):
        return jnp.concatenate([a, jnp.zeros((N_SMALL - a.shape[0], a.shape[1]), a.dtype)], axis=0)

    def even_odd(a):
        a4 = a.reshape(a.shape[0], 2, NPAIR, 2)
        return a4.transpose(0, 1, 3, 2).reshape(a.shape[0], 2 * H_B)

    return {
        "w_small_a": pad_rows(feats(O_GATE, 4 * NH_A)),
        "w_small_b": pad_rows(even_odd(feats(O_DT, 2 * H_B).T).T),
        "conv_qk_w": conv_qk_w[l], "conv_qk_b": conv_qk_b[l].reshape(1, -1),
        "gate_b128": pad_cols(gate_b[l].reshape(1, -1)),
        "mh_w": mh_norm_w[l].reshape(1, W_A),
        "conv_xbc_w": conv_xbc_w[l], "conv_xbc_b": conv_xbc_b[l].reshape(1, -1),
        "dtb128": pad_cols(even_odd(dt_bias[l].reshape(1, -1))),
        "alog128": pad_cols(even_odd(a_log[l].reshape(1, -1))),
        "d_skip": jnp.repeat(d_skip[l], P_B).reshape(1, W_B),
        "ssm_w": ssm_norm_w[l].reshape(1, W_B),
    }


def kernel(x, c, ctx, c_ctx, w_ada, b_ada, w_in, conv_qk_w, conv_qk_b, gate_b, mh_norm_w,
           conv_xbc_w, conv_xbc_b, dt_bias, a_log, d_skip, ssm_norm_w, w_out, ln_g, ln_b):
    bsz, t_len, d = x.shape
    ctx_len = ctx.shape[1]
    depth = w_in.shape[0]
    assert d == D_MODEL and depth == DEPTH and bsz + 1 <= 8
    assert t_len % (CHUNK * GRID_W) == 0 and ctx_len % CHUNK == 0

    cvec = jnp.concatenate([c, c_ctx[None, :]], axis=0)
    mod = _ada_call(jnp.broadcast_to(cvec[:, :, None], (bsz + 1, d, 128)), w_ada, b_ada)

    w_t = jnp.swapaxes(w_in, 1, 2)
    w_out_bf = _cast_bf16(w_out, 2048)

    x2d = x.reshape(bsz * t_len, d)
    xc2d = ctx.reshape(bsz * ctx_len, d)
    tm_in, tm_ssd, tm_out, tm_ctx_in, tm_ctx_out = 1024, 2048, 512, bsz * ctx_len, 256
    lat_in = lambda i: i // (t_len // tm_in)
    lat_out = lambda i: i // (t_len // tm_out)
    ctx_index = lambda i: bsz

    for l in range(depth):
        last = l == depth - 1
        lp = _layer_params(l, w_t, conv_qk_w, conv_qk_b, gate_b, mh_norm_w, conv_xbc_w,
                           conv_xbc_b, dt_bias, a_log, d_skip, ssm_norm_w)
        shift = mod[l, :, 0:d].reshape(8, 1, d)
        scale = mod[l, :, d:2 * d].reshape(8, 1, d)
        gate = mod[l, :, 2 * d:3 * d].reshape(8, 1, d)

        yc_a, yc_b, ctx_states, w_bf = _mixer(xc2d, bsz, ctx_len, False, shift, scale, ctx_index,
                                              tm_ctx_in, tm_ctx_in, w_t, lp, _zero_states(bsz),
                                              cast_layer=l)
        y_a, y_b, _, _ = _mixer(x2d, bsz, t_len, True, shift, scale, lat_in, tm_in, tm_ssd, w_bf,
                                lp, ctx_states)
        x2d = _out_proj_call(y_a, y_b, x2d, gate, w_out_bf, l, ln_g[l], ln_b[l], lat_out, tm_out)
        if not last:
            xc2d = _out_proj_call(yc_a, yc_b, xc2d, gate, w_out_bf, l, ln_g[l], ln_b[l],
                                  ctx_index, tm_ctx_out)
    return x2d.reshape(bsz, t_len, d)
```

```python
import functools
import math

import jax
import jax.numpy as jnp
from jax import lax
from jax.experimental import pallas as pl
from jax.experimental.pallas import tpu as pltpu

F32 = jnp.float32
BF16 = jnp.bfloat16

D_MODEL = 2048
DEPTH = 2
GRID_W = 64
NH_A = 8
DV_A = 256
DQK_A = 128
W_A = NH_A * DV_A
QK_A = NH_A * DQK_A
P_B = 64
H_B = 32
G_B = 4
R_B = H_B // G_B
N_B = 128
W_B = H_B * P_B
GP_B = R_B * P_B
NPAIR = H_B // 2
CHUNK = 128
HALF = CHUNK // 2
SUB_MLSTM = 1
HEAD_GROUP_BWD = 4
HEAD_GROUP_FWD = 8
SUB_SSD = 8
LN_EPS = 1e-5
DEEPNORM_ALPHA = (2 * DEPTH) ** 0.25
LOG2E = 1.4426950408889634

CW = 1024
COL_Q, COL_K, COL_V, COL_O, COL_ZA = 0, 1, 2, 4, 6
N_MAIN_A = 8 * CW
COL_X, COL_BC, COL_ZB = 0, 2, 3
N_SMALL = 128
O_GATE = N_MAIN_A
O_XBC = O_GATE + 4 * NH_A
O_DT = O_XBC + W_B + 2 * G_B * N_B
O_ZB = O_DT + 2 * H_B
A_STARTS = [k * CW for k in range(N_MAIN_A // CW)]
B_STARTS = ([O_XBC + k * CW for k in range((W_B + 2 * G_B * N_B) // CW)]
            + [O_ZB + k * CW for k in range(W_B // CW)])

HALO = 16
VMEM_LIMIT = 62 * 1024 * 1024

PROJ_DTYPE = BF16
HB_DTYPE = BF16


def _cparams(sem):
    return pltpu.CompilerParams(dimension_semantics=sem, vmem_limit_bytes=VMEM_LIMIT)


def _softplus(x):
    return jnp.maximum(x, 0.0) + jnp.log1p(jnp.exp(-jnp.abs(x)))


def _log_sigmoid(x):
    return jnp.minimum(x, 0.0) - jnp.log1p(jnp.exp(-jnp.abs(x)))


def _silu(x):
    return x * jax.nn.sigmoid(x)


def _dot(a, b):
    return jnp.dot(a, b, preferred_element_type=F32)


def _dot_nt(a, b):
    return lax.dot_general(a, b, (((1,), (1,)), ((), ())), preferred_element_type=F32)


def _dot_tn(a, b):
    return lax.dot_general(a, b, (((0,), (0,)), ((), ())), preferred_element_type=F32)


def _cumsum_rows(tri, x):
    return jnp.dot(tri, x, preferred_element_type=F32, precision=lax.Precision.HIGHEST)


def _ada_kernel(c_ref, w_ref, b_ref, o_ref):
    n_rows, d, _ = c_ref.shape
    tn = w_ref.shape[2]
    o_ref[...] = jnp.zeros(o_ref.shape, F32)
    for r in range(n_rows):
        s = _silu(c_ref[r]).reshape(d // 8, 8, 128)
        for cb in range(tn // 128):
            cs = slice(cb * 128, (cb + 1) * 128)
            acc = jnp.sum(w_ref[0, :, cs].reshape(d // 8, 8, 128) * s, axis=0)
            o_ref[0, r:r + 1, cs] = jnp.sum(acc, axis=0, keepdims=True) + b_ref[0, :, cs]


def _ada_call(crep, w_ada, b_ada):
    depth, d, n3 = w_ada.shape
    n_rows = crep.shape[0]
    tn = 1024
    return pl.pallas_call(
        _ada_kernel,
        grid=(depth, n3 // tn),
        in_specs=[pl.BlockSpec((n_rows, d, 128), lambda l, j: (0, 0, 0)),
                  pl.BlockSpec((1, d, tn), lambda l, j: (l, 0, j)),
                  pl.BlockSpec((1, 1, tn), lambda l, j: (l, 0, j))],
        out_specs=pl.BlockSpec((1, 8, tn), lambda l, j: (l, 0, j)),
        out_shape=jax.ShapeDtypeStruct((depth, 8, n3), F32),
        compiler_params=_cparams(("arbitrary", "arbitrary")),
        name="adaln_mod",
    )(crep, w_ada, b_ada.reshape(depth, 1, n3))


def _cast_kernel(x_ref, o_ref):
    o_ref[...] = x_ref[...].astype(o_ref.dtype)


def _cast_bf16(w3, tr):
    depth, rows, cols = w3.shape
    spec = pl.BlockSpec((1, tr, cols), lambda l, i: (l, i, 0))
    return pl.pallas_call(
        _cast_kernel, grid=(depth, pl.cdiv(rows, tr)), in_specs=[spec], out_specs=spec,
        out_shape=jax.ShapeDtypeStruct(w3.shape, BF16),
        compiler_params=_cparams(("arbitrary", "arbitrary")),
        name="cast_bf16",
    )(w3)


def _proj_tile(cast_w, u_ref, w_ref, p_ref, wout):
    if cast_w:
        w = w_ref[0].astype(BF16)
        wout[0][...] = w
    else:
        w = w_ref[...]
    p_ref[...] = _dot_nt(u_ref[...], w).astype(p_ref.dtype)


def _in_proj_kernel(cast_w, x_ref, shift_ref, scale_ref, w_ref, ws_ref, p_ref, small_ref, u_ref,
                    *wout):
    @pl.when(pl.program_id(1) == 0)
    def _():
        x = x_ref[...]
        mu = jnp.mean(x, axis=-1, keepdims=True)
        xc = x - mu
        var = jnp.mean(xc * xc, axis=-1, keepdims=True)
        u = xc * lax.rsqrt(var + LN_EPS) * (1.0 + scale_ref[0]) + shift_ref[0]
        ub = u.astype(BF16)
        u_ref[...] = ub
        small_ref[...] = _dot_nt(ub, ws_ref[...].astype(BF16))

    _proj_tile(cast_w, u_ref, w_ref, p_ref, wout)


def _proj_kernel(u_ref, w_ref, ws_ref, p_ref, small_ref):
    @pl.when(pl.program_id(1) == 0)
    def _():
        small_ref[...] = _dot_nt(u_ref[...], ws_ref[...].astype(BF16))

    _proj_tile(False, u_ref, w_ref, p_ref, ())


def _weight_specs(w, starts, layer, tile0, rows, tm, tn, d):
    if layer is None:
        return False, pl.BlockSpec((tn, d), lambda i, j: (tile0 + j, 0)), [], []
    assert rows == tm and tile0 == 0

    align = functools.reduce(math.gcd, starts, tn)

    def start(j):
        row = starts[0]
        for k in range(1, len(starts)):
            row = jnp.where(j >= k, starts[k], row)
        return pl.multiple_of(row, align)

    spec = pl.BlockSpec((pl.Element(1), pl.Element(tn), pl.Element(d)),
                        lambda i, j: (layer, start(j), 0))
    return (True, spec, [pl.BlockSpec((tn, d), lambda i, j: (j, 0))],
            [jax.ShapeDtypeStruct((len(starts) * tn, d), BF16)])


def _in_proj_call(x2d, shift, scale, w, starts, w_small, mod_index, tm, layer=None, tile0=0,
                  tn=CW):
    rows, d = x2d.shape
    n_main = len(starts) * CW
    assert layer is None or tn == CW
    n_small = w_small.shape[0]
    cast_w, w_spec, wout_specs, wout_shapes = _weight_specs(w, starts, layer, tile0, rows, tm,
                                                            tn, d)
    return pl.pallas_call(
        functools.partial(_in_proj_kernel, cast_w),
        grid=(rows // tm, n_main // tn),
        in_specs=[pl.BlockSpec((tm, d), lambda i, j: (i, 0)),
                  pl.BlockSpec((1, 1, d), lambda i, j: (mod_index(i), 0, 0)),
                  pl.BlockSpec((1, 1, d), lambda i, j: (mod_index(i), 0, 0)),
                  w_spec,
                  pl.BlockSpec((n_small, d), lambda i, j: (0, 0))],
        out_specs=[pl.BlockSpec((tm, tn), lambda i, j: (i, j)),
                   pl.BlockSpec((tm, n_small), lambda i, j: (i, 0)),
                   pl.BlockSpec((tm, d), lambda i, j: (i, 0))] + wout_specs,
        out_shape=[jax.ShapeDtypeStruct((rows, n_main), PROJ_DTYPE),
                   jax.ShapeDtypeStruct((rows, n_small), F32),
                   jax.ShapeDtypeStruct((rows, d), BF16)] + wout_shapes,
        compiler_params=_cparams(("arbitrary", "arbitrary")),
        name="in_proj",
    )(x2d, shift, scale, w, w_small)


def _proj_call(u2d, w, starts, w_small, tm, tile0=0):
    rows, d = u2d.shape
    tn = CW
    n_main = len(starts) * tn
    _, w_spec, _, _ = _weight_specs(w, starts, None, tile0, rows, tm, tn, d)
    return pl.pallas_call(
        _proj_kernel,
        grid=(rows // tm, n_main // tn),
        in_specs=[pl.BlockSpec((tm, d), lambda i, j: (i, 0)),
                  w_spec,
                  pl.BlockSpec((N_SMALL, d), lambda i, j: (0, 0))],
        out_specs=[pl.BlockSpec((tm, tn), lambda i, j: (i, j)),
                   pl.BlockSpec((tm, N_SMALL), lambda i, j: (i, 0))],
        out_shape=[jax.ShapeDtypeStruct((rows, n_main), PROJ_DTYPE),
                   jax.ShapeDtypeStruct((rows, N_SMALL), F32)],
        compiler_params=_cparams(("arbitrary", "arbitrary")),
        name="ssd_proj",
    )(u2d, w, w_small)


def _out_proj_kernel(ya_ref, yb_ref, x_ref, gate_ref, wa_ref, wb_ref, g_ref, b_ref, o_ref):
    acc = _dot(ya_ref[...], wa_ref[0]) + _dot(yb_ref[...], wb_ref[0])
    r = DEEPNORM_ALPHA * x_ref[...] + gate_ref[0] * acc
    mu = jnp.mean(r, axis=-1, keepdims=True)
    rc = r - mu
    var = jnp.mean(rc * rc, axis=-1, keepdims=True)
    o_ref[...] = rc * lax.rsqrt(var + LN_EPS) * g_ref[...] + b_ref[...]


def _out_proj_call(ya, yb, x2d, gate, w_out3, layer, ln_g, ln_b, mod_index, tm):
    rows, d = x2d.shape
    const = lambda i: (0, 0)
    return pl.pallas_call(
        _out_proj_kernel,
        grid=(rows // tm,),
        in_specs=[pl.BlockSpec((tm, W_A), lambda i: (i, 0)),
                  pl.BlockSpec((tm, W_B), lambda i: (i, 0)),
                  pl.BlockSpec((tm, d), lambda i: (i, 0)),
                  pl.BlockSpec((1, 1, d), lambda i: (mod_index(i), 0, 0)),
                  pl.BlockSpec((1, W_A, d), lambda i: (layer, 0, 0)),
                  pl.BlockSpec((1, W_B, d), lambda i: (layer, 1, 0)),
                  pl.BlockSpec((1, d), const),
                  pl.BlockSpec((1, d), const)],
        out_specs=pl.BlockSpec((tm, d), lambda i: (i, 0)),
        out_shape=jax.ShapeDtypeStruct((rows, d), F32),
        compiler_params=_cparams(("arbitrary",)),
        name="out_proj",
    )(ya, yb, x2d, gate, w_out3, w_out3, ln_g.reshape(1, d), ln_b.reshape(1, d))


def _conv_silu(x, prev_row, next_row, w, bias):
    rows = x.shape[0]
    row = lax.broadcasted_iota(jnp.int32, x.shape, 0)
    xm = jnp.where(row == 0, prev_row, pltpu.roll(x, 1, 0))
    xp = jnp.where(row == rows - 1, next_row, pltpu.roll(x, rows - 1, 0))
    y = w[0:1] * xm + w[1:2] * x + w[2:3] * xp + bias
    return _silu(y)


def _conv_block(x_ref, p_ref, n_ref, sc, sub, cs, has_prev, has_next, w, bias):
    r0 = sc * CHUNK
    if sc > 0:
        pr = x_ref[0, r0 - 1:r0, cs].astype(F32)
    else:
        pr = jnp.where(has_prev, p_ref[0, HALO - 1:HALO, cs].astype(F32), 0.0)
    if sc < sub - 1:
        nx = x_ref[0, r0 + CHUNK:r0 + CHUNK + 1, cs].astype(F32)
    else:
        nx = jnp.where(has_next, n_ref[0, 0:1, cs].astype(F32), 0.0)
    return _conv_silu(x_ref[0, r0:r0 + CHUNK, cs].astype(F32), pr, nx, w, bias)


def _tri_masks(reverse):
    t = lax.broadcasted_iota(jnp.int32, (CHUNK, CHUNK), 0)
    s = lax.broadcasted_iota(jnp.int32, (CHUNK, CHUNK), 1)
    mask = (s >= t) if reverse else (s <= t)
    mask_t = (t >= s) if reverse else (t <= s)
    return mask, mask.astype(F32), mask_t.astype(F32)


def _chunk_pos(reverse):
    i = pl.program_id(1)
    nc = pl.num_programs(1)
    j = (nc - 1 - i) if reverse else i
    return i, j, nc


def _chunk_specs(t_len, reverse, sub):
    blk = sub * CHUNK
    nc = t_len // blk
    nhb = t_len // HALO
    hpc = blk // HALO

    def jj(i):
        return (nc - 1 - i) if reverse else i

    def main(col, width):
        return pl.BlockSpec((1, blk, width), lambda b, i: (b, jj(i), (col * CW) // width))

    def prev(col):
        return pl.BlockSpec((1, HALO, CW), lambda b, i: (b, jnp.maximum(jj(i) * hpc - 1, 0), col))

    def nxt(col):
        return pl.BlockSpec((1, HALO, CW),
                            lambda b, i: (b, jnp.minimum((jj(i) + 1) * hpc, nhb - 1), col))

    def const2(shape):
        return pl.BlockSpec(shape, lambda b, i: (0, 0))

    return nc, main, prev, nxt, const2


def _mlstm_kernel(reverse, finalize, *refs):
    if finalize:
        (qk_ref, v_ref, gs_ref, gb_ref, c0_ref, m0_ref, hb_ref, o_ref, z_ref, mhw_ref,
         out_ref, cf_ref, mf_ref) = refs
    else:
        (q_ref, qp_ref, qn_ref, k_ref, kp_ref, kn_ref, v_ref, gs_ref, cw_ref, cb_ref, gb_ref,
         c0_ref, m0_ref, out_ref, qk_ref, cf_ref, mf_ref) = refs

    i, j, nc = _chunk_pos(reverse)

    @pl.when(i == 0)
    def _():
        cf_ref[...] = c0_ref[...]
        mf_ref[...] = m0_ref[...]

    has_prev = j > 0
    has_next = j < nc - 1
    for sc in (reversed(range(SUB_MLSTM)) if reverse else range(SUB_MLSTM)):
        _mlstm_chunk(reverse, finalize, sc, has_prev, has_next, refs)


def _mlstm_chunk(reverse, finalize, sc, has_prev, has_next, refs):
    if finalize:
        (qk_ref, v_ref, gs_ref, gb_ref, c0_ref, m0_ref, hb_ref, o_ref, z_ref, mhw_ref,
         out_ref, cf_ref, mf_ref) = refs
    else:
        (q_ref, qp_ref, qn_ref, k_ref, kp_ref, kn_ref, v_ref, gs_ref, cw_ref, cb_ref, gb_ref,
         c0_ref, m0_ref, out_ref, qk_ref, cf_ref, mf_ref) = refs
    rows = slice(sc * CHUNK, (sc + 1) * CHUNK)
    mask, tri, tri_t = _tri_masks(reverse)
    edge = 0 if reverse else CHUNK - 1

    gb = gs_ref[0, rows, :] + gb_ref[...]
    gbt = gb.T
    io = 2 * NH_A if reverse else 0
    b_all = _cumsum_rows(tri, LOG2E * _log_sigmoid(gb))
    li_c = LOG2E * gb[:, io:io + NH_A]
    b_c = b_all[:, io + NH_A:io + 2 * NH_A]
    b_r = jnp.dot(LOG2E * _log_sigmoid(gbt[io + NH_A:io + 2 * NH_A, :]), tri_t,
                  preferred_element_type=F32, precision=lax.Precision.HIGHEST)
    c_r = LOG2E * gbt[io:io + NH_A, :] - b_r

    m_prev = mf_ref[0][:, 0:NH_A]
    g_c = b_c[edge:edge + 1, :]
    a_c = g_c - b_c + li_c
    m_new = jnp.maximum(g_c + m_prev, jnp.max(a_c, axis=0, keepdims=True))
    w_c = jnp.exp2(a_c - m_new)
    dec = jnp.exp2(g_c + m_prev - m_new)

    ksl = [slice(h * DQK_A, (h + 1) * DQK_A) for h in range(NH_A)]
    vsl = [slice(h * DV_A, (h + 1) * DV_A) for h in range(NH_A)]
    kksl = [slice(QK_A + h * DQK_A, QK_A + (h + 1) * DQK_A) for h in range(NH_A)]
    mf_ref[0, :, 0:NH_A] = m_new

    def run_heads(heads):
        if finalize:
            qb = {h: qk_ref[0, rows, ksl[h]] for h in heads}
            kb = {h: qk_ref[0, rows, kksl[h]] for h in heads}
            qh = {h: qb[h].astype(F32) for h in heads}
            kh = {h: kb[h].astype(F32) for h in heads}
            gate = {}
            for h in heads:
                o = o_ref[0, rows, vsl[h]].astype(F32)
                z = z_ref[0, rows, vsl[h]].astype(F32)
                gate[h] = z / ((1.0 + jnp.exp(-o)) * (1.0 + jnp.exp(-z)))
        else:
            qh = {h: _conv_block(q_ref, qp_ref, qn_ref, sc, SUB_MLSTM, ksl[h], has_prev, has_next,
                              cw_ref[:, ksl[h]], cb_ref[:, ksl[h]]) * (DQK_A ** -0.5) for h in heads}
            kh = {h: _conv_block(k_ref, kp_ref, kn_ref, sc, SUB_MLSTM, ksl[h], has_prev, has_next,
                              cw_ref[:, kksl[h]], cb_ref[:, kksl[h]]) for h in heads}
            qb = {h: qh[h].astype(BF16) for h in heads}
            kb = {h: kh[h].astype(BF16) for h in heads}
            for h in heads:
                qk_ref[0, rows, ksl[h]] = qb[h]
                qk_ref[0, rows, kksl[h]] = kb[h]
        ones_l = jnp.ones((CHUNK, CHUNK), BF16)
        vb = {h: jnp.concatenate([v_ref[0, rows, vsl[h]].astype(BF16), ones_l], axis=1)
              for h in heads}
        c_prev = {h: cf_ref[0, h] for h in heads}
        mp = {h: m_prev[:, h:h + 1] for h in heads}

        s_qk = {h: _dot_nt(qb[h], kb[h]) for h in heads}
        q_c = {h: _dot(qb[h], c_prev[h].astype(BF16)) for h in heads}
        cm = {h: jnp.where(mask, c_r[h:h + 1, :], -jnp.inf) for h in heads}
        mm = {h: jnp.broadcast_to(jnp.maximum(jnp.max(cm[h], axis=1, keepdims=True), mp[h]),
                               (CHUNK, CHUNK)) for h in heads}

        kw = {h: kh[h] * w_c[:, h:h + 1] for h in heads}
        for h in heads:
            cf_ref[0, h] = dec[:, h:h + 1] * c_prev[h] + _dot_tn(kw[h].astype(BF16), vb[h])

        p = {h: s_qk[h] * jnp.exp2(cm[h] - mm[h]) for h in heads}
        p_v = {h: _dot(p[h].astype(BF16), vb[h]) for h in heads}
        w_int = {h: jnp.exp2(mp[h] - mm[h]) for h in heads}
        den = {h: w_int[h] * q_c[h][:, DV_A:] + p_v[h][:, DV_A:] for h in heads}
        b_t = {h: jnp.broadcast_to(b_c[:, h:h + 1], (CHUNK, CHUNK)) for h in heads}
        inv = {h: 1.0 / jnp.maximum(jnp.abs(den[h]), jnp.exp2(-(b_t[h] + mm[h]))) for h in heads}
        hval = {h: jnp.concatenate(
            [(w_int[h] * q_c[h][:, c * CHUNK:(c + 1) * CHUNK] + p_v[h][:, c * CHUNK:(c + 1) * CHUNK])
             * inv[h] for c in range(DV_A // CHUNK)], axis=1) for h in heads}

        if not finalize:
            for h in heads:
                out_ref[0, rows, vsl[h]] = hval[h].astype(out_ref.dtype)
            return

        hs = {h: hval[h] + hb_ref[0, rows, vsl[h]].astype(F32) for h in heads}
        mu = {h: jnp.mean(hs[h], axis=-1, keepdims=True) for h in heads}
        hc = {h: hs[h] - mu[h] for h in heads}
        var = {h: jnp.mean(hc[h] * hc[h], axis=-1, keepdims=True) for h in heads}
        for h in heads:
            hn = hc[h] * lax.rsqrt(var[h] + LN_EPS) * mhw_ref[:, vsl[h]]
            out_ref[0, rows, vsl[h]] = (hn * gate[h]).astype(out_ref.dtype)

    group = HEAD_GROUP_FWD if finalize else HEAD_GROUP_BWD
    for g0 in range(0, NH_A, group):
        run_heads(range(g0, g0 + group))


def _mlstm_call(reverse, finalize, state, *, p3, small3, gate_b128, conv_w=None, conv_b=None,
                qk=None, hb=None, mh_w=None):
    bsz, t_len, _ = p3.shape
    nc, main, prev, nxt, const2 = _chunk_specs(t_len, reverse, SUB_MLSTM)
    c0, m0 = state
    state_specs = [pl.BlockSpec((1, NH_A, DQK_A, DV_A + CHUNK), lambda b, i: (b, 0, 0, 0)),
                   pl.BlockSpec((1, 1, 128), lambda b, i: (b, 0, 0))]
    state_shapes = [jax.ShapeDtypeStruct(c0.shape, F32), jax.ShapeDtypeStruct(m0.shape, F32)]
    wide = main(0, W_A)
    if finalize:
        in_specs = [wide, main(COL_V, W_A), main(0, N_SMALL), const2((1, N_SMALL))] + state_specs
        in_specs += [wide, main(COL_O, W_A), main(COL_ZA, W_A), const2((1, W_A))]
        args = [qk, p3, small3, gate_b128, c0, m0, hb, p3, p3, mh_w]
        out_specs = [wide] + state_specs
        out_shape = [jax.ShapeDtypeStruct((bsz, t_len, W_A), BF16)] + state_shapes
    else:
        in_specs = [main(COL_Q, CW), prev(COL_Q), nxt(COL_Q),
                    main(COL_K, CW), prev(COL_K), nxt(COL_K),
                    main(COL_V, W_A), main(0, N_SMALL),
                    const2((3, 2 * QK_A)), const2((1, 2 * QK_A)), const2((1, N_SMALL))] + state_specs
        args = [p3, p3, p3, p3, p3, p3, p3, small3, conv_w, conv_b, gate_b128, c0, m0]
        out_specs = [wide, wide] + state_specs
        out_shape = [jax.ShapeDtypeStruct((bsz, t_len, W_A), HB_DTYPE),
                     jax.ShapeDtypeStruct((bsz, t_len, 2 * QK_A), BF16)] + state_shapes
    res = pl.pallas_call(
        functools.partial(_mlstm_kernel, reverse, finalize),
        grid=(bsz, nc), in_specs=in_specs, out_specs=out_specs, out_shape=out_shape,
        compiler_params=_cparams(("arbitrary", "arbitrary")),
        name="mlstm_" + ("bwd" if reverse else "fwd"),
    )(*args)
    if finalize:
        return res[0], tuple(res[1:])
    return res[0], res[1], tuple(res[2:])


def _ssd_kernel(reverse, finalize, sub, *refs):
    if finalize:
        (xc_ref, bcc_ref, gs_ref, dtb_ref, alog_ref, s0_ref, yb_ref, z0_ref, z1_ref, dsk_ref,
         nw_ref, out_ref, sf_ref, y_s) = refs
    else:
        (x0_ref, x0p_ref, x0n_ref, x1_ref, x1p_ref, x1n_ref, bc_ref, bcp_ref, bcn_ref,
         gs_ref, cw_ref, cb_ref, dtb_ref, alog_ref, s0_ref, out_ref, xc_ref, sf_ref) = refs

    i, j, nc = _chunk_pos(reverse)

    @pl.when(i == 0)
    def _():
        sf_ref[...] = s0_ref[...]

    has_prev = j > 0
    has_next = j < nc - 1
    for sc in (reversed(range(sub)) if reverse else range(sub)):
        _ssd_chunk(reverse, finalize, sc, sub, has_prev, has_next, refs)


def _ssd_chunk(reverse, finalize, sc, sub, has_prev, has_next, refs):
    if finalize:
        (xc_ref, bcc_ref, gs_ref, dtb_ref, alog_ref, s0_ref, yb_ref, z0_ref, z1_ref, dsk_ref,
         nw_ref, out_ref, sf_ref, y_s) = refs
    else:
        (x0_ref, x0p_ref, x0n_ref, x1_ref, x1p_ref, x1n_ref, bc_ref, bcp_ref, bcn_ref,
         gs_ref, cw_ref, cb_ref, dtb_ref, alog_ref, s0_ref, out_ref, xc_ref, sf_ref) = refs
    rows = slice(sc * CHUNK, (sc + 1) * CHUNK)
    _, tri, _ = _tri_masks(reverse)
    edge = 0 if reverse else CHUNK - 1

    t_i = lax.broadcasted_iota(jnp.int32, (CHUNK, CHUNK), 0)
    l_i = lax.broadcasted_iota(jnp.int32, (CHUNK, CHUNK), 1)
    s_lo = jnp.where(l_i < HALF, l_i, l_i - HALF)
    lo_half = l_i < HALF
    if reverse:
        mask1, mask2 = s_lo >= t_i, s_lo + HALF >= t_i
    else:
        mask1, mask2 = s_lo <= t_i, s_lo + HALF <= t_i
    k_i = lax.broadcasted_iota(jnp.int32, (2 * CHUNK, CHUNK), 0)
    kl_i = lax.broadcasted_iota(jnp.int32, (2 * CHUNK, CHUNK), 1)
    blockdiag = ((k_i // HALF) % 2 == 1) == (kl_i >= HALF)

    dt_all = _softplus(gs_ref[0, rows, :] + dtb_ref[...])
    da_all = dt_all * (-LOG2E * jnp.exp(alog_ref[...]))
    acum_all = _cumsum_rows(tri, da_all)
    do = H_B if reverse else 0
    acum_t = acum_all.T
    a_even = acum_t[do:do + NPAIR, :]
    a_odd = acum_t[do + NPAIR:do + H_B, :]
    lane16 = lax.broadcasted_iota(jnp.int32, (NPAIR, CHUNK), 1) < HALF
    rv1 = jnp.where(lane16, a_even, pltpu.roll(a_odd, HALF, 1))
    rv2 = jnp.where(lane16, pltpu.roll(a_even, HALF, 1), a_odd)

    if finalize:
        bmat = bcc_ref[0, rows, 0:GP_B]
        cmat = bcc_ref[0, rows, GP_B:2 * GP_B]
        ssq = jnp.zeros((CHUNK, CHUNK), F32)
    else:
        bmat = _conv_block(bc_ref, bcp_ref, bcn_ref, sc, sub, slice(0, GP_B), has_prev, has_next,
                           cw_ref[:, W_B:W_B + GP_B], cb_ref[:, W_B:W_B + GP_B]).astype(BF16)
        cmat = _conv_block(bc_ref, bcp_ref, bcn_ref, sc, sub, slice(GP_B, 2 * GP_B), has_prev, has_next,
                           cw_ref[:, W_B + GP_B:W_B + 2 * GP_B],
                           cb_ref[:, W_B + GP_B:W_B + 2 * GP_B]).astype(BF16)
        xc_ref[0, rows, W_B:W_B + GP_B] = bmat
        xc_ref[0, rows, W_B + GP_B:W_B + 2 * GP_B] = cmat

    for g in range(G_B):
        gs = slice(g * GP_B, (g + 1) * GP_B)
        if finalize:
            xg = xc_ref[0, rows, gs].astype(F32)
        else:
            xr, xpr, xnr = (x0_ref, x0p_ref, x0n_ref) if g < 2 else (x1_ref, x1p_ref, x1n_ref)
            cs = slice((g % 2) * GP_B, (g % 2 + 1) * GP_B)
            xg = _conv_block(xr, xpr, xnr, sc, sub, cs, has_prev, has_next, cw_ref[:, gs], cb_ref[:, gs])
            xc_ref[0, rows, gs] = xg.astype(BF16)
        bg = bmat[:, g * N_B:(g + 1) * N_B]
        cg = cmat[:, g * N_B:(g + 1) * N_B]
        cb1 = _dot_nt(cg, jnp.concatenate([bg[0:HALF], bg[0:HALF]], axis=0))
        cb2 = _dot_nt(cg, jnp.concatenate([bg[HALF:], bg[HALF:]], axis=0))
        s_prev = sf_ref[0, g]
        cs_prev = _dot(cg, s_prev.astype(BF16))
        y_parts, xw_parts, tot_parts = [], [], []
        for pp in range(R_B // 2):
            pr = g * (R_B // 2) + pp
            ps = slice(pp * CHUNK, (pp + 1) * CHUNK)
            idx = jnp.where(lo_half, do + pr, do + NPAIR + pr)
            acum_b = jnp.take_along_axis(acum_all, idx, axis=1)
            dt_b = jnp.take_along_axis(dt_all, idx, axis=1)
            xdt = xg[:, ps] * dt_b
            xdtb = xdt.astype(BF16)
            w1 = cb1 * jnp.exp2(jnp.where(mask1, acum_b - rv1[pr:pr + 1, :], -jnp.inf))
            w2 = cb2 * jnp.exp2(jnp.where(mask2, acum_b - rv2[pr:pr + 1, :], -jnp.inf))
            w12 = jnp.concatenate([w1.astype(BF16), w2.astype(BF16)], axis=1)
            x12 = jnp.concatenate([xdtb[0:HALF], xdtb[0:HALF], xdtb[HALF:], xdtb[HALF:]], axis=0)
            x12 = jnp.where(blockdiag, x12, jnp.zeros_like(x12))
            tot = acum_b[edge:edge + 1, :]
            y_parts.append(_dot(w12, x12) + jnp.exp2(acum_b) * cs_prev[:, ps])
            xw_parts.append((xdt * jnp.exp2(tot - acum_b)).astype(BF16))
            tot_parts.append(tot)
        xw = jnp.concatenate(xw_parts, axis=1)
        etot = jnp.exp2(jnp.concatenate(tot_parts, axis=1))
        sf_ref[0, g] = etot * s_prev + _dot_tn(bg, xw)
        y_grp = jnp.concatenate(y_parts, axis=1)
        if finalize:
            z_ref = z0_ref if g < 2 else z1_ref
            zs = slice((g % 2) * GP_B, (g % 2 + 1) * GP_B)
            y = y_grp + yb_ref[0, rows, gs].astype(F32) + dsk_ref[:, gs] * xg
            y = y * _silu(z_ref[0, rows, zs].astype(F32))
            y_s[:, gs] = y
            y2 = y * y
            ssq = ssq + (y2[:, 0:CHUNK] + y2[:, CHUNK:2 * CHUNK]
                         + y2[:, 2 * CHUNK:3 * CHUNK] + y2[:, 3 * CHUNK:])
        else:
            out_ref[0, rows, gs] = y_grp.astype(out_ref.dtype)

    if finalize:
        inv = lax.rsqrt(jnp.sum(ssq, axis=-1, keepdims=True) * (1.0 / W_B) + LN_EPS)
        for half in range(2):
            cs = slice(half * CW, (half + 1) * CW)
            out_ref[0, rows, cs] = (y_s[:, cs] * inv * nw_ref[:, cs]).astype(out_ref.dtype)


def _ssd_call(reverse, finalize, s0, *, small3, dtb128, alog128, p3, pcol0=0, conv_w=None,
              conv_b=None, xc=None, yb=None, d_skip=None, norm_w=None):
    bsz, t_len, _ = p3.shape
    c_x, c_bc, c_zb = COL_X + pcol0, COL_BC + pcol0, COL_ZB + pcol0
    sub = math.gcd(SUB_SSD, t_len // CHUNK)
    nc, main, prev, nxt, const2 = _chunk_specs(t_len, reverse, sub)
    state_spec = pl.BlockSpec((1, G_B, N_B, GP_B), lambda b, i: (b, 0, 0, 0))
    state_shape = jax.ShapeDtypeStruct(s0.shape, F32)
    xc_spec = main(0, W_B + 2 * GP_B)
    if finalize:
        in_specs = [main(0, W_B), main(2, CW), main(0, N_SMALL), const2((1, N_SMALL)),
                    const2((1, N_SMALL)), state_spec,
                    main(0, W_B), main(c_zb, CW), main(c_zb + 1, CW),
                    const2((1, W_B)), const2((1, W_B))]
        args = [xc, xc, small3, dtb128, alog128, s0, yb, p3, p3, d_skip, norm_w]
        out_specs = [main(0, W_B), state_spec]
        out_shape = [jax.ShapeDtypeStruct((bsz, t_len, W_B), BF16), state_shape]
        scratch = [pltpu.VMEM((CHUNK, W_B), F32)]
    else:
        in_specs = [main(c_x, CW), prev(c_x), nxt(c_x),
                    main(c_x + 1, CW), prev(c_x + 1), nxt(c_x + 1),
                    main(c_bc, CW), prev(c_bc), nxt(c_bc),
                    main(0, N_SMALL),
                    const2((3, 3 * CW)), const2((1, 3 * CW)), const2((1, N_SMALL)),
                    const2((1, N_SMALL)), state_spec]
        args = [p3] * 9 + [small3, conv_w, conv_b, dtb128, alog128, s0]
        out_specs = [main(0, W_B), xc_spec, state_spec]
        out_shape = [jax.ShapeDtypeStruct((bsz, t_len, W_B), HB_DTYPE),
                     jax.ShapeDtypeStruct((bsz, t_len, W_B + 2 * GP_B), BF16), state_shape]
        scratch = []
    return pl.pallas_call(
        functools.partial(_ssd_kernel, reverse, finalize, sub),
        grid=(bsz, nc), in_specs=in_specs, out_specs=out_specs, out_shape=out_shape,
        scratch_shapes=scratch,
        compiler_params=_cparams(("arbitrary", "arbitrary")),
        name="ssd_" + ("bwd" if reverse else "fwd"),
    )(*args)


def _to_colmajor(t, bsz):
    ch = t.shape[-1]
    return t.reshape(bsz, -1, GRID_W, ch).transpose(0, 2, 1, 3).reshape(-1, ch)


def _from_colmajor(t, bsz):
    ch = t.shape[-1]
    return t.reshape(bsz, GRID_W, -1, ch).transpose(0, 2, 1, 3).reshape(-1, ch)


def _mixer_scans(pa, sa, pb, sb, bsz, t_len, lp, states, pcol0=0):
    st_mf, st_mb, st_sf, st_sb = states
    pa = pa.reshape(bsz, t_len, pa.shape[-1])
    sa = sa.reshape(bsz, t_len, N_SMALL)
    h_b, qk, st_mb = _mlstm_call(True, False, st_mb, p3=pa, small3=sa, gate_b128=lp["gate_b128"],
                                 conv_w=lp["conv_qk_w"], conv_b=lp["conv_qk_b"])
    y_a, st_mf = _mlstm_call(False, True, st_mf, p3=pa, small3=sa, gate_b128=lp["gate_b128"],
                             qk=qk, hb=h_b, mh_w=lp["mh_w"])

    pb = pb.reshape(bsz, t_len, pb.shape[-1])
    sb = sb.reshape(bsz, t_len, N_SMALL)
    y_bb, xc, st_sb = _ssd_call(True, False, st_sb, small3=sb, dtb128=lp["dtb128"],
                                alog128=lp["alog128"], p3=pb, pcol0=pcol0,
                                conv_w=lp["conv_xbc_w"], conv_b=lp["conv_xbc_b"])
    y_b, st_sf = _ssd_call(False, True, st_sf, small3=sb, dtb128=lp["dtb128"],
                           alog128=lp["alog128"], p3=pb, pcol0=pcol0, xc=xc, yb=y_bb,
                           d_skip=lp["d_skip"], norm_w=lp["ssm_w"])
    return y_a, y_b, (st_mf, st_mb, st_sf, st_sb)


def _mixer(x2d, bsz, t_len, latent, shift, scale, mod_index, tm, tm_b, w, lp, states,
           cast_layer=None):
    n_a = len(A_STARTS)
    if latent:
        pa, sa, u2d = _in_proj_call(x2d, shift, scale, w, A_STARTS, lp["w_small_a"], mod_index, tm,
                                    tn=2 * CW)
        pb, sb = _proj_call(_to_colmajor(u2d, bsz), w, B_STARTS, lp["w_small_b"], tm_b, tile0=n_a)
        pcol0 = 0
    else:
        w_small = jnp.concatenate([lp["w_small_a"], lp["w_small_b"]], axis=0)
        p, small, _, w = _in_proj_call(x2d, shift, scale, w, A_STARTS + B_STARTS, w_small,
                                       mod_index, tm, cast_layer)
        pa, pb, pcol0 = p, p, n_a
        sa, sb = small[:, :N_SMALL], small[:, N_SMALL:]
    y_a, y_b, states = _mixer_scans(pa, sa, pb, sb, bsz, t_len, lp, states, pcol0)
    y_b = y_b.reshape(bsz * t_len, W_B)
    if latent:
        y_b = _from_colmajor(y_b, bsz)
    return y_a.reshape(bsz * t_len, W_A), y_b, states, w


def _zero_states(bsz):
    m_state = (jnp.zeros((bsz, NH_A, DQK_A, DV_A + CHUNK), F32), jnp.zeros((bsz, 1, 128), F32))
    s_state = jnp.zeros((bsz, G_B, N_B, GP_B), F32)
    return (m_state, m_state, s_state, s_state)


def _layer_params(l, w_t, conv_qk_w, conv_qk_b, gate_b, mh_norm_w, conv_xbc_w, conv_xbc_b,
                  dt_bias, a_log, d_skip, ssm_norm_w):
    feats = lambda a, n: lax.slice(w_t, (l, a, 0), (l + 1, a + n, w_t.shape[2]))[0]

    def pad_cols(a):
        return jnp.concatenate([a, jnp.zeros((a.shape[0], N_SMALL - a.shape[1]), a.dtype)], axis=1)

    def pad_rows(a):
        return jnp.concatenate([a, jnp.zeros((N_SMALL - a.shape[0], a.shape[1]), a.dtype)], axis=0)

    def even_odd(a):
        a4 = a.reshape(a.shape[0], 2, NPAIR, 2)
        return a4.transpose(0, 1, 3, 2).reshape(a.shape[0], 2 * H_B)

    return {
        "w_small_a": pad_rows(feats(O_GATE, 4 * NH_A)),
        "w_small_b": pad_rows(even_odd(feats(O_DT, 2 * H_B).T).T),
        "conv_qk_w": conv_qk_w[l], "conv_qk_b": conv_qk_b[l].reshape(1, -1),
        "gate_b128": pad_cols(gate_b[l].reshape(1, -1)),
        "mh_w": mh_norm_w[l].reshape(1, W_A),
        "conv_xbc_w": conv_xbc_w[l], "conv_xbc_b": conv_xbc_b[l].reshape(1, -1),
        "dtb128": pad_cols(even_odd(dt_bias[l].reshape(1, -1))),
        "alog128": pad_cols(even_odd(a_log[l].reshape(1, -1))),
        "d_skip": jnp.repeat(d_skip[l], P_B).reshape(1, W_B),
        "ssm_w": ssm_norm_w[l].reshape(1, W_B),
    }


def kernel(x, c, ctx, c_ctx, w_ada, b_ada, w_in, conv_qk_w, conv_qk_b, gate_b, mh_norm_w,
           conv_xbc_w, conv_xbc_b, dt_bias, a_log, d_skip, ssm_norm_w, w_out, ln_g, ln_b):
    bsz, t_len, d = x.shape
    ctx_len = ctx.shape[1]
    depth = w_in.shape[0]
    assert d == D_MODEL and depth == DEPTH and bsz + 1 <= 8
    assert t_len % (CHUNK * GRID_W) == 0 and ctx_len % CHUNK == 0

    cvec = jnp.concatenate([c, c_ctx[None, :]], axis=0)
    mod = _ada_call(jnp.broadcast_to(cvec[:, :, None], (bsz + 1, d, 128)), w_ada, b_ada)

    w_t = jnp.swapaxes(w_in, 1, 2)
    w_out_bf = _cast_bf16(w_out, 2048)

    x2d = x.reshape(bsz * t_len, d)
    xc2d = ctx.reshape(bsz * ctx_len, d)
    tm_in, tm_ssd, tm_out, tm_ctx_in, tm_ctx_out = 1024, 2048, 512, bsz * ctx_len, 256
    lat_in = lambda i: i // (t_len // tm_in)
    lat_out = lambda i: i // (t_len // tm_out)
    ctx_index = lambda i: bsz

    for l in range(depth):
        last = l == depth - 1
        lp = _layer_params(l, w_t, conv_qk_w, conv_qk_b, gate_b, mh_norm_w, conv_xbc_w,
                           conv_xbc_b, dt_bias, a_log, d_skip, ssm_norm_w)
        shift = mod[l, :, 0:d].reshape(8, 1, d)
        scale = mod[l, :, d:2 * d].reshape(8, 1, d)
        gate = mod[l, :, 2 * d:3 * d].reshape(8, 1, d)

        yc_a, yc_b, ctx_states, w_bf = _mixer(xc2d, bsz, ctx_len, False, shift, scale, ctx_index,
                                              tm_ctx_in, tm_ctx_in, w_t, lp, _zero_states(bsz),
                                              cast_layer=l)
        y_a, y_b, _, _ = _mixer(x2d, bsz, t_len, True, shift, scale, lat_in, tm_in, tm_ssd, w_bf,
                                lp, ctx_states)
        x2d = _out_proj_call(y_a, y_b, x2d, gate, w_out_bf, l, ln_g[l], ln_b[l], lat_out, tm_out)
        if not last:
            xc2d = _out_proj_call(yc_a, yc_b, xc2d, gate, w_out_bf, l, ln_g[l], ln_b[l],
                                  ctx_index, tm_ctx_out)
    return x2d.reshape(bsz, t_len, d)
```
